```python
import math
import jax, jax.numpy as jnp
from jax import lax
import numpy as np

D_MODEL = 1024
BATCH = 1
SEQ = 16384
DEPTH = 1

HEAD_DIM = 64
A_Q_HEADS = 8
A_KV_HEADS = 2
A_GROUP = A_Q_HEADS // A_KV_HEADS
B_HEADS = 4
B_V_DIM = 2 * HEAD_DIM
Q_BLOCK = 128
GRID_W = 64
AXIAL_THETA = 10000.0
AXIAL_HALF = HEAD_DIM // 2
ROPE_THETA = 500000.0
ROPE_DIMS = HEAD_DIM // 4
NORM_EPS = 1e-6
SUBLN_EPS = 1e-5

A_Q_W = A_Q_HEADS * HEAD_DIM
A_KV_W = A_KV_HEADS * HEAD_DIM
B_QK_W = 2 * B_HEADS * HEAD_DIM
B_V_W = B_HEADS * B_V_DIM
IN_SPLITS = [A_Q_W, A_KV_W, A_KV_W, B_QK_W, B_QK_W, B_V_W]
IN_COLS = sum(IN_SPLITS)
MIX_WIDTH = A_Q_W + B_V_W

PEER_HEADS = 8
PEER_KEYS = 128
PEER_EXPERTS = PEER_KEYS * PEER_KEYS
PEER_QUERY_DIM = 256
PEER_HALF = PEER_QUERY_DIM // 2
PEER_TOPK = 16
TOKEN_CHUNK = 128

kernel_name = "hybrid_gqa_diffattn_peer_encoder"


def _rms_norm(x, g, eps=NORM_EPS):
    xf = x.astype(jnp.float32)
    y = xf * lax.rsqrt(jnp.mean(xf * xf, axis=-1, keepdims=True) + eps)
    return (y * g.astype(jnp.float32)).astype(x.dtype)


def _rotate(x, ang):
    xf = x.astype(jnp.float32)
    x1, x2 = jnp.split(xf, 2, axis=-1)
    c, s = jnp.cos(ang), jnp.sin(ang)
    return jnp.concatenate([x1 * c - x2 * s, x2 * c + x1 * s], axis=-1).astype(x.dtype)


def _axial_rope(x, row_ang, col_ang):
    return jnp.concatenate([_rotate(x[..., :AXIAL_HALF], row_ang),
                            _rotate(x[..., AXIAL_HALF:], col_ang)], axis=-1)


def _partial_rope(x, ang):
    return jnp.concatenate([_rotate(x[..., :ROPE_DIMS], ang), x[..., ROPE_DIMS:]], axis=-1)


def _gqa_attention(q, k, v):
    B, Hk, G, S, d = q.shape
    nb = S // Q_BLOCK
    scale = d ** -0.5
    qb = jnp.moveaxis(q.reshape(B, Hk, G, nb, Q_BLOCK, d), 3, 0)

    def one(qi):
        s = jnp.einsum('bhgqd,bhkd->bhgqk', qi, k).astype(jnp.float32) * scale
        p = jax.nn.softmax(s, axis=-1)
        return jnp.einsum('bhgqk,bhkv->bhgqv', p.astype(v.dtype), v)

    o = lax.map(one, qb)
    return jnp.moveaxis(o, 0, 3).reshape(B, Hk, G, S, v.shape[-1])


def _diff_attention(q, k, v, lam):
    B, H, _, S, d = q.shape
    nb = S // Q_BLOCK
    scale = d ** -0.5
    qb = jnp.moveaxis(q.reshape(B, H, 2, nb, Q_BLOCK, d), 3, 0)

    def one(qi):
        s = jnp.einsum('bhcqd,bhckd->bhcqk', qi, k).astype(jnp.float32) * scale
        p = jax.nn.softmax(s, axis=-1)
        w = p[:, :, 0] - lam * p[:, :, 1]
        return jnp.einsum('bhqk,bhkv->bhqv', w.astype(v.dtype), v)

    o = lax.map(one, qb)
    return jnp.moveaxis(o, 0, 2).reshape(B, H, S, v.shape[-1])


def _peer(xn, w_query, sub_keys, expert_u, expert_v):
    B, S, D = xn.shape
    xc_all = xn.reshape(B * S // TOKEN_CHUNK, TOKEN_CHUNK, D)

    def one(xc):
        C = xc.shape[0]
        q = (xc @ w_query).reshape(C, PEER_HEADS, 2, PEER_HALF)
        sc = jnp.einsum('chpd,hpnd->chpn', q, sub_keys).astype(jnp.float32)
        v1, i1 = lax.top_k(sc[:, :, 0], PEER_TOPK)
        v2, i2 = lax.top_k(sc[:, :, 1], PEER_TOPK)
        cand = (v1[..., :, None] + v2[..., None, :]).reshape(C, PEER_HEADS, PEER_TOPK * PEER_TOPK)
        cand_idx = (i1[..., :, None] * PEER_KEYS + i2[..., None, :]).reshape(C, PEER_HEADS, PEER_TOPK * PEER_TOPK)
        top, pos = lax.top_k(cand, PEER_TOPK)
        idx = jnp.take_along_axis(cand_idx, pos, axis=-1)
        g = jax.nn.softmax(top, axis=-1)
        u = expert_u[idx]
        v = expert_v[idx]
        a = jax.nn.gelu(jnp.einsum('cd,chkd->chk', xc, u), approximate=False)
        return jnp.einsum('chk,chkd->cd', (g * a.astype(jnp.float32)).astype(v.dtype), v)

    return lax.map(one, xc_all).reshape(B, S, D)


def setup_inputs(seed: int = 0) -> dict:
    key = jax.random.key(seed)
    ks = jax.random.split(key, 20)
    f32 = jnp.float32
    L = DEPTH

    def nrm(k, shape, scale):
        return jax.random.normal(k, shape, f32) * scale

    def gain(k, shape):
        return 1.0 + 0.02 * jax.random.normal(k, shape, f32)

    return {
        "x": jax.random.normal(ks[0], (BATCH, SEQ, D_MODEL), f32),
        "norm_attn_g": gain(ks[1], (L, D_MODEL)),
        "w_in": nrm(ks[2], (L, D_MODEL, IN_COLS), D_MODEL ** -0.5),
        "q_norm_g": gain(ks[3], (L, HEAD_DIM)),
        "k_norm_g": gain(ks[4], (L, HEAD_DIM)),
        "lambda_q1": nrm(ks[5], (L, HEAD_DIM), 0.1),
        "lambda_k1": nrm(ks[6], (L, HEAD_DIM), 0.1),
        "lambda_q2": nrm(ks[7], (L, HEAD_DIM), 0.1),
        "lambda_k2": nrm(ks[8], (L, HEAD_DIM), 0.1),
        "subln_g": gain(ks[9], (L, B_V_DIM)),
        "w_out": nrm(ks[10], (L, MIX_WIDTH, D_MODEL), MIX_WIDTH ** -0.5),
        "norm_ffn_g": gain(ks[11], (L, D_MODEL)),
        "w_query": nrm(ks[12], (L, D_MODEL, PEER_HEADS * PEER_QUERY_DIM), D_MODEL ** -0.5),
        "sub_keys": nrm(ks[13], (L, PEER_HEADS, 2, PEER_KEYS, PEER_HALF), PEER_HALF ** -0.5),
        "expert_u": nrm(ks[14], (L, PEER_EXPERTS, D_MODEL), D_MODEL ** -0.5),
        "expert_v": nrm(ks[15], (L, PEER_EXPERTS, D_MODEL), 0.25),
        "norm_final_g": gain(ks[16], (D_MODEL,)),
    }


def reference(x, norm_attn_g, w_in, q_norm_g, k_norm_g, lambda_q1, lambda_k1,
              lambda_q2, lambda_k2, subln_g, w_out, norm_ffn_g, w_query, sub_keys,
              expert_u, expert_v, norm_final_g):
    B, S, _ = x.shape
    rows = S // GRID_W
    row = jnp.repeat(jnp.arange(rows, dtype=jnp.float32), GRID_W)
    col = jnp.tile(jnp.arange(GRID_W, dtype=jnp.float32), rows)
    pos = jnp.arange(S, dtype=jnp.float32)
    inv_ax = AXIAL_THETA ** (-jnp.arange(0, AXIAL_HALF, 2, dtype=jnp.float32) / AXIAL_HALF)
    inv_p = ROPE_THETA ** (-jnp.arange(0, ROPE_DIMS, 2, dtype=jnp.float32) / ROPE_DIMS)
    row_ang = row[:, None] * inv_ax[None, :]
    col_ang = col[:, None] * inv_ax[None, :]
    pos_ang = pos[:, None] * inv_p[None, :]
    split_pts = [int(v) for v in np.cumsum(IN_SPLITS)[:-1]]

    h = x
    for l in range(DEPTH):
        lambda_init = 0.8 - 0.6 * math.exp(-0.3 * l)
        xn = _rms_norm(h, norm_attn_g[l])
        proj = xn @ w_in[l]
        qa, ka, va, qb, kb, vb = jnp.split(proj, split_pts, axis=-1)

        qa = qa.reshape(B, S, A_Q_HEADS, HEAD_DIM).transpose(0, 2, 1, 3)
        ka = ka.reshape(B, S, A_KV_HEADS, HEAD_DIM).transpose(0, 2, 1, 3)
        va = va.reshape(B, S, A_KV_HEADS, HEAD_DIM).transpose(0, 2, 1, 3)
        qa = _axial_rope(_rms_norm(qa, q_norm_g[l]), row_ang, col_ang)
        ka = _axial_rope(_rms_norm(ka, k_norm_g[l]), row_ang, col_ang)
        qa = qa.reshape(B, A_KV_HEADS, A_GROUP, S, HEAD_DIM)
        oa = _gqa_attention(qa, ka, va).reshape(B, A_Q_HEADS, S, HEAD_DIM)
        oa = oa.transpose(0, 2, 1, 3).reshape(B, S, A_Q_W)

        qb = qb.reshape(B, S, B_HEADS, 2, HEAD_DIM).transpose(0, 2, 3, 1, 4)
        kb = kb.reshape(B, S, B_HEADS, 2, HEAD_DIM).transpose(0, 2, 3, 1, 4)
        vb = vb.reshape(B, S, B_HEADS, B_V_DIM).transpose(0, 2, 1, 3)
        qb = _partial_rope(qb, pos_ang)
        kb = _partial_rope(kb, pos_ang)
        lam = (jnp.exp(jnp.sum(lambda_q1[l].astype(jnp.float32) * lambda_k1[l].astype(jnp.float32)))
               - jnp.exp(jnp.sum(lambda_q2[l].astype(jnp.float32) * lambda_k2[l].astype(jnp.float32)))
               + lambda_init)
        ob = _diff_attention(qb, kb, vb, lam)
        ob = _rms_norm(ob, subln_g[l], SUBLN_EPS) * (1.0 - lambda_init)
        ob = ob.transpose(0, 2, 1, 3).reshape(B, S, B_V_W)

        h = h + jnp.concatenate([oa, ob], axis=-1) @ w_out[l]

        h = h + _peer(_rms_norm(h, norm_ffn_g[l]), w_query[l], sub_keys[l],
                      expert_u[l], expert_v[l])

    return _rms_norm(h, norm_final_g)
```

```python
import functools
import math

import jax
import jax.numpy as jnp
from jax import lax
from jax.experimental import pallas as pl
from jax.experimental.pallas import tpu as pltpu

F32 = jnp.float32
BF16 = jnp.bfloat16

D_MODEL = 1024
HEAD_DIM = 64
A_Q_HEADS = 8
A_KV_HEADS = 2
A_GROUP = A_Q_HEADS // A_KV_HEADS
B_HEADS = 4
B_V_DIM = 2 * HEAD_DIM
GRID_W = 64
AXIAL_THETA = 10000.0
AXIAL_HALF = HEAD_DIM // 2
ROPE_THETA = 500000.0
ROPE_DIMS = HEAD_DIM // 4
NORM_EPS = 1e-6
SUBLN_EPS = 1e-5
A_Q_W = A_Q_HEADS * HEAD_DIM
A_KV_W = A_KV_HEADS * HEAD_DIM
B_QK_W = 2 * B_HEADS * HEAD_DIM
B_V_W = B_HEADS * B_V_DIM
IN_COLS = A_Q_W + 2 * A_KV_W + 2 * B_QK_W + B_V_W
PEER_HEADS = 8
PEER_KEYS = 128
PEER_HALF = 128
PEER_TOPK = 16
LANES = 128

NT_DIMS = (((1,), (1,)), ((), ()))

VMEM_LIMIT = 56 * 1024 * 1024


def _cparams(sem):
    return pltpu.CompilerParams(dimension_semantics=sem, vmem_limit_bytes=VMEM_LIMIT)


def _rms(x, g, eps):
    return x * lax.rsqrt(jnp.mean(x * x, axis=-1, keepdims=True) + eps) * g


def _group_mean_sq(v, bd):
    v2 = v * v
    hi = v2.astype(BF16)
    lo = (v2 - hi.astype(F32)).astype(BF16)
    return (jnp.dot(hi, bd, preferred_element_type=F32)
            + jnp.dot(lo, bd, preferred_element_type=F32))


def _rot_half(v, half, group):
    width = v.shape[-1]
    lane = lax.broadcasted_iota(jnp.int32, v.shape, 1)
    fwd = pltpu.roll(v, width - half, 1)
    bwd = pltpu.roll(v, half, 1)
    return jnp.where((lane % group) < half, fwd, bwd)


def _tile4(t):
    return jnp.concatenate([t, t, t, t], axis=1)


def _proj_kernel(x_ref, g_ref, w_ref, qg_ref, kg_ref, tab_ref, bd_ref,
                 qa_ref, ka_ref, va_ref, qb_ref, kb_ref, vb_ref):
    xn = _rms(x_ref[...], g_ref[...], NORM_EPS)
    proj = jnp.dot(xn.astype(BF16), w_ref[...], preferred_element_type=F32)
    tab = tab_ref[...]
    cos_a, sin_a = tab[:, 0:128], tab[:, 128:256]
    cos_b, sin_b = tab[:, 256:384], tab[:, 384:512]
    bd = bd_ref[...]

    c0 = 0
    qa = proj[:, c0:c0 + A_Q_W]
    qa = qa * lax.rsqrt(_group_mean_sq(qa, bd) + NORM_EPS) * qg_ref[...]
    qa = qa * _tile4(cos_a) + _rot_half(qa, AXIAL_HALF // 2, AXIAL_HALF) * _tile4(sin_a)
    qa_ref[...] = qa.astype(BF16)
    c0 += A_Q_W

    ka = proj[:, c0:c0 + A_KV_W]
    ka = ka * lax.rsqrt(_group_mean_sq(ka, bd[:A_KV_W, :A_KV_W]) + NORM_EPS) * kg_ref[...]
    ka = ka * cos_a + _rot_half(ka, AXIAL_HALF // 2, AXIAL_HALF) * sin_a
    ka_ref[...] = ka.astype(BF16)
    c0 += A_KV_W

    va_ref[...] = proj[:, c0:c0 + A_KV_W].astype(BF16)
    c0 += A_KV_W

    qb = proj[:, c0:c0 + B_QK_W]
    qb = qb * _tile4(cos_b) + _rot_half(qb, ROPE_DIMS // 2, HEAD_DIM) * _tile4(sin_b)
    qb_ref[...] = (qb * (HEAD_DIM ** -0.5)).astype(BF16)
    c0 += B_QK_W

    kb = proj[:, c0:c0 + B_QK_W]
    kb = kb * _tile4(cos_b) + _rot_half(kb, ROPE_DIMS // 2, HEAD_DIM) * _tile4(sin_b)
    kb_ref[...] = kb.astype(BF16)
    c0 += B_QK_W

    vb_ref[...] = proj[:, c0:c0 + B_V_W].astype(BF16)


def _rope_tables(seq):
    rows = seq // GRID_W
    row = jnp.repeat(jnp.arange(rows, dtype=F32), GRID_W)
    col = jnp.tile(jnp.arange(GRID_W, dtype=F32), rows)
    pos = jnp.arange(seq, dtype=F32)
    inv_ax = AXIAL_THETA ** (-jnp.arange(0, AXIAL_HALF, 2, dtype=F32) / AXIAL_HALF)
    inv_p = ROPE_THETA ** (-jnp.arange(0, ROPE_DIMS, 2, dtype=F32) / ROPE_DIMS)
    row_ang = row[:, None] * inv_ax[None, :]
    col_ang = col[:, None] * inv_ax[None, :]
    pos_ang = pos[:, None] * inv_p[None, :]
    cr, sr = jnp.cos(row_ang), jnp.sin(row_ang)
    cc, sc = jnp.cos(col_ang), jnp.sin(col_ang)
    cp, sp = jnp.cos(pos_ang), jnp.sin(pos_ang)
    rest = HEAD_DIM - ROPE_DIMS
    cos_a = jnp.concatenate([cr, cr, cc, cc], axis=1)
    sin_a = jnp.concatenate([-sr, sr, -sc, sc], axis=1)
    cos_b = jnp.concatenate([cp, cp, jnp.ones((seq, rest), F32)], axis=1)
    sin_b = jnp.concatenate([-sp, sp, jnp.zeros((seq, rest), F32)], axis=1)
    two = lambda t: jnp.concatenate([t, t], axis=1)
    return jnp.concatenate([two(cos_a), two(sin_a), two(cos_b), two(sin_b)], axis=1)


def _proj_call(x, g, w_in, qg, kg, tab, bd, tm):
    seq = x.shape[0]
    row = lambda i: (i, 0)
    fix = lambda i: (0, 0)
    out_w = [A_Q_W, A_KV_W, A_KV_W, B_QK_W, B_QK_W, B_V_W]
    return pl.pallas_call(
        _proj_kernel,
        grid=(seq // tm,),
        in_specs=[
            pl.BlockSpec((tm, D_MODEL), row),
            pl.BlockSpec((1, D_MODEL), fix),
            pl.BlockSpec((D_MODEL, IN_COLS), fix),
            pl.BlockSpec((1, A_Q_W), fix),
            pl.BlockSpec((1, A_KV_W), fix),
            pl.BlockSpec((tm, 512), row),
            pl.BlockSpec((A_Q_W, A_Q_W), fix),
        ],
        out_specs=[pl.BlockSpec((tm, w), row) for w in out_w],
        out_shape=[jax.ShapeDtypeStruct((seq, w), BF16) for w in out_w],
        compiler_params=_cparams(("parallel",)),
        name="proj",
    )(x, g, w_in, qg, kg, tab, bd)


def _flash_rows(qs, k_ref, v_ref, tk):
    rows = qs.shape[0]
    seq = k_ref.shape[0]

    def body(j, carry):
        m, l, acc = carry
        off = pl.multiple_of(j * tk, tk)
        k = k_ref[pl.ds(off, tk), :]
        v = v_ref[pl.ds(off, tk), :]
        s = lax.dot_general(qs, k, NT_DIMS, preferred_element_type=F32)
        m_new = jnp.maximum(m, jnp.max(s, axis=-1, keepdims=True))
        alpha = jnp.exp(m - m_new)
        p = jnp.exp(s - m_new)
        l = alpha * l + jnp.sum(p, axis=-1, keepdims=True)
        acc = alpha * acc + jnp.dot(p.astype(BF16), v, preferred_element_type=F32)
        return m_new, l, acc

    init = (jnp.full((rows, 1), -jnp.inf, F32), jnp.zeros((rows, 1), F32),
            jnp.zeros((rows, LANES), F32))
    _, l, acc = lax.fori_loop(0, seq // tk, body, init)
    return acc, l


def _gqa_kernel(q_ref, k_ref, v_ref, o_ref, *, tq, tk):
    g = pl.program_id(0)
    lane = lax.broadcasted_iota(jnp.int32, (tq, LANES), 1)
    in_g = (lane // HEAD_DIM) == g
    rows = []
    for hh in range(A_GROUP):
        qp = q_ref[:, LANES * (hh // 2):LANES * (hh // 2 + 1)].astype(F32)
        aligned = jnp.where((hh % 2) == g, qp, pltpu.roll(qp, HEAD_DIM, 1))
        rows.append(jnp.where(in_g, aligned, 0.0).astype(BF16))
    qs = jnp.concatenate(rows, axis=0)
    acc, l = _flash_rows(qs, k_ref, v_ref, tk)
    o = acc / l
    left_half = lane < HEAD_DIM
    for p in range(A_GROUP // 2):
        a = o[(2 * p) * tq:(2 * p + 1) * tq]
        b = o[(2 * p + 1) * tq:(2 * p + 2) * tq]
        left = jnp.where(g == 0, a, pltpu.roll(a, HEAD_DIM, 1))
        right = jnp.where(g == 0, pltpu.roll(b, HEAD_DIM, 1), b)
        o_ref[:, LANES * p:LANES * (p + 1)] = jnp.where(left_half, left, right).astype(BF16)


def _gqa_call(qa, ka, va, tq, tk):
    seq = qa.shape[0]
    return pl.pallas_call(
        functools.partial(_gqa_kernel, tq=tq, tk=tk),
        grid=(A_KV_HEADS, seq // tq),
        in_specs=[
            pl.BlockSpec((tq, A_GROUP * HEAD_DIM), lambda g, i: (i, g)),
            pl.BlockSpec((seq, A_KV_W), lambda g, i: (0, 0)),
            pl.BlockSpec((seq, A_KV_W), lambda g, i: (0, 0)),
        ],
        out_specs=pl.BlockSpec((tq, A_GROUP * HEAD_DIM), lambda g, i: (i, g)),
        out_shape=jax.ShapeDtypeStruct((seq, A_Q_W), BF16),
        compiler_params=_cparams(("parallel", "parallel")),
        name="gqa",
    )(qa, ka, va)


def _diff_kernel(lam_ref, q_ref, k_ref, v_ref, sg_ref, o_ref, *, tq, tk, lambda_init):
    lane = lax.broadcasted_iota(jnp.int32, (tq, LANES), 1)
    q = q_ref[...]
    zero = jnp.zeros_like(q)
    qs = jnp.concatenate([jnp.where(lane < HEAD_DIM, q, zero),
                          jnp.where(lane >= HEAD_DIM, q, zero)], axis=0)
    acc, l = _flash_rows(qs, k_ref, v_ref, tk)
    o = acc / l
    lv = lam_ref[...]
    lam = (jnp.exp(jnp.sum(lv[0:1] * lv[1:2], axis=-1, keepdims=True))
           - jnp.exp(jnp.sum(lv[2:3] * lv[3:4], axis=-1, keepdims=True)) + lambda_init)
    ob = o[:tq] - lam * o[tq:]
    ob = _rms(ob, sg_ref[...], SUBLN_EPS) * (1.0 - lambda_init)
    o_ref[...] = ob.astype(BF16)


def _diff_call(lam_vecs, qb, kb, vb, subln_g, tq, tk, lambda_init):
    seq = qb.shape[0]
    return pl.pallas_call(
        functools.partial(_diff_kernel, tq=tq, tk=tk, lambda_init=lambda_init),
        grid=(B_HEADS, seq // tq),
        in_specs=[
            pl.BlockSpec((4, HEAD_DIM), lambda h, i: (0, 0)),
            pl.BlockSpec((tq, LANES), lambda h, i: (i, h)),
            pl.BlockSpec((seq, LANES), lambda h, i: (0, h)),
            pl.BlockSpec((seq, LANES), lambda h, i: (0, h)),
            pl.BlockSpec((1, B_V_DIM), lambda h, i: (0, 0)),
        ],
        out_specs=pl.BlockSpec((tq, LANES), lambda h, i: (i, h)),
        out_shape=jax.ShapeDtypeStruct((seq, B_V_W), BF16),
        compiler_params=_cparams(("parallel", "parallel")),
        name="diff",
    )(lam_vecs, qb, kb, vb, subln_g)


def _mid_kernel(oa_ref, ob_ref, x_ref, wo_ref, g_ref, wq_ref, sk_ref, h_ref, xn_ref, sc_ref):
    o = jnp.concatenate([oa_ref[...], ob_ref[...]], axis=1)
    h = x_ref[...] + jnp.dot(o, wo_ref[...], preferred_element_type=F32)
    h_ref[...] = h
    xn = _rms(h, g_ref[...], NORM_EPS).astype(BF16)
    xn_ref[...] = xn
    q = jnp.dot(xn, wq_ref[...], preferred_element_type=F32).astype(BF16)
    for hp in range(2 * PEER_HEADS):
        sc_ref[hp] = lax.dot_general(sk_ref[hp], q[:, PEER_HALF * hp:PEER_HALF * (hp + 1)],
                                     NT_DIMS, preferred_element_type=F32)


def _mid_call(oa, ob, x, w_out, g, w_query, sub_keys, tm):
    seq = x.shape[0]
    row = lambda i: (i, 0)
    fix = lambda i: (0, 0)
    nq = 2 * PEER_HEADS * PEER_HALF
    return pl.pallas_call(
        _mid_kernel,
        grid=(seq // tm,),
        in_specs=[
            pl.BlockSpec((tm, A_Q_W), row),
            pl.BlockSpec((tm, B_V_W), row),
            pl.BlockSpec((tm, D_MODEL), row),
            pl.BlockSpec((D_MODEL, D_MODEL), fix),
            pl.BlockSpec((1, D_MODEL), fix),
            pl.BlockSpec((D_MODEL, nq), fix),
            pl.BlockSpec((2 * PEER_HEADS, PEER_KEYS, PEER_HALF), lambda i: (0, 0, 0)),
        ],
        out_specs=[
            pl.BlockSpec((tm, D_MODEL), row),
            pl.BlockSpec((tm, D_MODEL), row),
            pl.BlockSpec((2 * PEER_HEADS, PEER_KEYS, tm), lambda i: (0, 0, i)),
        ],
        out_shape=[
            jax.ShapeDtypeStruct((seq, D_MODEL), F32),
            jax.ShapeDtypeStruct((seq, D_MODEL), BF16),
            jax.ShapeDtypeStruct((2 * PEER_HEADS, PEER_KEYS, seq), F32),
        ],
        compiler_params=_cparams(("parallel",)),
        name="mid",
    )(oa, ob, x, w_out, g, w_query, sub_keys)


N_TOP = PEER_TOPK + 1
_CAND = [(a, b) for a in range(N_TOP) for b in range(N_TOP) if (a + 1) * (b + 1) <= N_TOP]


def _top_vals(s, n):
    vals = []
    for r in range(n):
        m = jnp.max(s, axis=0, keepdims=True)
        vals.append(m)
        if r + 1 < n:
            s = jnp.where(s >= m, -jnp.inf, s)
    return vals


def _topk_kernel(sc_ref, e1_ref, e2_ref, tau_ref):
    t = sc_ref.shape[-1]
    taus = []
    for h in range(PEER_HEADS):
        s1 = sc_ref[2 * h]
        s2 = sc_ref[2 * h + 1]
        v1 = _top_vals(s1, N_TOP)
        v2 = _top_vals(s2, N_TOP)
        cands = [v1[a] + v2[b] for (a, b) in _CAND]
        pad = (-len(cands)) % 8
        cands += [jnp.full((1, t), -jnp.inf, F32)] * pad
        c = jnp.concatenate(cands, axis=0)
        top = _top_vals(c, N_TOP)
        t16, t17 = top[PEER_TOPK - 1], top[PEER_TOPK]
        best = v1[0] + v2[0]
        z = jnp.sum(jnp.where(c >= t16, jnp.exp(c - best), 0.0), axis=0, keepdims=True)
        rz = 1.0 / z
        e1_ref[h] = jnp.exp(s1 - v1[0]) * rz
        e2_ref[h] = jnp.exp(s2 - v2[0])
        taus.append(0.5 * (jnp.exp(t16 - best) + jnp.exp(t17 - best)) * rz)
    tau_ref[...] = jnp.concatenate(taus, axis=0)


def _topk_call(sc, tt):
    seq = sc.shape[-1]
    blk = lambda i: (0, 0, i)
    return pl.pallas_call(
        _topk_kernel,
        grid=(seq // tt,),
        in_specs=[pl.BlockSpec((2 * PEER_HEADS, PEER_KEYS, tt), blk)],
        out_specs=[
            pl.BlockSpec((PEER_HEADS, PEER_KEYS, tt), blk),
            pl.BlockSpec((PEER_HEADS, PEER_KEYS, tt), blk),
            pl.BlockSpec((PEER_HEADS, tt), lambda i: (0, i)),
        ],
        out_shape=[
            jax.ShapeDtypeStruct((PEER_HEADS, PEER_KEYS, seq), F32),
            jax.ShapeDtypeStruct((PEER_HEADS, PEER_KEYS, seq), F32),
            jax.ShapeDtypeStruct((PEER_HEADS, seq), F32),
        ],
        compiler_params=_cparams(("parallel",)),
        name="topk",
    )(sc)


def _peer_kernel(xn_ref, u_ref, vt_ref, e1_ref, e2_ref, tau_ref, h_ref, g_ref, o_ref,
                 acc_ref, w_ref, *, eb):
    e = pl.program_id(1)
    xn = xn_ref[...]
    inv_sqrt2 = 1.0 / math.sqrt(2.0)
    for ii in range(eb // PEER_KEYS):
        i = e * (eb // PEER_KEYS) + ii
        a = lax.dot_general(u_ref[PEER_KEYS * ii:PEER_KEYS * (ii + 1), :], xn, NT_DIMS,
                            preferred_element_type=F32)
        act = 0.5 * a * (1.0 + lax.erf(a * inv_sqrt2))
        gate = jnp.zeros_like(a)
        for h in range(PEER_HEADS):
            p = e2_ref[h] * e1_ref[h, pl.ds(i, 1), :]
            gate = gate + jnp.where(p >= tau_ref[h:h + 1, :], p, 0.0)
        w_ref[PEER_KEYS * ii:PEER_KEYS * (ii + 1), :] = (gate * act).astype(BF16)
    part = jnp.dot(vt_ref[...], w_ref[...], preferred_element_type=F32)

    @pl.when(e == 0)
    def _():
        acc_ref[...] = part

    @pl.when(e > 0)
    def _():
        acc_ref[...] += part

    @pl.when(e == pl.num_programs(1) - 1)
    def _():
        out = h_ref[...] + acc_ref[...].T
        o_ref[...] = _rms(out, g_ref[...], NORM_EPS)


def _peer_call(xn, u, vt, e1, e2, tau, h, g, tt, eb):
    seq = xn.shape[0]
    n_exp = u.shape[0]
    tok = lambda t, e: (t, 0)
    tok3 = lambda t, e: (0, 0, t)
    return pl.pallas_call(
        functools.partial(_peer_kernel, eb=eb),
        grid=(seq // tt, n_exp // eb),
        in_specs=[
            pl.BlockSpec((tt, D_MODEL), tok),
            pl.BlockSpec((eb, D_MODEL), lambda t, e: (e, 0)),
            pl.BlockSpec((D_MODEL, eb), lambda t, e: (0, e)),
            pl.BlockSpec((PEER_HEADS, PEER_KEYS, tt), tok3),
            pl.BlockSpec((PEER_HEADS, PEER_KEYS, tt), tok3),
            pl.BlockSpec((PEER_HEADS, tt), lambda t, e: (0, t)),
            pl.BlockSpec((tt, D_MODEL), tok),
            pl.BlockSpec((1, D_MODEL), lambda t, e: (0, 0)),
        ],
        out_specs=pl.BlockSpec((tt, D_MODEL), tok),
        out_shape=jax.ShapeDtypeStruct((seq, D_MODEL), F32),
        scratch_shapes=[pltpu.VMEM((D_MODEL, tt), F32), pltpu.VMEM((eb, tt), BF16)],
        compiler_params=_cparams(("parallel", "arbitrary")),
        name="peer",
    )(xn, u, vt, e1, e2, tau, h, g)


def _tiles(seq):
    big = seq >= 4096
    return dict(
        tm=512 if big else 256,
        tq_a=128, tq_b=256, tk=512 if big else 256,
        tt_topk=256, tt_peer=512 if big else 256, eb=512,
    )


def kernel(x, norm_attn_g, w_in, q_norm_g, k_norm_g, lambda_q1, lambda_k1, lambda_q2, lambda_k2,
           subln_g, w_out, norm_ffn_g, w_query, sub_keys, expert_u, expert_v, norm_final_g):
    batch, seq, d = x.shape
    assert batch == 1 and d == D_MODEL and norm_attn_g.shape[0] == 1
    t = _tiles(seq)
    lambda_init = 0.8 - 0.6 * math.exp(-0.3 * 0)
    x2 = x.reshape(seq, d)

    tab = _rope_tables(seq)
    blk = jnp.arange(A_Q_W) // HEAD_DIM
    bd = jnp.where(blk[:, None] == blk[None, :], 1.0 / HEAD_DIM, 0.0).astype(BF16)
    qg = (jnp.tile(q_norm_g[0], A_Q_HEADS) * (HEAD_DIM ** -0.5)).reshape(1, A_Q_W)
    kg = jnp.tile(k_norm_g[0], A_KV_HEADS).reshape(1, A_KV_W)

    qa, ka, va, qb, kb, vb = _proj_call(x2, norm_attn_g, w_in[0].astype(BF16), qg, kg, tab, bd, t["tm"])
    oa = _gqa_call(qa, ka, va, t["tq_a"], t["tk"])
    lam_vecs = jnp.concatenate([lambda_q1, lambda_k1, lambda_q2, lambda_k2], axis=0)
    ob = _diff_call(lam_vecs, qb, kb, vb, subln_g, t["tq_b"], t["tk"], lambda_init)

    sk = sub_keys[0].reshape(2 * PEER_HEADS, PEER_KEYS, PEER_HALF).astype(BF16)
    h, xn, sc = _mid_call(oa, ob, x2, w_out[0].astype(BF16), norm_ffn_g, w_query[0].astype(BF16), sk, t["tm"])
    e1, e2, tau = _topk_call(sc, t["tt_topk"])
    out = _peer_call(xn, expert_u[0].astype(BF16), expert_v[0].T.astype(BF16), e1, e2, tau, h,
                     norm_final_g.reshape(1, d), t["tt_peer"], t["eb"])
    return out.reshape(batch, seq, d)
```

```python
import functools
import math

import jax
import jax.numpy as jnp
from jax import lax
from jax.experimental import pallas as pl
from jax.experimental.pallas import tpu as pltpu

F32 = jnp.float32
BF16 = jnp.bfloat16

D_MODEL = 1024
HEAD_DIM = 64
A_Q_HEADS = 8
A_KV_HEADS = 2
A_GROUP = A_Q_HEADS // A_KV_HEADS
B_HEADS = 4
B_V_DIM = 2 * HEAD_DIM
GRID_W = 64
AXIAL_THETA = 10000.0
AXIAL_HALF = HEAD_DIM // 2
ROPE_THETA = 500000.0
ROPE_DIMS = HEAD_DIM // 4
NORM_EPS = 1e-6
SUBLN_EPS = 1e-5
A_Q_W = A_Q_HEADS * HEAD_DIM
A_KV_W = A_KV_HEADS * HEAD_DIM
B_QK_W = 2 * B_HEADS * HEAD_DIM
B_V_W = B_HEADS * B_V_DIM
IN_COLS = A_Q_W + 2 * A_KV_W + 2 * B_QK_W + B_V_W
PEER_HEADS = 8
PEER_KEYS = 128
PEER_HALF = 128
PEER_TOPK = 16
LANES = 128

NT_DIMS = (((1,), (1,)), ((), ()))
Q_SCALE = (HEAD_DIM ** -0.5) * math.log2(math.e)

VMEM_LIMIT = 56 * 1024 * 1024


def _cparams(sem):
    return pltpu.CompilerParams(dimension_semantics=sem, vmem_limit_bytes=VMEM_LIMIT)


def _rms(x, g, eps):
    return x * lax.rsqrt(jnp.mean(x * x, axis=-1, keepdims=True) + eps) * g


def _group_mean_sq(v, bd):
    v2 = v * v
    hi = v2.astype(BF16)
    lo = (v2 - hi.astype(F32)).astype(BF16)
    return (jnp.dot(hi, bd, preferred_element_type=F32)
            + jnp.dot(lo, bd, preferred_element_type=F32))


def _rot_half(v, half, group):
    width = v.shape[-1]
    lane = lax.broadcasted_iota(jnp.int32, v.shape, 1)
    fwd = pltpu.roll(v, width - half, 1)
    bwd = pltpu.roll(v, half, 1)
    return jnp.where((lane % group) < half, fwd, bwd)


def _tile4(t):
    return jnp.concatenate([t, t, t, t], axis=1)


def _proj_kernel(x_ref, g_ref, w_ref, qg_ref, kg_ref, tab_ref, bd_ref,
                 qa_ref, ka_ref, va_ref, qb_ref, kb_ref, vb_ref):
    xn = _rms(x_ref[...], g_ref[...], NORM_EPS)
    proj = jnp.dot(xn.astype(BF16), w_ref[...], preferred_element_type=F32)
    tab = tab_ref[...]
    cos_a, sin_a = tab[:, 0:128], tab[:, 128:256]
    cos_b, sin_b = tab[:, 256:384], tab[:, 384:512]
    bd = bd_ref[...]

    c0 = 0
    qa = proj[:, c0:c0 + A_Q_W]
    qa = qa * lax.rsqrt(_group_mean_sq(qa, bd) + NORM_EPS) * qg_ref[...]
    qa = qa * _tile4(cos_a) + _rot_half(qa, AXIAL_HALF // 2, AXIAL_HALF) * _tile4(sin_a)
    qa_ref[...] = qa.astype(BF16)
    c0 += A_Q_W

    ka = proj[:, c0:c0 + A_KV_W]
    ka = ka * lax.rsqrt(_group_mean_sq(ka, bd[:A_KV_W, :A_KV_W]) + NORM_EPS) * kg_ref[...]
    ka = ka * cos_a + _rot_half(ka, AXIAL_HALF // 2, AXIAL_HALF) * sin_a
    ka_ref[...] = ka.astype(BF16)
    c0 += A_KV_W

    va_ref[0] = proj[:, c0:c0 + A_KV_W].T.astype(BF16)
    c0 += A_KV_W

    qb = proj[:, c0:c0 + B_QK_W]
    qb = qb * _tile4(cos_b) + _rot_half(qb, ROPE_DIMS // 2, HEAD_DIM) * _tile4(sin_b)
    qb_ref[...] = (qb * Q_SCALE).astype(BF16)
    c0 += B_QK_W

    kb = proj[:, c0:c0 + B_QK_W]
    kb = kb * _tile4(cos_b) + _rot_half(kb, ROPE_DIMS // 2, HEAD_DIM) * _tile4(sin_b)
    kb_ref[...] = kb.astype(BF16)
    c0 += B_QK_W

    vb_ref[0] = proj[:, c0:c0 + B_V_W].T.astype(BF16)


def _rope_tables(seq):
    rows = seq // GRID_W
    row = jnp.repeat(jnp.arange(rows, dtype=F32), GRID_W)
    col = jnp.tile(jnp.arange(GRID_W, dtype=F32), rows)
    pos = jnp.arange(seq, dtype=F32)
    inv_ax = AXIAL_THETA ** (-jnp.arange(0, AXIAL_HALF, 2, dtype=F32) / AXIAL_HALF)
    inv_p = ROPE_THETA ** (-jnp.arange(0, ROPE_DIMS, 2, dtype=F32) / ROPE_DIMS)
    row_ang = row[:, None] * inv_ax[None, :]
    col_ang = col[:, None] * inv_ax[None, :]
    pos_ang = pos[:, None] * inv_p[None, :]
    cr, sr = jnp.cos(row_ang), jnp.sin(row_ang)
    cc, sc = jnp.cos(col_ang), jnp.sin(col_ang)
    cp, sp = jnp.cos(pos_ang), jnp.sin(pos_ang)
    rest = HEAD_DIM - ROPE_DIMS
    cos_a = jnp.concatenate([cr, cr, cc, cc], axis=1)
    sin_a = jnp.concatenate([-sr, sr, -sc, sc], axis=1)
    cos_b = jnp.concatenate([cp, cp, jnp.ones((seq, rest), F32)], axis=1)
    sin_b = jnp.concatenate([-sp, sp, jnp.zeros((seq, rest), F32)], axis=1)
    two = lambda t: jnp.concatenate([t, t], axis=1)
    return jnp.concatenate([two(cos_a), two(sin_a), two(cos_b), two(sin_b)], axis=1)


def _proj_call(x, g, w_in, qg, kg, tab, bd, tm):
    seq = x.shape[0]
    row = lambda i: (i, 0)
    fix = lambda i: (0, 0)
    n = seq // tm
    rows_out = lambda w: (pl.BlockSpec((tm, w), row), jax.ShapeDtypeStruct((seq, w), BF16))
    cols_out = lambda w: (pl.BlockSpec((1, w, tm), lambda i: (i, 0, 0)), jax.ShapeDtypeStruct((n, w, tm), BF16))
    outs = [rows_out(A_Q_W), rows_out(A_KV_W), cols_out(A_KV_W), rows_out(B_QK_W), rows_out(B_QK_W), cols_out(B_V_W)]
    return pl.pallas_call(
        _proj_kernel,
        grid=(seq // tm,),
        in_specs=[
            pl.BlockSpec((tm, D_MODEL), row),
            pl.BlockSpec((1, D_MODEL), fix),
            pl.BlockSpec((D_MODEL, IN_COLS), fix),
            pl.BlockSpec((1, A_Q_W), fix),
            pl.BlockSpec((1, A_KV_W), fix),
            pl.BlockSpec((tm, 512), row),
            pl.BlockSpec((A_Q_W, A_Q_W), fix),
        ],
        out_specs=[o[0] for o in outs],
        out_shape=[o[1] for o in outs],
        compiler_params=_cparams(("parallel",)),
        name="proj",
    )(x, g, w_in, qg, kg, tab, bd)


SOFTMAX_ROWS = 32


def _flash_scratch(nq, tk):
    return [pltpu.VMEM((2, tk, nq), F32), pltpu.VMEM((2, tk, nq), BF16), pltpu.VMEM((LANES, nq), F32)]


def _flash_cols(qst, k_ref, vt_ref, s_ref, p_ref, acc_ref):
    nq = qst.shape[1]
    n_chunks, _, tk = vt_ref.shape
    assert n_chunks % 2 == 0 and tk % SOFTMAX_ROWS == 0
    n_sub = tk // SOFTMAX_ROWS

    def scores(j, slot):
        k = k_ref[pl.ds(pl.multiple_of(j * tk, tk), tk), :]
        s_ref[slot] = jnp.dot(k, qst, preferred_element_type=F32)

    def softmax(slot, m, l):
        sub = lambda b: s_ref[slot, b * SOFTMAX_ROWS:(b + 1) * SOFTMAX_ROWS, :]
        mx = sub(0)
        for b in range(1, n_sub):
            mx = jnp.maximum(mx, sub(b))
        m_new = jnp.maximum(m, jnp.max(mx, axis=0, keepdims=True))
        alpha = jnp.exp2(m - m_new)
        psum = jnp.zeros((8, nq), F32)
        for b in range(n_sub):
            p = jnp.exp2(sub(b) - m_new)
            psum = psum + jnp.sum(p.reshape(SOFTMAX_ROWS // 8, 8, nq), axis=0)
            p_ref[slot, b * SOFTMAX_ROWS:(b + 1) * SOFTMAX_ROWS, :] = p.astype(BF16)
        return alpha, m_new, alpha * l + psum

    def weighted(j, slot, alpha):
        acc_ref[...] = alpha * acc_ref[...] + jnp.dot(vt_ref[j], p_ref[slot], preferred_element_type=F32)

    def step(j, slot, a_prev, m, l):
        scores(j + 1, 1 - slot)
        alpha, m, l = softmax(slot, m, l)
        weighted(j - 1, 1 - slot, a_prev)
        return alpha, m, l

    def body(t, carry):
        a_prev, m, l = carry
        a_prev, m, l = step(2 * t + 1, 1, a_prev, m, l)
        return step(2 * t + 2, 0, a_prev, m, l)

    scores(0, 0)
    acc_ref[...] = jnp.zeros_like(acc_ref)
    carry = softmax(0, jnp.full((1, nq), -jnp.inf, F32), jnp.zeros((8, nq), F32))
    scores(1, 1)
    a_prev, m, l = lax.fori_loop(0, (n_chunks - 2) // 2, body, carry)
    alpha, m, l = softmax(1, m, l)
    weighted(n_chunks - 2, 0, a_prev)
    weighted(n_chunks - 1, 1, alpha)
    return acc_ref[...], jnp.sum(l, axis=0, keepdims=True)


def _gqa_kernel(q_ref, k_ref, vt_ref, o_ref, s_ref, p_ref, acc_ref, *, tq):
    g = pl.program_id(0)
    lane = lax.broadcasted_iota(jnp.int32, (tq, LANES), 1)
    in_g = (lane // HEAD_DIM) == g
    rows = []
    for hh in range(A_GROUP):
        qp = q_ref[:, LANES * (hh // 2):LANES * (hh // 2 + 1)].astype(F32)
        aligned = jnp.where((hh % 2) == g, qp, pltpu.roll(qp, HEAD_DIM, 1))
        rows.append(jnp.where(in_g, aligned, 0.0))
    qs = jnp.concatenate(rows, axis=0)
    acc_t, l = _flash_cols(qs.T.astype(BF16), k_ref, vt_ref, s_ref, p_ref, acc_ref)
    o = (acc_t / l).T
    left_half = lane < HEAD_DIM
    for p in range(A_GROUP // 2):
        a = o[(2 * p) * tq:(2 * p + 1) * tq]
        b = o[(2 * p + 1) * tq:(2 * p + 2) * tq]
        left = jnp.where(g == 0, a, pltpu.roll(a, HEAD_DIM, 1))
        right = jnp.where(g == 0, pltpu.roll(b, HEAD_DIM, 1), b)
        o_ref[:, LANES * p:LANES * (p + 1)] = jnp.where(left_half, left, right).astype(BF16)


def _gqa_call(qa, ka, vat, tq):
    seq = qa.shape[0]
    return pl.pallas_call(
        functools.partial(_gqa_kernel, tq=tq),
        grid=(A_KV_HEADS, seq // tq),
        in_specs=[
            pl.BlockSpec((tq, A_GROUP * HEAD_DIM), lambda g, i: (i, g)),
            pl.BlockSpec((seq, A_KV_W), lambda g, i: (0, 0)),
            pl.BlockSpec(vat.shape, lambda g, i: (0, 0, 0)),
        ],
        out_specs=pl.BlockSpec((tq, A_GROUP * HEAD_DIM), lambda g, i: (i, g)),
        out_shape=jax.ShapeDtypeStruct((seq, A_Q_W), BF16),
        scratch_shapes=_flash_scratch(A_GROUP * tq, vat.shape[2]),
        compiler_params=_cparams(("parallel", "parallel")),
        name="gqa",
    )(qa, ka, vat)


def _diff_kernel(lam_ref, q_ref, k_ref, vt_ref, sg_ref, o_ref, s_ref, p_ref, acc_ref, *, tq, lambda_init):
    lane = lax.broadcasted_iota(jnp.int32, (tq, LANES), 1)
    q = q_ref[...].astype(F32)
    qs = jnp.concatenate([jnp.where(lane < HEAD_DIM, q, 0.0),
                          jnp.where(lane >= HEAD_DIM, q, 0.0)], axis=0)
    acc_t, l = _flash_cols(qs.T.astype(BF16), k_ref, vt_ref, s_ref, p_ref, acc_ref)
    o = (acc_t / l).T
    lv = lam_ref[...]
    lam = (jnp.exp(jnp.sum(lv[0:1] * lv[1:2], axis=-1, keepdims=True))
           - jnp.exp(jnp.sum(lv[2:3] * lv[3:4], axis=-1, keepdims=True)) + lambda_init)
    ob = o[:tq] - lam * o[tq:]
    ob = _rms(ob, sg_ref[...], SUBLN_EPS) * (1.0 - lambda_init)
    o_ref[...] = ob.astype(BF16)


def _diff_call(lam_vecs, qb, kb, vbt, subln_g, tq, lambda_init):
    seq = qb.shape[0]
    n_chunks, _, tk = vbt.shape
    return pl.pallas_call(
        functools.partial(_diff_kernel, tq=tq, lambda_init=lambda_init),
        grid=(B_HEADS, seq // tq),
        in_specs=[
            pl.BlockSpec((4, HEAD_DIM), lambda h, i: (0, 0)),
            pl.BlockSpec((tq, LANES), lambda h, i: (i, h)),
            pl.BlockSpec((seq, LANES), lambda h, i: (0, h)),
            pl.BlockSpec((n_chunks, B_V_DIM, tk), lambda h, i: (0, h, 0)),
            pl.BlockSpec((1, B_V_DIM), lambda h, i: (0, 0)),
        ],
        out_specs=pl.BlockSpec((tq, LANES), lambda h, i: (i, h)),
        out_shape=jax.ShapeDtypeStruct((seq, B_V_W), BF16),
        scratch_shapes=_flash_scratch(2 * tq, tk),
        compiler_params=_cparams(("parallel", "parallel")),
        name="diff",
    )(lam_vecs, qb, kb, vbt, subln_g)


def _mid_kernel(oa_ref, ob_ref, x_ref, wo_ref, g_ref, wq_ref, sk_ref, h_ref, xn_ref, sc_ref):
    o = jnp.concatenate([oa_ref[...], ob_ref[...]], axis=1)
    h = x_ref[...] + jnp.dot(o, wo_ref[...], preferred_element_type=F32)
    h_ref[...] = h
    xn = _rms(h, g_ref[...], NORM_EPS).astype(BF16)
    xn_ref[...] = xn
    q = jnp.dot(xn, wq_ref[...], preferred_element_type=F32).astype(BF16)
    for hp in range(2 * PEER_HEADS):
        sc_ref[hp] = lax.dot_general(sk_ref[hp], q[:, PEER_HALF * hp:PEER_HALF * (hp + 1)],
                                     NT_DIMS, preferred_element_type=F32)


def _mid_call(oa, ob, x, w_out, g, w_query, sub_keys, tm):
    seq = x.shape[0]
    row = lambda i: (i, 0)
    fix = lambda i: (0, 0)
    nq = 2 * PEER_HEADS * PEER_HALF
    return pl.pallas_call(
        _mid_kernel,
        grid=(seq // tm,),
        in_specs=[
            pl.BlockSpec((tm, A_Q_W), row),
            pl.BlockSpec((tm, B_V_W), row),
            pl.BlockSpec((tm, D_MODEL), row),
            pl.BlockSpec((D_MODEL, D_MODEL), fix),
            pl.BlockSpec((1, D_MODEL), fix),
            pl.BlockSpec((D_MODEL, nq), fix),
            pl.BlockSpec((2 * PEER_HEADS, PEER_KEYS, PEER_HALF), lambda i: (0, 0, 0)),
        ],
        out_specs=[
            pl.BlockSpec((tm, D_MODEL), row),
            pl.BlockSpec((tm, D_MODEL), row),
            pl.BlockSpec((2 * PEER_HEADS, PEER_KEYS, tm), lambda i: (0, 0, i)),
        ],
        out_shape=[
            jax.ShapeDtypeStruct((seq, D_MODEL), F32),
            jax.ShapeDtypeStruct((seq, D_MODEL), BF16),
            jax.ShapeDtypeStruct((2 * PEER_HEADS, PEER_KEYS, seq), F32),
        ],
        compiler_params=_cparams(("parallel",)),
        name="mid",
    )(oa, ob, x, w_out, g, w_query, sub_keys)


N_TOP = PEER_TOPK + 1
_CAND = [(a, b) for a in range(N_TOP) for b in range(N_TOP) if (a + 1) * (b + 1) <= N_TOP]


def _top_vals(s, n):
    vals = []
    for r in range(n):
        m = jnp.max(s, axis=0, keepdims=True)
        vals.append(m)
        if r + 1 < n:
            s = jnp.where(s >= m, -jnp.inf, s)
    return vals


def _topk_kernel(sc_ref, e1_ref, e2_ref, tau_ref):
    t = sc_ref.shape[-1]
    taus = []
    for h in range(PEER_HEADS):
        s1 = sc_ref[2 * h]
        s2 = sc_ref[2 * h + 1]
        v1 = _top_vals(s1, N_TOP)
        v2 = _top_vals(s2, N_TOP)
        cands = [v1[a] + v2[b] for (a, b) in _CAND]
        pad = (-len(cands)) % 8
        cands += [jnp.full((1, t), -jnp.inf, F32)] * pad
        c = jnp.concatenate(cands, axis=0)
        top = _top_vals(c, N_TOP)
        t16, t17 = top[PEER_TOPK - 1], top[PEER_TOPK]
        best = v1[0] + v2[0]
        z = jnp.sum(jnp.where(c >= t16, jnp.exp(c - best), 0.0), axis=0, keepdims=True)
        rz = 1.0 / z
        e1_ref[h] = jnp.exp(s1 - v1[0]) * rz
        e2_ref[h] = jnp.exp(s2 - v2[0])
        taus.append(0.5 * (jnp.exp(t16 - best) + jnp.exp(t17 - best)) * rz)
    tau_ref[...] = jnp.concatenate(taus, axis=0)


def _topk_call(sc, tt):
    seq = sc.shape[-1]
    blk = lambda i: (0, 0, i)
    return pl.pallas_call(
        _topk_kernel,
        grid=(seq // tt,),
        in_specs=[pl.BlockSpec((2 * PEER_HEADS, PEER_KEYS, tt), blk)],
        out_specs=[
            pl.BlockSpec((PEER_HEADS, PEER_KEYS, tt), blk),
            pl.BlockSpec((PEER_HEADS, PEER_KEYS, tt), blk),
            pl.BlockSpec((PEER_HEADS, tt), lambda i: (0, i)),
        ],
        out_shape=[
            jax.ShapeDtypeStruct((PEER_HEADS, PEER_KEYS, seq), F32),
            jax.ShapeDtypeStruct((PEER_HEADS, PEER_KEYS, seq), F32),
            jax.ShapeDtypeStruct((PEER_HEADS, seq), F32),
        ],
        compiler_params=_cparams(("parallel",)),
        name="topk",
    )(sc)


def _peer_kernel(xn_ref, u_ref, vt_ref, e1_ref, e2_ref, tau_ref, h_ref, g_ref, o_ref,
                 acc_ref, w_ref, *, eb):
    e = pl.program_id(1)
    xn = xn_ref[...]
    inv_sqrt2 = 1.0 / math.sqrt(2.0)
    for ii in range(eb // PEER_KEYS):
        i = e * (eb // PEER_KEYS) + ii
        a = lax.dot_general(u_ref[PEER_KEYS * ii:PEER_KEYS * (ii + 1), :], xn, NT_DIMS,
                            preferred_element_type=F32)
        act = 0.5 * a * (1.0 + lax.erf(a * inv_sqrt2))
        gate = jnp.zeros_like(a)
        for h in range(PEER_HEADS):
            p = e2_ref[h] * e1_ref[h, pl.ds(i, 1), :]
            gate = gate + jnp.where(p >= tau_ref[h:h + 1, :], p, 0.0)
        w_ref[PEER_KEYS * ii:PEER_KEYS * (ii + 1), :] = (gate * act).astype(BF16)
    part = jnp.dot(vt_ref[...], w_ref[...], preferred_element_type=F32)

    @pl.when(e == 0)
    def _():
        acc_ref[...] = part

    @pl.when(e > 0)
    def _():
        acc_ref[...] += part

    @pl.when(e == pl.num_programs(1) - 1)
    def _():
        out = h_ref[...] + acc_ref[...].T
        o_ref[...] = _rms(out, g_ref[...], NORM_EPS)


def _peer_call(xn, u, vt, e1, e2, tau, h, g, tt, eb):
    seq = xn.shape[0]
    n_exp = u.shape[0]
    tok = lambda t, e: (t, 0)
    tok3 = lambda t, e: (0, 0, t)
    return pl.pallas_call(
        functools.partial(_peer_kernel, eb=eb),
        grid=(seq // tt, n_exp // eb),
        in_specs=[
            pl.BlockSpec((tt, D_MODEL), tok),
            pl.BlockSpec((eb, D_MODEL), lambda t, e: (e, 0)),
            pl.BlockSpec((D_MODEL, eb), lambda t, e: (0, e)),
            pl.BlockSpec((PEER_HEADS, PEER_KEYS, tt), tok3),
            pl.BlockSpec((PEER_HEADS, PEER_KEYS, tt), tok3),
            pl.BlockSpec((PEER_HEADS, tt), lambda t, e: (0, t)),
            pl.BlockSpec((tt, D_MODEL), tok),
            pl.BlockSpec((1, D_MODEL), lambda t, e: (0, 0)),
        ],
        out_specs=pl.BlockSpec((tt, D_MODEL), tok),
        out_shape=jax.ShapeDtypeStruct((seq, D_MODEL), F32),
        scratch_shapes=[pltpu.VMEM((D_MODEL, tt), F32), pltpu.VMEM((eb, tt), BF16)],
        compiler_params=_cparams(("parallel", "arbitrary")),
        name="peer",
    )(xn, u, vt, e1, e2, tau, h, g)


def _tiles(seq):
    big = seq >= 4096
    return dict(
        tm=512 if big else 256,
        tq_a=128, tq_b=256,
        tt_topk=256, tt_peer=512 if big else 256, eb=512,
    )


def kernel(x, norm_attn_g, w_in, q_norm_g, k_norm_g, lambda_q1, lambda_k1, lambda_q2, lambda_k2,
           subln_g, w_out, norm_ffn_g, w_query, sub_keys, expert_u, expert_v, norm_final_g):
    batch, seq, d = x.shape
    assert batch == 1 and d == D_MODEL and norm_attn_g.shape[0] == 1
    t = _tiles(seq)
    lambda_init = 0.8 - 0.6 * math.exp(-0.3 * 0)
    x2 = x.reshape(seq, d)

    tab = _rope_tables(seq)
    blk = jnp.arange(A_Q_W) // HEAD_DIM
    bd = jnp.where(blk[:, None] == blk[None, :], 1.0 / HEAD_DIM, 0.0).astype(BF16)
    qg = (jnp.tile(q_norm_g[0], A_Q_HEADS) * Q_SCALE).reshape(1, A_Q_W)
    kg = jnp.tile(k_norm_g[0], A_KV_HEADS).reshape(1, A_KV_W)

    qa, ka, vat, qb, kb, vbt = _proj_call(x2, norm_attn_g, w_in[0].astype(BF16), qg, kg, tab, bd, t["tm"])
    oa = _gqa_call(qa, ka, vat, t["tq_a"])
    lam_vecs = jnp.concatenate([lambda_q1, lambda_k1, lambda_q2, lambda_k2], axis=0)
    ob = _diff_call(lam_vecs, qb, kb, vbt, subln_g, t["tq_b"], lambda_init)

    sk = sub_keys[0].reshape(2 * PEER_HEADS, PEER_KEYS, PEER_HALF).astype(BF16)
    h, xn, sc = _mid_call(oa, ob, x2, w_out[0].astype(BF16), norm_ffn_g, w_query[0].astype(BF16), sk, t["tm"])
    e1, e2, tau = _topk_call(sc, t["tt_topk"])
    out = _peer_call(xn, expert_u[0].astype(BF16), expert_v[0].T.astype(BF16), e1, e2, tau, h,
                     norm_final_g.reshape(1, d), t["tt_peer"], t["eb"])
    return out.reshape(batch, seq, d)
```

```python
import functools
import math

import jax
import jax.numpy as jnp
from jax import lax
from jax.experimental import pallas as pl
from jax.experimental.pallas import tpu as pltpu

F32 = jnp.float32
BF16 = jnp.bfloat16

D_MODEL = 1024
HEAD_DIM = 64
A_Q_HEADS = 8
A_KV_HEADS = 2
A_GROUP = A_Q_HEADS // A_KV_HEADS
B_HEADS = 4
B_V_DIM = 2 * HEAD_DIM
GRID_W = 64
AXIAL_THETA = 10000.0
AXIAL_HALF = HEAD_DIM // 2
ROPE_THETA = 500000.0
ROPE_DIMS = HEAD_DIM // 4
NORM_EPS = 1e-6
SUBLN_EPS = 1e-5
A_Q_W = A_Q_HEADS * HEAD_DIM
A_KV_W = A_KV_HEADS * HEAD_DIM
B_QK_W = 2 * B_HEADS * HEAD_DIM
B_V_W = B_HEADS * B_V_DIM
IN_COLS = A_Q_W + 2 * A_KV_W + 2 * B_QK_W + B_V_W
PEER_HEADS = 8
PEER_KEYS = 128
PEER_HALF = 128
PEER_TOPK = 16
LANES = 128

NT_DIMS = (((1,), (1,)), ((), ()))
Q_SCALE = (HEAD_DIM ** -0.5) * math.log2(math.e)

VMEM_LIMIT = 56 * 1024 * 1024


def _cparams(sem):
    return pltpu.CompilerParams(dimension_semantics=sem, vmem_limit_bytes=VMEM_LIMIT)


def _rms(x, g, eps):
    return x * lax.rsqrt(jnp.mean(x * x, axis=-1, keepdims=True) + eps) * g


def _group_mean_sq(v, bd):
    v2 = v * v
    hi = v2.astype(BF16)
    lo = (v2 - hi.astype(F32)).astype(BF16)
    return (jnp.dot(hi, bd, preferred_element_type=F32)
            + jnp.dot(lo, bd, preferred_element_type=F32))


def _rot_half(v, half, group):
    width = v.shape[-1]
    lane = lax.broadcasted_iota(jnp.int32, v.shape, 1)
    fwd = pltpu.roll(v, width - half, 1)
    bwd = pltpu.roll(v, half, 1)
    return jnp.where((lane % group) < half, fwd, bwd)


def _tile4(t):
    return jnp.concatenate([t, t, t, t], axis=1)


def _proj_kernel(x_ref, g_ref, w_ref, qg_ref, kg_ref, tab_ref, bd_ref,
                 qa_ref, ka_ref, va_ref, qb_ref, kb_ref, vb_ref):
    xn = _rms(x_ref[...], g_ref[...], NORM_EPS)
    proj = jnp.dot(xn.astype(BF16), w_ref[...], preferred_element_type=F32)
    tab = tab_ref[...]
    cos_a, sin_a = tab[:, 0:128], tab[:, 128:256]
    cos_b, sin_b = tab[:, 256:384], tab[:, 384:512]
    bd = bd_ref[...]

    c0 = 0
    qa = proj[:, c0:c0 + A_Q_W]
    qa = qa * lax.rsqrt(_group_mean_sq(qa, bd) + NORM_EPS) * qg_ref[...]
    qa = qa * _tile4(cos_a) + _rot_half(qa, AXIAL_HALF // 2, AXIAL_HALF) * _tile4(sin_a)
    qa_ref[...] = qa.astype(BF16)
    c0 += A_Q_W

    ka = proj[:, c0:c0 + A_KV_W]
    ka = ka * lax.rsqrt(_group_mean_sq(ka, bd[:A_KV_W, :A_KV_W]) + NORM_EPS) * kg_ref[...]
    ka = ka * cos_a + _rot_half(ka, AXIAL_HALF // 2, AXIAL_HALF) * sin_a
    ka_ref[...] = ka.astype(BF16)
    c0 += A_KV_W

    va_ref[0] = proj[:, c0:c0 + A_KV_W].T.astype(BF16)
    c0 += A_KV_W

    qb = proj[:, c0:c0 + B_QK_W]
    qb = qb * _tile4(cos_b) + _rot_half(qb, ROPE_DIMS // 2, HEAD_DIM) * _tile4(sin_b)
    qb_ref[...] = (qb * Q_SCALE).astype(BF16)
    c0 += B_QK_W

    kb = proj[:, c0:c0 + B_QK_W]
    kb = kb * _tile4(cos_b) + _rot_half(kb, ROPE_DIMS // 2, HEAD_DIM) * _tile4(sin_b)
    kb_ref[...] = kb.astype(BF16)
    c0 += B_QK_W

    vb_ref[0] = proj[:, c0:c0 + B_V_W].T.astype(BF16)


def _rope_tables(seq):
    rows = seq // GRID_W
    row = jnp.repeat(jnp.arange(rows, dtype=F32), GRID_W)
    col = jnp.tile(jnp.arange(GRID_W, dtype=F32), rows)
    pos = jnp.arange(seq, dtype=F32)
    inv_ax = AXIAL_THETA ** (-jnp.arange(0, AXIAL_HALF, 2, dtype=F32) / AXIAL_HALF)
    inv_p = ROPE_THETA ** (-jnp.arange(0, ROPE_DIMS, 2, dtype=F32) / ROPE_DIMS)
    row_ang = row[:, None] * inv_ax[None, :]
    col_ang = col[:, None] * inv_ax[None, :]
    pos_ang = pos[:, None] * inv_p[None, :]
    cr, sr = jnp.cos(row_ang), jnp.sin(row_ang)
    cc, sc = jnp.cos(col_ang), jnp.sin(col_ang)
    cp, sp = jnp.cos(pos_ang), jnp.sin(pos_ang)
    rest = HEAD_DIM - ROPE_DIMS
    cos_a = jnp.concatenate([cr, cr, cc, cc], axis=1)
    sin_a = jnp.concatenate([-sr, sr, -sc, sc], axis=1)
    cos_b = jnp.concatenate([cp, cp, jnp.ones((seq, rest), F32)], axis=1)
    sin_b = jnp.concatenate([-sp, sp, jnp.zeros((seq, rest), F32)], axis=1)
    two = lambda t: jnp.concatenate([t, t], axis=1)
    return jnp.concatenate([two(cos_a), two(sin_a), two(cos_b), two(sin_b)], axis=1)


def _proj_call(x, g, w_in, qg, kg, tab, bd, tm):
    seq = x.shape[0]
    row = lambda i: (i, 0)
    fix = lambda i: (0, 0)
    n = seq // tm
    rows_out = lambda w: (pl.BlockSpec((tm, w), row), jax.ShapeDtypeStruct((seq, w), BF16))
    cols_out = lambda w: (pl.BlockSpec((1, w, tm), lambda i: (i, 0, 0)), jax.ShapeDtypeStruct((n, w, tm), BF16))
    outs = [rows_out(A_Q_W), rows_out(A_KV_W), cols_out(A_KV_W), rows_out(B_QK_W), rows_out(B_QK_W), cols_out(B_V_W)]
    return pl.pallas_call(
        _proj_kernel,
        grid=(seq // tm,),
        in_specs=[
            pl.BlockSpec((tm, D_MODEL), row),
            pl.BlockSpec((1, D_MODEL), fix),
            pl.BlockSpec((D_MODEL, IN_COLS), fix),
            pl.BlockSpec((1, A_Q_W), fix),
            pl.BlockSpec((1, A_KV_W), fix),
            pl.BlockSpec((tm, 512), row),
            pl.BlockSpec((A_Q_W, A_Q_W), fix),
        ],
        out_specs=[o[0] for o in outs],
        out_shape=[o[1] for o in outs],
        compiler_params=_cparams(("parallel",)),
        name="proj",
    )(x, g, w_in, qg, kg, tab, bd)


SOFTMAX_ROWS = 32


def _flash_scratch(nq, tk):
    return [pltpu.VMEM((2, tk, nq), F32), pltpu.VMEM((2, tk, nq), BF16), pltpu.VMEM((LANES, nq), F32)]


def _flash_cols(qst, k_ref, vt_ref, s_ref, p_ref, acc_ref):
    nq = qst.shape[1]
    tk = s_ref.shape[1]
    per = tk // vt_ref.shape[2]
    n_chunks = vt_ref.shape[0] // per
    assert n_chunks % 2 == 0 and tk % SOFTMAX_ROWS == 0
    n_sub = tk // SOFTMAX_ROWS

    def scores(j, slot):
        k = k_ref[pl.ds(pl.multiple_of(j * tk, tk), tk), :]
        s_ref[slot] = jnp.dot(k, qst, preferred_element_type=F32)

    def softmax(slot, m, l):
        sub = lambda b: s_ref[slot, b * SOFTMAX_ROWS:(b + 1) * SOFTMAX_ROWS, :]
        mx = sub(0)
        for b in range(1, n_sub):
            mx = jnp.maximum(mx, sub(b))
        m_new = jnp.maximum(m, jnp.max(mx, axis=0, keepdims=True))
        alpha = jnp.exp2(m - m_new)
        psum = jnp.zeros((8, nq), F32)
        for b in range(n_sub):
            p = jnp.exp2(sub(b) - m_new)
            psum = psum + jnp.sum(p.reshape(SOFTMAX_ROWS // 8, 8, nq), axis=0)
            p_ref[slot, b * SOFTMAX_ROWS:(b + 1) * SOFTMAX_ROWS, :] = p.astype(BF16)
        return alpha, m_new, alpha * l + psum

    def weighted(j, slot, alpha):
        vt = jnp.concatenate([vt_ref[per * j + c] for c in range(per)], axis=1)
        acc_ref[...] = alpha * acc_ref[...] + jnp.dot(vt, p_ref[slot], preferred_element_type=F32)

    def step(j, slot, a_prev, m, l):
        scores(j + 1, 1 - slot)
        alpha, m, l = softmax(slot, m, l)
        weighted(j - 1, 1 - slot, a_prev)
        return alpha, m, l

    def body(t, carry):
        a_prev, m, l = carry
        a_prev, m, l = step(2 * t + 1, 1, a_prev, m, l)
        return step(2 * t + 2, 0, a_prev, m, l)

    scores(0, 0)
    acc_ref[...] = jnp.zeros_like(acc_ref)
    carry = softmax(0, jnp.full((1, nq), -jnp.inf, F32), jnp.zeros((8, nq), F32))
    scores(1, 1)
    a_prev, m, l = lax.fori_loop(0, (n_chunks - 2) // 2, body, carry)
    alpha, m, l = softmax(1, m, l)
    weighted(n_chunks - 2, 0, a_prev)
    weighted(n_chunks - 1, 1, alpha)
    return acc_ref[...], jnp.sum(l, axis=0, keepdims=True)


def _gqa_kernel(q_ref, k_ref, vt_ref, o_ref, s_ref, p_ref, acc_ref, *, tq):
    g = pl.program_id(0)
    lane = lax.broadcasted_iota(jnp.int32, (tq, LANES), 1)
    in_g = (lane // HEAD_DIM) == g
    rows = []
    for hh in range(A_GROUP):
        qp = q_ref[:, LANES * (hh // 2):LANES * (hh // 2 + 1)].astype(F32)
        aligned = jnp.where((hh % 2) == g, qp, pltpu.roll(qp, HEAD_DIM, 1))
        rows.append(jnp.where(in_g, aligned, 0.0))
    qs = jnp.concatenate(rows, axis=0)
    acc_t, l = _flash_cols(qs.T.astype(BF16), k_ref, vt_ref, s_ref, p_ref, acc_ref)
    o = (acc_t / l).T
    left_half = lane < HEAD_DIM
    for p in range(A_GROUP // 2):
        a = o[(2 * p) * tq:(2 * p + 1) * tq]
        b = o[(2 * p + 1) * tq:(2 * p + 2) * tq]
        left = jnp.where(g == 0, a, pltpu.roll(a, HEAD_DIM, 1))
        right = jnp.where(g == 0, pltpu.roll(b, HEAD_DIM, 1), b)
        o_ref[:, LANES * p:LANES * (p + 1)] = jnp.where(left_half, left, right).astype(BF16)


def _gqa_call(qa, ka, vat, tq, tk):
    seq = qa.shape[0]
    return pl.pallas_call(
        functools.partial(_gqa_kernel, tq=tq),
        grid=(A_KV_HEADS, seq // tq),
        in_specs=[
            pl.BlockSpec((tq, A_GROUP * HEAD_DIM), lambda g, i: (i, g)),
            pl.BlockSpec((seq, A_KV_W), lambda g, i: (0, 0)),
            pl.BlockSpec(vat.shape, lambda g, i: (0, 0, 0)),
        ],
        out_specs=pl.BlockSpec((tq, A_GROUP * HEAD_DIM), lambda g, i: (i, g)),
        out_shape=jax.ShapeDtypeStruct((seq, A_Q_W), BF16),
        scratch_shapes=_flash_scratch(A_GROUP * tq, tk),
        compiler_params=_cparams(("parallel", "parallel")),
        name="gqa",
    )(qa, ka, vat)


def _diff_kernel(lam_ref, q_ref, k_ref, vt_ref, sg_ref, o_ref, s_ref, p_ref, acc_ref, *, tq, lambda_init):
    lane = lax.broadcasted_iota(jnp.int32, (tq, LANES), 1)
    q = q_ref[...].astype(F32)
    qs = jnp.concatenate([jnp.where(lane < HEAD_DIM, q, 0.0),
                          jnp.where(lane >= HEAD_DIM, q, 0.0)], axis=0)
    acc_t, l = _flash_cols(qs.T.astype(BF16), k_ref, vt_ref, s_ref, p_ref, acc_ref)
    o = (acc_t / l).T
    lv = lam_ref[...]
    lam = (jnp.exp(jnp.sum(lv[0:1] * lv[1:2], axis=-1, keepdims=True))
           - jnp.exp(jnp.sum(lv[2:3] * lv[3:4], axis=-1, keepdims=True)) + lambda_init)
    ob = o[:tq] - lam * o[tq:]
    ob = _rms(ob, sg_ref[...], SUBLN_EPS) * (1.0 - lambda_init)
    o_ref[...] = ob.astype(BF16)


def _diff_call(lam_vecs, qb, kb, vbt, subln_g, tq, tk, lambda_init):
    seq = qb.shape[0]
    n_slabs, _, slab = vbt.shape
    return pl.pallas_call(
        functools.partial(_diff_kernel, tq=tq, lambda_init=lambda_init),
        grid=(B_HEADS, seq // tq),
        in_specs=[
            pl.BlockSpec((4, HEAD_DIM), lambda h, i: (0, 0)),
            pl.BlockSpec((tq, LANES), lambda h, i: (i, h)),
            pl.BlockSpec((seq, LANES), lambda h, i: (0, h)),
            pl.BlockSpec((n_slabs, B_V_DIM, slab), lambda h, i: (0, h, 0)),
            pl.BlockSpec((1, B_V_DIM), lambda h, i: (0, 0)),
        ],
        out_specs=pl.BlockSpec((tq, LANES), lambda h, i: (i, h)),
        out_shape=jax.ShapeDtypeStruct((seq, B_V_W), BF16),
        scratch_shapes=_flash_scratch(2 * tq, tk),
        compiler_params=_cparams(("parallel", "parallel")),
        name="diff",
    )(lam_vecs, qb, kb, vbt, subln_g)


def _mid_kernel(oa_ref, ob_ref, x_ref, wo_ref, g_ref, wq_ref, sk_ref, h_ref, xn_ref, sc_ref):
    o = jnp.concatenate([oa_ref[...], ob_ref[...]], axis=1)
    h = x_ref[...] + jnp.dot(o, wo_ref[...], preferred_element_type=F32)
    h_ref[...] = h
    xn = _rms(h, g_ref[...], NORM_EPS).astype(BF16)
    xn_ref[...] = xn
    q = jnp.dot(xn, wq_ref[...], preferred_element_type=F32).astype(BF16)
    for hp in range(2 * PEER_HEADS):
        sc_ref[hp] = lax.dot_general(sk_ref[hp], q[:, PEER_HALF * hp:PEER_HALF * (hp + 1)],
                                     NT_DIMS, preferred_element_type=F32)


def _mid_call(oa, ob, x, w_out, g, w_query, sub_keys, tm):
    seq = x.shape[0]
    row = lambda i: (i, 0)
    fix = lambda i: (0, 0)
    nq = 2 * PEER_HEADS * PEER_HALF
    return pl.pallas_call(
        _mid_kernel,
        grid=(seq // tm,),
        in_specs=[
            pl.BlockSpec((tm, A_Q_W), row),
            pl.BlockSpec((tm, B_V_W), row),
            pl.BlockSpec((tm, D_MODEL), row),
            pl.BlockSpec((D_MODEL, D_MODEL), fix),
            pl.BlockSpec((1, D_MODEL), fix),
            pl.BlockSpec((D_MODEL, nq), fix),
            pl.BlockSpec((2 * PEER_HEADS, PEER_KEYS, PEER_HALF), lambda i: (0, 0, 0)),
        ],
        out_specs=[
            pl.BlockSpec((tm, D_MODEL), row),
            pl.BlockSpec((tm, D_MODEL), row),
            pl.BlockSpec((2 * PEER_HEADS, PEER_KEYS, tm), lambda i: (0, 0, i)),
        ],
        out_shape=[
            jax.ShapeDtypeStruct((seq, D_MODEL), F32),
            jax.ShapeDtypeStruct((seq, D_MODEL), BF16),
            jax.ShapeDtypeStruct((2 * PEER_HEADS, PEER_KEYS, seq), F32),
        ],
        compiler_params=_cparams(("parallel",)),
        name="mid",
    )(oa, ob, x, w_out, g, w_query, sub_keys)


N_TOP = PEER_TOPK + 1
_CAND = [(a, b) for a in range(N_TOP) for b in range(N_TOP) if (a + 1) * (b + 1) <= N_TOP]


def _top_vals(s, n):
    vals = []
    for r in range(n):
        m = jnp.max(s, axis=0, keepdims=True)
        vals.append(m)
        if r + 1 < n:
            s = jnp.where(s >= m, -jnp.inf, s)
    return vals


def _topk_kernel(sc_ref, e1_ref, e2_ref, tau_ref):
    t = sc_ref.shape[-1]
    taus = []
    for h in range(PEER_HEADS):
        s1 = sc_ref[2 * h]
        s2 = sc_ref[2 * h + 1]
        v1 = _top_vals(s1, N_TOP)
        v2 = _top_vals(s2, N_TOP)
        cands = [v1[a] + v2[b] for (a, b) in _CAND]
        pad = (-len(cands)) % 8
        cands += [jnp.full((1, t), -jnp.inf, F32)] * pad
        c = jnp.concatenate(cands, axis=0)
        top = _top_vals(c, N_TOP)
        t16, t17 = top[PEER_TOPK - 1], top[PEER_TOPK]
        best = v1[0] + v2[0]
        z = jnp.sum(jnp.where(c >= t16, jnp.exp(c - best), 0.0), axis=0, keepdims=True)
        rz = 1.0 / z
        e1_ref[h] = jnp.exp(s1 - v1[0]) * rz
        e2_ref[h] = jnp.exp(s2 - v2[0])
        taus.append(0.5 * (jnp.exp(t16 - best) + jnp.exp(t17 - best)) * rz)
    tau_ref[...] = jnp.concatenate(taus, axis=0)


def _topk_call(sc, tt):
    seq = sc.shape[-1]
    blk = lambda i: (0, 0, i)
    return pl.pallas_call(
        _topk_kernel,
        grid=(seq // tt,),
        in_specs=[pl.BlockSpec((2 * PEER_HEADS, PEER_KEYS, tt), blk)],
        out_specs=[
            pl.BlockSpec((PEER_HEADS, PEER_KEYS, tt), blk),
            pl.BlockSpec((PEER_HEADS, PEER_KEYS, tt), blk),
            pl.BlockSpec((PEER_HEADS, tt), lambda i: (0, i)),
        ],
        out_shape=[
            jax.ShapeDtypeStruct((PEER_HEADS, PEER_KEYS, seq), F32),
            jax.ShapeDtypeStruct((PEER_HEADS, PEER_KEYS, seq), F32),
            jax.ShapeDtypeStruct((PEER_HEADS, seq), F32),
        ],
        compiler_params=_cparams(("parallel",)),
        name="topk",
    )(sc)


def _peer_kernel(xn_ref, u_ref, vt_ref, e1_ref, e2_ref, tau_ref, h_ref, g_ref, o_ref,
                 acc_ref, a_ref, w_ref, *, eb):
    e = pl.program_id(1)
    tt = xn_ref.shape[0]
    inv_sqrt2 = 1.0 / math.sqrt(2.0)
    a_ref[...] = lax.dot_general(u_ref[...], xn_ref[...], NT_DIMS,
                                 preferred_element_type=F32)
    for ii in range(eb // PEER_KEYS):
        i = e * (eb // PEER_KEYS) + ii
        rows = slice(PEER_KEYS * ii, PEER_KEYS * (ii + 1))
        e1_rows = [e1_ref[h, pl.ds(i, 1), :] for h in range(PEER_HEADS)]
        for tc in range(tt // LANES):
            cols = slice(LANES * tc, LANES * (tc + 1))
            a = a_ref[rows, cols]
            act = 0.5 * a * (1.0 + lax.erf(a * inv_sqrt2))
            gate = None
            for h in range(PEER_HEADS):
                p = e2_ref[h, :, cols] * e1_rows[h][:, cols]
                term = jnp.where(p >= tau_ref[h:h + 1, cols], p, 0.0)
                gate = term if gate is None else gate + term
            w_ref[rows, cols] = (gate * act).astype(BF16)
    part = jnp.dot(vt_ref[...], w_ref[...], preferred_element_type=F32)

    @pl.when(e == 0)
    def _():
        acc_ref[...] = part

    @pl.when(e > 0)
    def _():
        acc_ref[...] += part

    @pl.when(e == pl.num_programs(1) - 1)
    def _():
        out = h_ref[...] + acc_ref[...].T
        o_ref[...] = _rms(out, g_ref[...], NORM_EPS)


def _peer_call(xn, u, vt, e1, e2, tau, h, g, tt, eb):
    seq = xn.shape[0]
    n_exp = u.shape[0]
    tok = lambda t, e: (t, 0)
    tok3 = lambda t, e: (0, 0, t)
    return pl.pallas_call(
        functools.partial(_peer_kernel, eb=eb),
        grid=(seq // tt, n_exp // eb),
        in_specs=[
            pl.BlockSpec((tt, D_MODEL), tok),
            pl.BlockSpec((eb, D_MODEL), lambda t, e: (e, 0)),
            pl.BlockSpec((D_MODEL, eb), lambda t, e: (0, e)),
            pl.BlockSpec((PEER_HEADS, PEER_KEYS, tt), tok3),
            pl.BlockSpec((PEER_HEADS, PEER_KEYS, tt), tok3),
            pl.BlockSpec((PEER_HEADS, tt), lambda t, e: (0, t)),
            pl.BlockSpec((tt, D_MODEL), tok),
            pl.BlockSpec((1, D_MODEL), lambda t, e: (0, 0)),
        ],
        out_specs=pl.BlockSpec((tt, D_MODEL), tok),
        out_shape=jax.ShapeDtypeStruct((seq, D_MODEL), F32),
        scratch_shapes=[pltpu.VMEM((D_MODEL, tt), F32), pltpu.VMEM((eb, tt), F32), pltpu.VMEM((eb, tt), BF16)],
        compiler_params=_cparams(("parallel", "arbitrary")),
        name="peer",
    )(xn, u, vt, e1, e2, tau, h, g)


def _tiles(seq):
    big = seq >= 4096
    return dict(
        tm=512 if big else 256,
        tq_a=128, tq_b=256,
        tk=1024 if big else 512,
        tt_topk=256, tt_peer=512 if big else 256, eb=1024,
    )


def kernel(x, norm_attn_g, w_in, q_norm_g, k_norm_g, lambda_q1, lambda_k1, lambda_q2, lambda_k2,
           subln_g, w_out, norm_ffn_g, w_query, sub_keys, expert_u, expert_v, norm_final_g):
    batch, seq, d = x.shape
    assert batch == 1 and d == D_MODEL and norm_attn_g.shape[0] == 1
    t = _tiles(seq)
    lambda_init = 0.8 - 0.6 * math.exp(-0.3 * 0)
    x2 = x.reshape(seq, d)

    tab = _rope_tables(seq)
    blk = jnp.arange(A_Q_W) // HEAD_DIM
    bd = jnp.where(blk[:, None] == blk[None, :], 1.0 / HEAD_DIM, 0.0).astype(BF16)
    qg = (jnp.tile(q_norm_g[0], A_Q_HEADS) * Q_SCALE).reshape(1, A_Q_W)
    kg = jnp.tile(k_norm_g[0], A_KV_HEADS).reshape(1, A_KV_W)

    qa, ka, vat, qb, kb, vbt = _proj_call(x2, norm_attn_g, w_in[0].astype(BF16), qg, kg, tab, bd, t["tm"])
    oa = _gqa_call(qa, ka, vat, t["tq_a"], t["tk"])
    lam_vecs = jnp.concatenate([lambda_q1, lambda_k1, lambda_q2, lambda_k2], axis=0)
    ob = _diff_call(lam_vecs, qb, kb, vbt, subln_g, t["tq_b"], t["tk"], lambda_init)

    sk = sub_keys[0].reshape(2 * PEER_HEADS, PEER_KEYS, PEER_HALF).astype(BF16)
    h, xn, sc = _mid_call(oa, ob, x2, w_out[0].astype(BF16), norm_ffn_g, w_query[0].astype(BF16), sk, t["tm"])
    e1, e2, tau = _topk_call(sc, t["tt_topk"])
    out = _peer_call(xn, expert_u[0].astype(BF16), expert_v[0].T.astype(BF16), e1, e2, tau, h,
                     norm_final_g.reshape(1, d), t["tt_peer"], t["eb"])
    return out.reshape(batch, seq, d)
```

```python
import functools
import math

import jax
import jax.numpy as jnp
from jax import lax
from jax.experimental import pallas as pl
from jax.experimental.pallas import tpu as pltpu

F32 = jnp.float32
BF16 = jnp.bfloat16

D_MODEL = 1024
HEAD_DIM = 64
A_Q_HEADS = 8
A_KV_HEADS = 2
A_GROUP = A_Q_HEADS // A_KV_HEADS
B_HEADS = 4
B_V_DIM = 2 * HEAD_DIM
GRID_W = 64
AXIAL_THETA = 10000.0
AXIAL_HALF = HEAD_DIM // 2
ROPE_THETA = 500000.0
ROPE_DIMS = HEAD_DIM // 4
NORM_EPS = 1e-6
SUBLN_EPS = 1e-5
A_Q_W = A_Q_HEADS * HEAD_DIM
A_KV_W = A_KV_HEADS * HEAD_DIM
B_QK_W = 2 * B_HEADS * HEAD_DIM
B_V_W = B_HEADS * B_V_DIM
IN_COLS = A_Q_W + 2 * A_KV_W + 2 * B_QK_W + B_V_W
PEER_HEADS = 8
PEER_KEYS = 128
PEER_HALF = 128
PEER_TOPK = 16
LANES = 128

NT_DIMS = (((1,), (1,)), ((), ()))
Q_SCALE = (HEAD_DIM ** -0.5) * math.log2(math.e)

VMEM_LIMIT = 56 * 1024 * 1024


def _cparams(sem, flags=None):
    return pltpu.CompilerParams(dimension_semantics=sem, vmem_limit_bytes=VMEM_LIMIT, flags=flags)


def _rms(x, g, eps):
    return x * lax.rsqrt(jnp.mean(x * x, axis=-1, keepdims=True) + eps) * g


def _group_mean_sq(v, bd):
    v2 = v * v
    hi = v2.astype(BF16)
    lo = (v2 - hi.astype(F32)).astype(BF16)
    return (jnp.dot(hi, bd, preferred_element_type=F32)
            + jnp.dot(lo, bd, preferred_element_type=F32))


def _rot_half(v, half, group):
    width = v.shape[-1]
    lane = lax.broadcasted_iota(jnp.int32, v.shape, 1)
    fwd = pltpu.roll(v, width - half, 1)
    bwd = pltpu.roll(v, half, 1)
    return jnp.where((lane % group) < half, fwd, bwd)


def _tile4(t):
    return jnp.concatenate([t, t, t, t], axis=1)


def _proj_kernel(x_ref, g_ref, w_ref, qg_ref, kg_ref, tab_ref, bd_ref,
                 qa_ref, ka_ref, va_ref, qb_ref, kb_ref, vb_ref):
    xn = _rms(x_ref[...], g_ref[...], NORM_EPS)
    proj = jnp.dot(xn.astype(BF16), w_ref[...], preferred_element_type=F32)
    tab = tab_ref[...]
    cos_a, sin_a = tab[:, 0:128], tab[:, 128:256]
    cos_b, sin_b = tab[:, 256:384], tab[:, 384:512]
    bd = bd_ref[...]

    c0 = 0
    qa = proj[:, c0:c0 + A_Q_W]
    qa = qa * lax.rsqrt(_group_mean_sq(qa, bd) + NORM_EPS) * qg_ref[...]
    qa = qa * _tile4(cos_a) + _rot_half(qa, AXIAL_HALF // 2, AXIAL_HALF) * _tile4(sin_a)
    qa_ref[...] = qa.astype(BF16)
    c0 += A_Q_W

    ka = proj[:, c0:c0 + A_KV_W]
    ka = ka * lax.rsqrt(_group_mean_sq(ka, bd[:A_KV_W, :A_KV_W]) + NORM_EPS) * kg_ref[...]
    ka = ka * cos_a + _rot_half(ka, AXIAL_HALF // 2, AXIAL_HALF) * sin_a
    ka_ref[...] = ka.astype(BF16)
    c0 += A_KV_W

    va_ref[0] = proj[:, c0:c0 + A_KV_W].T.astype(BF16)
    c0 += A_KV_W

    qb = proj[:, c0:c0 + B_QK_W]
    qb = qb * _tile4(cos_b) + _rot_half(qb, ROPE_DIMS // 2, HEAD_DIM) * _tile4(sin_b)
    qb_ref[...] = (qb * Q_SCALE).astype(BF16)
    c0 += B_QK_W

    kb = proj[:, c0:c0 + B_QK_W]
    kb = kb * _tile4(cos_b) + _rot_half(kb, ROPE_DIMS // 2, HEAD_DIM) * _tile4(sin_b)
    kb_ref[...] = kb.astype(BF16)
    c0 += B_QK_W

    vb_ref[0] = proj[:, c0:c0 + B_V_W].T.astype(BF16)


def _rope_tables(seq):
    rows = seq // GRID_W
    row = jnp.repeat(jnp.arange(rows, dtype=F32), GRID_W)
    col = jnp.tile(jnp.arange(GRID_W, dtype=F32), rows)
    pos = jnp.arange(seq, dtype=F32)
    inv_ax = AXIAL_THETA ** (-jnp.arange(0, AXIAL_HALF, 2, dtype=F32) / AXIAL_HALF)
    inv_p = ROPE_THETA ** (-jnp.arange(0, ROPE_DIMS, 2, dtype=F32) / ROPE_DIMS)
    row_ang = row[:, None] * inv_ax[None, :]
    col_ang = col[:, None] * inv_ax[None, :]
    pos_ang = pos[:, None] * inv_p[None, :]
    cr, sr = jnp.cos(row_ang), jnp.sin(row_ang)
    cc, sc = jnp.cos(col_ang), jnp.sin(col_ang)
    cp, sp = jnp.cos(pos_ang), jnp.sin(pos_ang)
    rest = HEAD_DIM - ROPE_DIMS
    cos_a = jnp.concatenate([cr, cr, cc, cc], axis=1)
    sin_a = jnp.concatenate([-sr, sr, -sc, sc], axis=1)
    cos_b = jnp.concatenate([cp, cp, jnp.ones((seq, rest), F32)], axis=1)
    sin_b = jnp.concatenate([-sp, sp, jnp.zeros((seq, rest), F32)], axis=1)
    two = lambda t: jnp.concatenate([t, t], axis=1)
    return jnp.concatenate([two(cos_a), two(sin_a), two(cos_b), two(sin_b)], axis=1)


def _proj_call(x, g, w_in, qg, kg, tab, bd, tm):
    seq = x.shape[0]
    row = lambda i: (i, 0)
    fix = lambda i: (0, 0)
    n = seq // tm
    rows_out = lambda w: (pl.BlockSpec((tm, w), row), jax.ShapeDtypeStruct((seq, w), BF16))
    cols_out = lambda w: (pl.BlockSpec((1, w, tm), lambda i: (i, 0, 0)), jax.ShapeDtypeStruct((n, w, tm), BF16))
    outs = [rows_out(A_Q_W), rows_out(A_KV_W), cols_out(A_KV_W), rows_out(B_QK_W), rows_out(B_QK_W), cols_out(B_V_W)]
    return pl.pallas_call(
        _proj_kernel,
        grid=(seq // tm,),
        in_specs=[
            pl.BlockSpec((tm, D_MODEL), row),
            pl.BlockSpec((1, D_MODEL), fix),
            pl.BlockSpec((D_MODEL, IN_COLS), fix),
            pl.BlockSpec((1, A_Q_W), fix),
            pl.BlockSpec((1, A_KV_W), fix),
            pl.BlockSpec((tm, 512), row),
            pl.BlockSpec((A_Q_W, A_Q_W), fix),
        ],
        out_specs=[o[0] for o in outs],
        out_shape=[o[1] for o in outs],
        compiler_params=_cparams(("parallel",)),
        name="proj",
    )(x, g, w_in, qg, kg, tab, bd)


SOFTMAX_ROWS = 32


def _flash_scratch(nq, tk, dv):
    return [pltpu.VMEM((2, tk, nq), F32), pltpu.VMEM((2, tk, nq), BF16), pltpu.VMEM((dv, nq), F32)]


def _flash_cols(qst, k_ref, vt_ref, s_ref, p_ref, acc_ref):
    nq = qst.shape[1]
    tk = s_ref.shape[1]
    per = tk // vt_ref.shape[2]
    n_chunks = vt_ref.shape[0] // per
    assert n_chunks % 2 == 0 and tk % SOFTMAX_ROWS == 0
    n_sub = tk // SOFTMAX_ROWS

    def scores(j, slot):
        k = k_ref[pl.ds(pl.multiple_of(j * tk, tk), tk), :]
        s_ref[slot] = jnp.dot(k, qst, preferred_element_type=F32)

    def softmax(slot, m, l):
        sub = lambda b: s_ref[slot, b * SOFTMAX_ROWS:(b + 1) * SOFTMAX_ROWS, :]
        mx = sub(0)
        for b in range(1, n_sub):
            mx = jnp.maximum(mx, sub(b))
        m_new = jnp.maximum(m, jnp.max(mx, axis=0, keepdims=True))
        alpha = jnp.exp2(m - m_new)
        psum = jnp.zeros((8, nq), F32)
        for b in range(n_sub):
            p = jnp.exp2(sub(b) - m_new)
            psum = psum + jnp.sum(p.reshape(SOFTMAX_ROWS // 8, 8, nq), axis=0)
            p_ref[slot, b * SOFTMAX_ROWS:(b + 1) * SOFTMAX_ROWS, :] = p.astype(BF16)
        return alpha, m_new, alpha * l + psum

    def weighted(j, slot, alpha):
        vt = jnp.concatenate([vt_ref[per * j + c] for c in range(per)], axis=1)
        acc_ref[...] = alpha * acc_ref[...] + jnp.dot(vt, p_ref[slot], preferred_element_type=F32)

    def step(j, slot, a_prev, m, l):
        scores(j + 1, 1 - slot)
        alpha, m, l = softmax(slot, m, l)
        weighted(j - 1, 1 - slot, a_prev)
        return alpha, m, l

    def body(t, carry):
        a_prev, m, l = carry
        a_prev, m, l = step(2 * t + 1, 1, a_prev, m, l)
        return step(2 * t + 2, 0, a_prev, m, l)

    scores(0, 0)
    acc_ref[...] = jnp.zeros_like(acc_ref)
    carry = softmax(0, jnp.full((1, nq), -jnp.inf, F32), jnp.zeros((8, nq), F32))
    scores(1, 1)
    a_prev, m, l = lax.fori_loop(0, (n_chunks - 2) // 2, body, carry)
    alpha, m, l = softmax(1, m, l)
    weighted(n_chunks - 2, 0, a_prev)
    weighted(n_chunks - 1, 1, alpha)
    return acc_ref[...], jnp.sum(l, axis=0, keepdims=True)


def _gqa_kernel(q_ref, k_ref, vt_ref, o_ref, s_ref, p_ref, acc_ref, *, tq):
    g = pl.program_id(0)
    lane = lax.broadcasted_iota(jnp.int32, (tq, LANES), 1)
    in_g = (lane // HEAD_DIM) == g
    rows = []
    for hh in range(A_GROUP):
        qp = q_ref[:, LANES * (hh // 2):LANES * (hh // 2 + 1)].astype(F32)
        aligned = jnp.where((hh % 2) == g, qp, pltpu.roll(qp, HEAD_DIM, 1))
        rows.append(jnp.where(in_g, aligned, 0.0))
    qs = jnp.concatenate(rows, axis=0)
    acc_t, l = _flash_cols(qs.T.astype(BF16), k_ref, vt_ref, s_ref, p_ref, acc_ref)
    o_t = acc_t / l
    o = jnp.concatenate([o_t, jnp.zeros_like(o_t)], axis=0).T
    left_half = lane < HEAD_DIM
    for p in range(A_GROUP // 2):
        a = o[(2 * p) * tq:(2 * p + 1) * tq]
        b = o[(2 * p + 1) * tq:(2 * p + 2) * tq]
        o_ref[:, LANES * p:LANES * (p + 1)] = jnp.where(left_half, a, pltpu.roll(b, HEAD_DIM, 1)).astype(BF16)


def _gqa_call(qa, ka, vat, tq, tk):
    seq = qa.shape[0]
    return pl.pallas_call(
        functools.partial(_gqa_kernel, tq=tq),
        grid=(A_KV_HEADS, seq // tq),
        in_specs=[
            pl.BlockSpec((tq, A_GROUP * HEAD_DIM), lambda g, i: (i, g)),
            pl.BlockSpec((seq, A_KV_W), lambda g, i: (0, 0)),
            pl.BlockSpec((vat.shape[0], HEAD_DIM, vat.shape[2]), lambda g, i: (0, g, 0)),
        ],
        out_specs=pl.BlockSpec((tq, A_GROUP * HEAD_DIM), lambda g, i: (i, g)),
        out_shape=jax.ShapeDtypeStruct((seq, A_Q_W), BF16),
        scratch_shapes=_flash_scratch(A_GROUP * tq, tk, HEAD_DIM),
        compiler_params=_cparams(("parallel", "parallel")),
        name="gqa",
    )(qa, ka, vat)


def _diff_kernel(lam_ref, q_ref, k_ref, vt_ref, sg_ref, o_ref, s_ref, p_ref, acc_ref, *, tq, lambda_init):
    lane = lax.broadcasted_iota(jnp.int32, (tq, LANES), 1)
    q = q_ref[...].astype(F32)
    qs = jnp.concatenate([jnp.where(lane < HEAD_DIM, q, 0.0),
                          jnp.where(lane >= HEAD_DIM, q, 0.0)], axis=0)
    acc_t, l = _flash_cols(qs.T.astype(BF16), k_ref, vt_ref, s_ref, p_ref, acc_ref)
    o = (acc_t / l).T
    lv = lam_ref[...]
    lam = (jnp.exp(jnp.sum(lv[0:1] * lv[1:2], axis=-1, keepdims=True))
           - jnp.exp(jnp.sum(lv[2:3] * lv[3:4], axis=-1, keepdims=True)) + lambda_init)
    ob = o[:tq] - lam * o[tq:]
    ob = _rms(ob, sg_ref[...], SUBLN_EPS) * (1.0 - lambda_init)
    o_ref[...] = ob.astype(BF16)


def _diff_call(lam_vecs, qb, kb, vbt, subln_g, tq, tk, lambda_init):
    seq = qb.shape[0]
    n_slabs, _, slab = vbt.shape
    return pl.pallas_call(
        functools.partial(_diff_kernel, tq=tq, lambda_init=lambda_init),
        grid=(B_HEADS, seq // tq),
        in_specs=[
            pl.BlockSpec((4, HEAD_DIM), lambda h, i: (0, 0)),
            pl.BlockSpec((tq, LANES), lambda h, i: (i, h)),
            pl.BlockSpec((seq, LANES), lambda h, i: (0, h)),
            pl.BlockSpec((n_slabs, B_V_DIM, slab), lambda h, i: (0, h, 0)),
            pl.BlockSpec((1, B_V_DIM), lambda h, i: (0, 0)),
        ],
        out_specs=pl.BlockSpec((tq, LANES), lambda h, i: (i, h)),
        out_shape=jax.ShapeDtypeStruct((seq, B_V_W), BF16),
        scratch_shapes=_flash_scratch(2 * tq, tk, B_V_DIM),
        compiler_params=_cparams(("parallel", "parallel")),
        name="diff",
    )(lam_vecs, qb, kb, vbt, subln_g)


def _mid_kernel(oa_ref, ob_ref, x_ref, wo_ref, g_ref, wq_ref, sk_ref, h_ref, xn_ref, sc_ref):
    o = jnp.concatenate([oa_ref[...], ob_ref[...]], axis=1)
    h = x_ref[...] + jnp.dot(o, wo_ref[...], preferred_element_type=F32)
    h_ref[...] = h
    xn = _rms(h, g_ref[...], NORM_EPS).astype(BF16)
    xn_ref[...] = xn
    q = jnp.dot(xn, wq_ref[...], preferred_element_type=F32).astype(BF16)
    for hp in range(2 * PEER_HEADS):
        sc_ref[hp] = lax.dot_general(sk_ref[hp], q[:, PEER_HALF * hp:PEER_HALF * (hp + 1)],
                                     NT_DIMS, preferred_element_type=F32)


def _mid_call(oa, ob, x, w_out, g, w_query, sub_keys, tm):
    seq = x.shape[0]
    row = lambda i: (i, 0)
    fix = lambda i: (0, 0)
    nq = 2 * PEER_HEADS * PEER_HALF
    return pl.pallas_call(
        _mid_kernel,
        grid=(seq // tm,),
        in_specs=[
            pl.BlockSpec((tm, A_Q_W), row),
            pl.BlockSpec((tm, B_V_W), row),
            pl.BlockSpec((tm, D_MODEL), row),
            pl.BlockSpec((D_MODEL, D_MODEL), fix),
            pl.BlockSpec((1, D_MODEL), fix),
            pl.BlockSpec((D_MODEL, nq), fix),
            pl.BlockSpec((2 * PEER_HEADS, PEER_KEYS, PEER_HALF), lambda i: (0, 0, 0)),
        ],
        out_specs=[
            pl.BlockSpec((tm, D_MODEL), row),
            pl.BlockSpec((tm, D_MODEL), row),
            pl.BlockSpec((2 * PEER_HEADS, PEER_KEYS, tm), lambda i: (0, 0, i)),
        ],
        out_shape=[
            jax.ShapeDtypeStruct((seq, D_MODEL), F32),
            jax.ShapeDtypeStruct((seq, D_MODEL), BF16),
            jax.ShapeDtypeStruct((2 * PEER_HEADS, PEER_KEYS, seq), F32),
        ],
        compiler_params=_cparams(("parallel",)),
        name="mid",
    )(oa, ob, x, w_out, g, w_query, sub_keys)


N_TOP = PEER_TOPK + 1
_CAND = [(a, b) for a in range(N_TOP) for b in range(N_TOP) if (a + 1) * (b + 1) <= N_TOP]


def _top_vals(s, n, want_rank=False):
    vals = []
    rank = jnp.full(s.shape, float(n), F32) if want_rank else None
    for r in range(n):
        m = jnp.max(s, axis=0, keepdims=True)
        vals.append(m)
        hit = s >= m
        if want_rank:
            rank = jnp.where(hit, float(r), rank)
        if r + 1 < n:
            s = jnp.where(hit, -jnp.inf, s)
    return (vals, rank) if want_rank else vals


def _topk_kernel(sc_ref, e1_ref, n1_ref, e2_ref, r2_ref):
    t = sc_ref.shape[-1]
    for h in range(PEER_HEADS):
        s1 = sc_ref[2 * h]
        s2 = sc_ref[2 * h + 1]
        v1, rank1 = _top_vals(s1, N_TOP, want_rank=True)
        v2, rank2 = _top_vals(s2, N_TOP, want_rank=True)
        cand = {ab: v1[ab[0]] + v2[ab[1]] for ab in _CAND}
        rows = list(cand.values())
        rows += [jnp.full((1, t), -jnp.inf, F32)] * ((-len(rows)) % 8)
        c = jnp.concatenate(rows, axis=0)
        top = _top_vals(c, N_TOP)
        cut = 0.5 * (top[PEER_TOPK - 1] + top[PEER_TOPK])
        best = v1[0] + v2[0]
        z = jnp.sum(jnp.where(c >= cut, jnp.exp(c - best), 0.0), axis=0, keepdims=True)
        n1 = jnp.zeros_like(s1)
        for a in range(N_TOP):
            n_a = sum((cand[(a, b)] >= cut).astype(F32) for b in range(N_TOP) if (a, b) in cand)
            n1 = jnp.where(rank1 == float(a), n_a, n1)
        e1_ref[h] = jnp.exp(s1 - v1[0]) / z
        n1_ref[h] = n1
        e2_ref[h] = jnp.exp(s2 - v2[0])
        r2_ref[h] = rank2


def _topk_call(sc, tt):
    seq = sc.shape[-1]
    blk = lambda i: (0, 0, i)
    return pl.pallas_call(
        _topk_kernel,
        grid=(seq // tt,),
        in_specs=[pl.BlockSpec((2 * PEER_HEADS, PEER_KEYS, tt), blk)],
        out_specs=[pl.BlockSpec((PEER_HEADS, PEER_KEYS, tt), blk)] * 4,
        out_shape=[jax.ShapeDtypeStruct((PEER_HEADS, PEER_KEYS, seq), F32)] * 4,
        compiler_params=_cparams(("parallel",)),
        name="topk",
    )(sc)


BF16_ROWS = 16


def _peer_kernel(xn_ref, u_ref, vt_ref, e1_ref, n1_ref, e2_ref, r2_ref, h_ref, g_ref, o_ref,
                 acc_ref, a_ref, w_ref, e2b_ref, r2b_ref, *, eb):
    e = pl.program_id(1)
    tt = xn_ref.shape[0]
    inv_sqrt2 = 1.0 / math.sqrt(2.0)
    n_pack = PEER_KEYS // BF16_ROWS

    @pl.when(e == 0)
    def _():
        e2b_ref[...] = e2_ref[...].astype(BF16)
        r2b_ref[...] = r2_ref[...].astype(BF16)

    a_ref[...] = lax.dot_general(u_ref[...], xn_ref[...], NT_DIMS,
                                 preferred_element_type=F32)
    for ii in range(eb // PEER_KEYS):
        i = e * (eb // PEER_KEYS) + ii
        rows = slice(PEER_KEYS * ii, PEER_KEYS * (ii + 1))
        tile_row = lambda ref, h: jnp.broadcast_to(ref[h, pl.ds(i, 1), :], (BF16_ROWS, tt)).astype(BF16)
        e1_rows = [tile_row(e1_ref, h) for h in range(PEER_HEADS)]
        n1_rows = [tile_row(n1_ref, h) for h in range(PEER_HEADS)]
        for tc in range(tt // LANES):
            cols = slice(LANES * tc, LANES * (tc + 1))
            a = a_ref[rows, cols]
            act = 0.5 * a * (1.0 + lax.erf(a * inv_sqrt2))
            gate = None
            for h in range(PEER_HEADS):
                e2 = e2b_ref[h, :, cols]
                r2 = r2b_ref[h, :, cols]
                e1_b = jnp.concatenate([e1_rows[h][:, cols]] * n_pack, axis=0)
                n1_b = jnp.concatenate([n1_rows[h][:, cols]] * n_pack, axis=0)
                term = jnp.where(r2 < n1_b, e2 * e1_b, jnp.zeros_like(e2))
                gate = term if gate is None else gate + term
            w_ref[rows, cols] = gate * act.astype(BF16)
    part = jnp.dot(vt_ref[...], w_ref[...], preferred_element_type=F32)

    @pl.when(e == 0)
    def _():
        acc_ref[...] = part

    @pl.when(e > 0)
    def _():
        acc_ref[...] += part

    @pl.when(e == pl.num_programs(1) - 1)
    def _():
        out = h_ref[...] + acc_ref[...].T
        o_ref[...] = _rms(out, g_ref[...], NORM_EPS)


def _peer_call(xn, u, vt, e1, n1, e2, r2, h, g, tt, eb):
    seq = xn.shape[0]
    n_exp = u.shape[0]
    tok = lambda t, e: (t, 0)
    tok3 = lambda t, e: (0, 0, t)
    return pl.pallas_call(
        functools.partial(_peer_kernel, eb=eb),
        grid=(seq // tt, n_exp // eb),
        in_specs=[
            pl.BlockSpec((tt, D_MODEL), tok),
            pl.BlockSpec((eb, D_MODEL), lambda t, e: (e, 0)),
            pl.BlockSpec((D_MODEL, eb), lambda t, e: (0, e)),
            pl.BlockSpec((PEER_HEADS, PEER_KEYS, tt), tok3),
            pl.BlockSpec((PEER_HEADS, PEER_KEYS, tt), tok3),
            pl.BlockSpec((PEER_HEADS, PEER_KEYS, tt), tok3),
            pl.BlockSpec((PEER_HEADS, PEER_KEYS, tt), tok3),
            pl.BlockSpec((tt, D_MODEL), tok),
            pl.BlockSpec((1, D_MODEL), lambda t, e: (0, 0)),
        ],
        out_specs=pl.BlockSpec((tt, D_MODEL), tok),
        out_shape=jax.ShapeDtypeStruct((seq, D_MODEL), F32),
        scratch_shapes=[pltpu.VMEM((D_MODEL, tt), F32), pltpu.VMEM((eb, tt), F32), pltpu.VMEM((eb, tt), BF16),
                        pltpu.VMEM((PEER_HEADS, PEER_KEYS, tt), BF16), pltpu.VMEM((PEER_HEADS, PEER_KEYS, tt), BF16)],
        compiler_params=_cparams(("parallel", "arbitrary")),
        name="peer",
    )(xn, u, vt, e1, n1, e2, r2, h, g)


def _tiles(seq):
    big = seq >= 4096
    return dict(
        tm=512 if big else 256,
        tq_a=128, tq_b=256,
        tk=1024 if big else 512,
        tt_topk=256, tt_peer=512 if big else 256, eb=1024,
    )


def kernel(x, norm_attn_g, w_in, q_norm_g, k_norm_g, lambda_q1, lambda_k1, lambda_q2, lambda_k2,
           subln_g, w_out, norm_ffn_g, w_query, sub_keys, expert_u, expert_v, norm_final_g):
    batch, seq, d = x.shape
    assert batch == 1 and d == D_MODEL and norm_attn_g.shape[0] == 1
    t = _tiles(seq)
    lambda_init = 0.8 - 0.6 * math.exp(-0.3 * 0)
    x2 = x.reshape(seq, d)

    tab = _rope_tables(seq)
    blk = jnp.arange(A_Q_W) // HEAD_DIM
    bd = jnp.where(blk[:, None] == blk[None, :], 1.0 / HEAD_DIM, 0.0).astype(BF16)
    qg = (jnp.tile(q_norm_g[0], A_Q_HEADS) * Q_SCALE).reshape(1, A_Q_W)
    kg = jnp.tile(k_norm_g[0], A_KV_HEADS).reshape(1, A_KV_W)

    qa, ka, vat, qb, kb, vbt = _proj_call(x2, norm_attn_g, w_in[0].astype(BF16), qg, kg, tab, bd, t["tm"])
    oa = _gqa_call(qa, ka, vat, t["tq_a"], t["tk"])
    lam_vecs = jnp.concatenate([lambda_q1, lambda_k1, lambda_q2, lambda_k2], axis=0)
    ob = _diff_call(lam_vecs, qb, kb, vbt, subln_g, t["tq_b"], t["tk"], lambda_init)

    sk = sub_keys[0].reshape(2 * PEER_HEADS, PEER_KEYS, PEER_HALF).astype(BF16)
    h, xn, sc = _mid_call(oa, ob, x2, w_out[0].astype(BF16), norm_ffn_g, w_query[0].astype(BF16), sk, t["tm"])
    e1, n1, e2, r2 = _topk_call(sc, t["tt_topk"])
    out = _peer_call(xn, expert_u[0].astype(BF16), expert_v[0].T.astype(BF16), e1, n1, e2, r2, h,
                     norm_final_g.reshape(1, d), t["tt_peer"], t["eb"])
    return out.reshape(batch, seq, d)
```

```python
import functools
import math

import jax
import jax.numpy as jnp
from jax import lax
from jax.experimental import pallas as pl
from jax.experimental.pallas import tpu as pltpu

F32 = jnp.float32
BF16 = jnp.bfloat16

D_MODEL = 1024
HEAD_DIM = 64
A_Q_HEADS = 8
A_KV_HEADS = 2
A_GROUP = A_Q_HEADS // A_KV_HEADS
B_HEADS = 4
B_V_DIM = 2 * HEAD_DIM
GRID_W = 64
AXIAL_THETA = 10000.0
AXIAL_HALF = HEAD_DIM // 2
ROPE_THETA = 500000.0
ROPE_DIMS = HEAD_DIM // 4
NORM_EPS = 1e-6
SUBLN_EPS = 1e-5
A_Q_W = A_Q_HEADS * HEAD_DIM
A_KV_W = A_KV_HEADS * HEAD_DIM
B_QK_W = 2 * B_HEADS * HEAD_DIM
B_V_W = B_HEADS * B_V_DIM
IN_COLS = A_Q_W + 2 * A_KV_W + 2 * B_QK_W + B_V_W
PEER_HEADS = 8
PEER_KEYS = 128
PEER_HALF = 128
PEER_TOPK = 16
LANES = 128

NT_DIMS = (((1,), (1,)), ((), ()))
Q_SCALE = (HEAD_DIM ** -0.5) * math.log2(math.e)

VMEM_LIMIT = 56 * 1024 * 1024


def _cparams(sem, flags=None):
    return pltpu.CompilerParams(dimension_semantics=sem, vmem_limit_bytes=VMEM_LIMIT, flags=flags)


def _rms(x, g, eps):
    return x * lax.rsqrt(jnp.mean(x * x, axis=-1, keepdims=True) + eps) * g


def _group_mean_sq(v, bd):
    v2 = v * v
    hi = v2.astype(BF16)
    lo = (v2 - hi.astype(F32)).astype(BF16)
    return (jnp.dot(hi, bd, preferred_element_type=F32)
            + jnp.dot(lo, bd, preferred_element_type=F32))


def _rot_half(v, half, group):
    width = v.shape[-1]
    lane = lax.broadcasted_iota(jnp.int32, v.shape, 1)
    fwd = pltpu.roll(v, width - half, 1)
    bwd = pltpu.roll(v, half, 1)
    return jnp.where((lane % group) < half, fwd, bwd)


def _tile4(t):
    return jnp.concatenate([t, t, t, t], axis=1)


def _proj_kernel(x_ref, g_ref, w_ref, qg_ref, kg_ref, tab_ref, bd_ref,
                 qa_ref, ka_ref, va_ref, qb_ref, kb_ref, vb_ref):
    xn = _rms(x_ref[...], g_ref[...], NORM_EPS)
    proj = jnp.dot(xn.astype(BF16), w_ref[...], preferred_element_type=F32)
    tab = tab_ref[...]
    cos_a, sin_a = tab[:, 0:128], tab[:, 128:256]
    cos_b, sin_b = tab[:, 256:384], tab[:, 384:512]
    bd = bd_ref[...]

    c0 = 0
    qa = proj[:, c0:c0 + A_Q_W]
    qa = qa * lax.rsqrt(_group_mean_sq(qa, bd) + NORM_EPS) * qg_ref[...]
    qa = qa * _tile4(cos_a) + _rot_half(qa, AXIAL_HALF // 2, AXIAL_HALF) * _tile4(sin_a)
    qa_ref[...] = qa.astype(BF16)
    c0 += A_Q_W

    ka = proj[:, c0:c0 + A_KV_W]
    ka = ka * lax.rsqrt(_group_mean_sq(ka, bd[:A_KV_W, :A_KV_W]) + NORM_EPS) * kg_ref[...]
    ka = ka * cos_a + _rot_half(ka, AXIAL_HALF // 2, AXIAL_HALF) * sin_a
    ka_ref[...] = ka.astype(BF16)
    c0 += A_KV_W

    va_ref[0] = proj[:, c0:c0 + A_KV_W].T.astype(BF16)
    c0 += A_KV_W

    qb = proj[:, c0:c0 + B_QK_W]
    qb = qb * _tile4(cos_b) + _rot_half(qb, ROPE_DIMS // 2, HEAD_DIM) * _tile4(sin_b)
    qb_ref[...] = (qb * Q_SCALE).astype(BF16)
    c0 += B_QK_W

    kb = proj[:, c0:c0 + B_QK_W]
    kb = kb * _tile4(cos_b) + _rot_half(kb, ROPE_DIMS // 2, HEAD_DIM) * _tile4(sin_b)
    kb_ref[...] = kb.astype(BF16)
    c0 += B_QK_W

    vb_ref[0] = proj[:, c0:c0 + B_V_W].T.astype(BF16)


def _rope_tables(seq):
    rows = seq // GRID_W
    row = jnp.repeat(jnp.arange(rows, dtype=F32), GRID_W)
    col = jnp.tile(jnp.arange(GRID_W, dtype=F32), rows)
    pos = jnp.arange(seq, dtype=F32)
    inv_ax = AXIAL_THETA ** (-jnp.arange(0, AXIAL_HALF, 2, dtype=F32) / AXIAL_HALF)
    inv_p = ROPE_THETA ** (-jnp.arange(0, ROPE_DIMS, 2, dtype=F32) / ROPE_DIMS)
    row_ang = row[:, None] * inv_ax[None, :]
    col_ang = col[:, None] * inv_ax[None, :]
    pos_ang = pos[:, None] * inv_p[None, :]
    cr, sr = jnp.cos(row_ang), jnp.sin(row_ang)
    cc, sc = jnp.cos(col_ang), jnp.sin(col_ang)
    cp, sp = jnp.cos(pos_ang), jnp.sin(pos_ang)
    rest = HEAD_DIM - ROPE_DIMS
    cos_a = jnp.concatenate([cr, cr, cc, cc], axis=1)
    sin_a = jnp.concatenate([-sr, sr, -sc, sc], axis=1)
    cos_b = jnp.concatenate([cp, cp, jnp.ones((seq, rest), F32)], axis=1)
    sin_b = jnp.concatenate([-sp, sp, jnp.zeros((seq, rest), F32)], axis=1)
    two = lambda t: jnp.concatenate([t, t], axis=1)
    return jnp.concatenate([two(cos_a), two(sin_a), two(cos_b), two(sin_b)], axis=1)


def _proj_call(x, g, w_in, qg, kg, tab, bd, tm):
    seq = x.shape[0]
    row = lambda i: (i, 0)
    fix = lambda i: (0, 0)
    n = seq // tm
    rows_out = lambda w: (pl.BlockSpec((tm, w), row), jax.ShapeDtypeStruct((seq, w), BF16))
    cols_out = lambda w: (pl.BlockSpec((1, w, tm), lambda i: (i, 0, 0)), jax.ShapeDtypeStruct((n, w, tm), BF16))
    outs = [rows_out(A_Q_W), rows_out(A_KV_W), cols_out(A_KV_W), rows_out(B_QK_W), rows_out(B_QK_W), cols_out(B_V_W)]
    return pl.pallas_call(
        _proj_kernel,
        grid=(seq // tm,),
        in_specs=[
            pl.BlockSpec((tm, D_MODEL), row),
            pl.BlockSpec((1, D_MODEL), fix),
            pl.BlockSpec((D_MODEL, IN_COLS), fix),
            pl.BlockSpec((1, A_Q_W), fix),
            pl.BlockSpec((1, A_KV_W), fix),
            pl.BlockSpec((tm, 512), row),
            pl.BlockSpec((A_Q_W, A_Q_W), fix),
        ],
        out_specs=[o[0] for o in outs],
        out_shape=[o[1] for o in outs],
        compiler_params=_cparams(("parallel",)),
        name="proj",
    )(x, g, w_in, qg, kg, tab, bd)


SOFTMAX_ROWS = 32


N_SLOTS = 3
STEP_PARTS = 1
STEPS_PER_TRIP = 3
ST_M, ST_ALPHA, ST_NEXT_MAX, ST_L = 0, 1, 2, 8


def _flash_scratch(nq, tk, dv):
    return ([pltpu.VMEM((tk, nq), F32)] * N_SLOTS + [pltpu.VMEM((tk, nq), BF16)] * N_SLOTS
            + [pltpu.VMEM((dv, nq), F32), pltpu.VMEM((16, nq), F32)])


def _flash_cols(qst, k_ref, vt_ref, scratch):
    s_ref, p_ref = scratch[:N_SLOTS], scratch[N_SLOTS:2 * N_SLOTS]
    acc_ref, st_ref = scratch[2 * N_SLOTS:]
    return _flash_cols_impl(qst, k_ref, vt_ref, s_ref, p_ref, acc_ref, st_ref)


def _flash_cols_impl(qst, k_ref, vt_ref, s_ref, p_ref, acc_ref, st_ref):
    nq = qst.shape[1]
    tk = s_ref[0].shape[0]
    per = tk // vt_ref.shape[2]
    n_chunks = vt_ref.shape[0] // per
    assert n_chunks >= N_SLOTS and tk % SOFTMAX_ROWS == 0
    n_sub = tk // SOFTMAX_ROWS
    row = lambda r, n=1: slice(r, r + n)
    sub = lambda slot, b: s_ref[slot][b * SOFTMAX_ROWS:(b + 1) * SOFTMAX_ROWS, :]

    def scores(j, slot):
        k = k_ref[pl.ds(j * tk, tk), :]
        s_ref[slot][...] = jnp.dot(k, qst, preferred_element_type=F32)

    def column_max(slot):
        mx = sub(slot, 0)
        for b in range(1, n_sub):
            mx = jnp.maximum(mx, sub(slot, b))
        st_ref[row(ST_NEXT_MAX), :] = jnp.max(mx, axis=0, keepdims=True)

    def weighted(j, slot, alpha):
        vt = jnp.concatenate([vt_ref[per * j + c] for c in range(per)], axis=1)
        acc_ref[...] = alpha * acc_ref[...] + jnp.dot(vt, p_ref[slot][...], preferred_element_type=F32)

    def step(j, slot, with_scores=True, with_max=True):
        nxt1, nxt2 = (slot + 1) % N_SLOTS, (slot + 2) % N_SLOTS
        a_prev = st_ref[row(ST_ALPHA), :]
        m_old = st_ref[row(ST_M), :]
        m_new = jnp.maximum(m_old, st_ref[row(ST_NEXT_MAX), :])
        alpha = jnp.exp2(m_old - m_new)
        j_prev = jnp.maximum(j - 1, 0)
        rows_q = tk // STEP_PARTS
        subs_q = n_sub // STEP_PARTS
        slab = vt_ref.shape[2]
        psum = jnp.zeros((8, nq), F32)
        mx = None
        part = None
        for q in range(STEP_PARTS):
            r0 = q * rows_q
            if with_scores:
                start = (j + 2) * tk + r0
                start = start if isinstance(start, int) else pl.multiple_of(start, rows_q)
                k = k_ref[pl.ds(start, rows_q), :]
                s_ref[nxt2][r0:r0 + rows_q, :] = jnp.dot(k, qst, preferred_element_type=F32)
            for b in range(q * subs_q, (q + 1) * subs_q):
                p = jnp.exp2(sub(slot, b) - m_new)
                psum = psum + jnp.sum(p.reshape(SOFTMAX_ROWS // 8, 8, nq), axis=0)
                p_ref[slot][b * SOFTMAX_ROWS:(b + 1) * SOFTMAX_ROWS, :] = p.astype(BF16)
            pieces = [vt_ref[per * j_prev + c // slab][:, c % slab:c % slab + min(slab, rows_q)]
                      for c in range(r0, r0 + rows_q, min(slab, rows_q))]
            vt = pieces[0] if len(pieces) == 1 else jnp.concatenate(pieces, axis=1)
            d = jnp.dot(vt, p_ref[nxt2][r0:r0 + rows_q, :], preferred_element_type=F32)
            part = d if part is None else part + d
            if with_max:
                for b in range(q * subs_q, (q + 1) * subs_q):
                    mx = sub(nxt1, b) if mx is None else jnp.maximum(mx, sub(nxt1, b))
        acc_ref[...] = a_prev * acc_ref[...] + part
        st_ref[row(ST_L, 8), :] = alpha * st_ref[row(ST_L, 8), :] + psum
        st_ref[row(ST_M), :] = m_new
        st_ref[row(ST_ALPHA), :] = alpha
        if with_max:
            st_ref[row(ST_NEXT_MAX), :] = jnp.max(mx, axis=0, keepdims=True)

    scores(0, 0)
    scores(1, 1)
    st_ref[...] = jnp.zeros_like(st_ref)
    st_ref[row(ST_M), :] = jnp.full((1, nq), -jnp.inf, F32)
    acc_ref[...] = jnp.zeros_like(acc_ref)
    p_ref[N_SLOTS - 1][...] = jnp.zeros((tk, nq), BF16)
    column_max(0)
    n_full = n_chunks - 2
    n_trips = n_full // STEPS_PER_TRIP
    lead = n_full - n_trips * STEPS_PER_TRIP
    for j in range(lead):
        step(j, j % N_SLOTS)

    def body(t, carry):
        for u in range(STEPS_PER_TRIP):
            step(lead + t * STEPS_PER_TRIP + u, (lead + u) % N_SLOTS)
        return carry

    lax.fori_loop(0, n_trips, body, 0)
    step(n_chunks - 2, (n_chunks - 2) % N_SLOTS, with_scores=False)
    step(n_chunks - 1, (n_chunks - 1) % N_SLOTS, with_scores=False, with_max=False)
    weighted(n_chunks - 1, (n_chunks - 1) % N_SLOTS, st_ref[row(ST_ALPHA), :])
    return acc_ref[...], jnp.sum(st_ref[row(ST_L, 8), :], axis=0, keepdims=True)


def _gqa_kernel(q_ref, k_ref, vt_ref, o_ref, *scratch, tq):
    g = pl.program_id(0)
    lane = lax.broadcasted_iota(jnp.int32, (tq, LANES), 1)
    in_g = (lane // HEAD_DIM) == g
    rows = []
    for hh in range(A_GROUP):
        qp = q_ref[:, LANES * (hh // 2):LANES * (hh // 2 + 1)].astype(F32)
        aligned = jnp.where((hh % 2) == g, qp, pltpu.roll(qp, HEAD_DIM, 1))
        rows.append(jnp.where(in_g, aligned, 0.0))
    qs = jnp.concatenate(rows, axis=0)
    acc_t, l = _flash_cols(qs.T.astype(BF16), k_ref, vt_ref, scratch)
    o_t = acc_t / l
    o = jnp.concatenate([o_t, jnp.zeros_like(o_t)], axis=0).T
    left_half = lane < HEAD_DIM
    for p in range(A_GROUP // 2):
        a = o[(2 * p) * tq:(2 * p + 1) * tq]
        b = o[(2 * p + 1) * tq:(2 * p + 2) * tq]
        o_ref[:, LANES * p:LANES * (p + 1)] = jnp.where(left_half, a, pltpu.roll(b, HEAD_DIM, 1)).astype(BF16)


def _gqa_call(qa, ka, vat, tq, tk):
    seq = qa.shape[0]
    return pl.pallas_call(
        functools.partial(_gqa_kernel, tq=tq),
        grid=(A_KV_HEADS, seq // tq),
        in_specs=[
            pl.BlockSpec((tq, A_GROUP * HEAD_DIM), lambda g, i: (i, g)),
            pl.BlockSpec((seq, A_KV_W), lambda g, i: (0, 0)),
            pl.BlockSpec((vat.shape[0], HEAD_DIM, vat.shape[2]), lambda g, i: (0, g, 0)),
        ],
        out_specs=pl.BlockSpec((tq, A_GROUP * HEAD_DIM), lambda g, i: (i, g)),
        out_shape=jax.ShapeDtypeStruct((seq, A_Q_W), BF16),
        scratch_shapes=_flash_scratch(A_GROUP * tq, tk, HEAD_DIM),
        compiler_params=_cparams(("parallel", "parallel")),
        name="gqa",
    )(qa, ka, vat)


def _diff_kernel(lam_ref, q_ref, k_ref, vt_ref, sg_ref, o_ref, *scratch, tq, lambda_init):
    lane = lax.broadcasted_iota(jnp.int32, (tq, LANES), 1)
    q = q_ref[...].astype(F32)
    qs = jnp.concatenate([jnp.where(lane < HEAD_DIM, q, 0.0),
                          jnp.where(lane >= HEAD_DIM, q, 0.0)], axis=0)
    acc_t, l = _flash_cols(qs.T.astype(BF16), k_ref, vt_ref, scratch)
    o = (acc_t / l).T
    lv = lam_ref[...]
    lam = (jnp.exp(jnp.sum(lv[0:1] * lv[1:2], axis=-1, keepdims=True))
           - jnp.exp(jnp.sum(lv[2:3] * lv[3:4], axis=-1, keepdims=True)) + lambda_init)
    ob = o[:tq] - lam * o[tq:]
    ob = _rms(ob, sg_ref[...], SUBLN_EPS) * (1.0 - lambda_init)
    o_ref[...] = ob.astype(BF16)


def _diff_call(lam_vecs, qb, kb, vbt, subln_g, tq, tk, lambda_init):
    seq = qb.shape[0]
    n_slabs, _, slab = vbt.shape
    return pl.pallas_call(
        functools.partial(_diff_kernel, tq=tq, lambda_init=lambda_init),
        grid=(B_HEADS, seq // tq),
        in_specs=[
            pl.BlockSpec((4, HEAD_DIM), lambda h, i: (0, 0)),
            pl.BlockSpec((tq, LANES), lambda h, i: (i, h)),
            pl.BlockSpec((seq, LANES), lambda h, i: (0, h)),
            pl.BlockSpec((n_slabs, B_V_DIM, slab), lambda h, i: (0, h, 0)),
            pl.BlockSpec((1, B_V_DIM), lambda h, i: (0, 0)),
        ],
        out_specs=pl.BlockSpec((tq, LANES), lambda h, i: (i, h)),
        out_shape=jax.ShapeDtypeStruct((seq, B_V_W), BF16),
        scratch_shapes=_flash_scratch(2 * tq, tk, B_V_DIM),
        compiler_params=_cparams(("parallel", "parallel")),
        name="diff",
    )(lam_vecs, qb, kb, vbt, subln_g)


def _mid_kernel(oa_ref, ob_ref, x_ref, wo_ref, g_ref, wq_ref, sk_ref, h_ref, xn_ref, sc_ref):
    o = jnp.concatenate([oa_ref[...], ob_ref[...]], axis=1)
    h = x_ref[...] + jnp.dot(o, wo_ref[...], preferred_element_type=F32)
    h_ref[...] = h
    xn = _rms(h, g_ref[...], NORM_EPS).astype(BF16)
    xn_ref[...] = xn
    q = jnp.dot(xn, wq_ref[...], preferred_element_type=F32).astype(BF16)
    for hp in range(2 * PEER_HEADS):
        sc_ref[hp] = lax.dot_general(sk_ref[hp], q[:, PEER_HALF * hp:PEER_HALF * (hp + 1)],
                                     NT_DIMS, preferred_element_type=F32)


def _mid_call(oa, ob, x, w_out, g, w_query, sub_keys, tm):
    seq = x.shape[0]
    row = lambda i: (i, 0)
    fix = lambda i: (0, 0)
    nq = 2 * PEER_HEADS * PEER_HALF
    return pl.pallas_call(
        _mid_kernel,
        grid=(seq // tm,),
        in_specs=[
            pl.BlockSpec((tm, A_Q_W), row),
            pl.BlockSpec((tm, B_V_W), row),
            pl.BlockSpec((tm, D_MODEL), row),
            pl.BlockSpec((D_MODEL, D_MODEL), fix),
            pl.BlockSpec((1, D_MODEL), fix),
            pl.BlockSpec((D_MODEL, nq), fix),
            pl.BlockSpec((2 * PEER_HEADS, PEER_KEYS, PEER_HALF), lambda i: (0, 0, 0)),
        ],
        out_specs=[
            pl.BlockSpec((tm, D_MODEL), row),
            pl.BlockSpec((tm, D_MODEL), row),
            pl.BlockSpec((2 * PEER_HEADS, PEER_KEYS, tm), lambda i: (0, 0, i)),
        ],
        out_shape=[
            jax.ShapeDtypeStruct((seq, D_MODEL), F32),
            jax.ShapeDtypeStruct((seq, D_MODEL), BF16),
            jax.ShapeDtypeStruct((2 * PEER_HEADS, PEER_KEYS, seq), F32),
        ],
        compiler_params=_cparams(("parallel",)),
        name="mid",
    )(oa, ob, x, w_out, g, w_query, sub_keys)


N_TOP = PEER_TOPK + 1
_CAND = [(a, b) for a in range(N_TOP) for b in range(N_TOP) if (a + 1) * (b + 1) <= N_TOP]


def _top_vals(s, n):
    vals = []
    for r in range(n):
        m = jnp.max(s, axis=0, keepdims=True)
        vals.append(m)
        if r + 1 < n:
            s = jnp.where(s >= m, -jnp.inf, s)
    return vals


def _topk_kernel(sc_ref, e1_ref, e2_ref, tau_ref):
    t = sc_ref.shape[-1]
    taus = []
    for h in range(PEER_HEADS):
        s1 = sc_ref[2 * h]
        s2 = sc_ref[2 * h + 1]
        v1 = _top_vals(s1, N_TOP)
        v2 = _top_vals(s2, N_TOP)
        cands = [v1[a] + v2[b] for (a, b) in _CAND]
        pad = (-len(cands)) % 8
        cands += [jnp.full((1, t), -jnp.inf, F32)] * pad
        c = jnp.concatenate(cands, axis=0)
        top = _top_vals(c, N_TOP)
        t16, t17 = top[PEER_TOPK - 1], top[PEER_TOPK]
        best = v1[0] + v2[0]
        z = jnp.sum(jnp.where(c >= t16, jnp.exp(c - best), 0.0), axis=0, keepdims=True)
        rz = 1.0 / z
        e1_ref[h] = jnp.exp(s1 - v1[0]) * rz
        e2_ref[h] = jnp.exp(s2 - v2[0])
        taus.append(0.5 * (jnp.exp(t16 - best) + jnp.exp(t17 - best)) * rz)
    tau_ref[...] = jnp.concatenate(taus, axis=0)


def _topk_call(sc, tt):
    seq = sc.shape[-1]
    blk = lambda i: (0, 0, i)
    return pl.pallas_call(
        _topk_kernel,
        grid=(seq // tt,),
        in_specs=[pl.BlockSpec((2 * PEER_HEADS, PEER_KEYS, tt), blk)],
        out_specs=[
            pl.BlockSpec((PEER_HEADS, PEER_KEYS, tt), blk),
            pl.BlockSpec((PEER_HEADS, PEER_KEYS, tt), blk),
            pl.BlockSpec((PEER_HEADS, tt), lambda i: (0, i)),
        ],
        out_shape=[
            jax.ShapeDtypeStruct((PEER_HEADS, PEER_KEYS, seq), F32),
            jax.ShapeDtypeStruct((PEER_HEADS, PEER_KEYS, seq), F32),
            jax.ShapeDtypeStruct((PEER_HEADS, seq), F32),
        ],
        compiler_params=_cparams(("parallel",)),
        name="topk",
    )(sc)


def _peer_kernel(xn_ref, u_ref, vt_ref, e1_ref, e2_ref, tau_ref, h_ref, g_ref, o_ref,
                 acc_ref, a_ref, w_ref, *, eb):
    e = pl.program_id(1)
    tt = xn_ref.shape[0]
    inv_sqrt2 = 1.0 / math.sqrt(2.0)
    a_ref[...] = lax.dot_general(u_ref[...], xn_ref[...], NT_DIMS,
                                 preferred_element_type=F32)
    for ii in range(eb // PEER_KEYS):
        i = e * (eb // PEER_KEYS) + ii
        rows = slice(PEER_KEYS * ii, PEER_KEYS * (ii + 1))
        e1_rows = [e1_ref[h, pl.ds(i, 1), :] for h in range(PEER_HEADS)]
        for tc in range(tt // LANES):
            cols = slice(LANES * tc, LANES * (tc + 1))
            a = a_ref[rows, cols]
            act = 0.5 * a * (1.0 + lax.erf(a * inv_sqrt2))
            gate = None
            for h in range(PEER_HEADS):
                p = e2_ref[h, :, cols] * e1_rows[h][:, cols]
                term = jnp.where(p >= tau_ref[h:h + 1, cols], p, 0.0)
                gate = term if gate is None else gate + term
            w_ref[rows, cols] = (gate * act).astype(BF16)
    part = jnp.dot(vt_ref[...], w_ref[...], preferred_element_type=F32)

    @pl.when(e == 0)
    def _():
        acc_ref[...] = part

    @pl.when(e > 0)
    def _():
        acc_ref[...] += part

    @pl.when(e == pl.num_programs(1) - 1)
    def _():
        out = h_ref[...] + acc_ref[...].T
        o_ref[...] = _rms(out, g_ref[...], NORM_EPS)


def _peer_call(xn, u, vt, e1, e2, tau, h, g, tt, eb):
    seq = xn.shape[0]
    n_exp = u.shape[0]
    tok = lambda t, e: (t, 0)
    tok3 = lambda t, e: (0, 0, t)
    return pl.pallas_call(
        functools.partial(_peer_kernel, eb=eb),
        grid=(seq // tt, n_exp // eb),
        in_specs=[
            pl.BlockSpec((tt, D_MODEL), tok),
            pl.BlockSpec((eb, D_MODEL), lambda t, e: (e, 0)),
            pl.BlockSpec((D_MODEL, eb), lambda t, e: (0, e)),
            pl.BlockSpec((PEER_HEADS, PEER_KEYS, tt), tok3),
            pl.BlockSpec((PEER_HEADS, PEER_KEYS, tt), tok3),
            pl.BlockSpec((PEER_HEADS, tt), lambda t, e: (0, t)),
            pl.BlockSpec((tt, D_MODEL), tok),
            pl.BlockSpec((1, D_MODEL), lambda t, e: (0, 0)),
        ],
        out_specs=pl.BlockSpec((tt, D_MODEL), tok),
        out_shape=jax.ShapeDtypeStruct((seq, D_MODEL), F32),
        scratch_shapes=[pltpu.VMEM((D_MODEL, tt), F32), pltpu.VMEM((eb, tt), F32), pltpu.VMEM((eb, tt), BF16)],
        compiler_params=_cparams(("parallel", "arbitrary")),
        name="peer",
    )(xn, u, vt, e1, e2, tau, h, g)


def _tiles(seq):
    big = seq >= 4096
    return dict(
        tm=512 if big else 256,
        tq_a=128, tq_b=256,
        tk=1024 if big else 512,
        tt_topk=256, tt_peer=512 if big else 256, eb=1024,
    )


def kernel(x, norm_attn_g, w_in, q_norm_g, k_norm_g, lambda_q1, lambda_k1, lambda_q2, lambda_k2,
           subln_g, w_out, norm_ffn_g, w_query, sub_keys, expert_u, expert_v, norm_final_g):
    batch, seq, d = x.shape
    assert batch == 1 and d == D_MODEL and norm_attn_g.shape[0] == 1
    t = _tiles(seq)
    lambda_init = 0.8 - 0.6 * math.exp(-0.3 * 0)
    x2 = x.reshape(seq, d)

    tab = _rope_tables(seq)
    blk = jnp.arange(A_Q_W) // HEAD_DIM
    bd = jnp.where(blk[:, None] == blk[None, :], 1.0 / HEAD_DIM, 0.0).astype(BF16)
    qg = (jnp.tile(q_norm_g[0], A_Q_HEADS) * Q_SCALE).reshape(1, A_Q_W)
    kg = jnp.tile(k_norm_g[0], A_KV_HEADS).reshape(1, A_KV_W)

    qa, ka, vat, qb, kb, vbt = _proj_call(x2, norm_attn_g, w_in[0].astype(BF16), qg, kg, tab, bd, t["tm"])
    oa = _gqa_call(qa, ka, vat, t["tq_a"], t["tk"])
    lam_vecs = jnp.concatenate([lambda_q1, lambda_k1, lambda_q2, lambda_k2], axis=0)
    ob = _diff_call(lam_vecs, qb, kb, vbt, subln_g, t["tq_b"], t["tk"], lambda_init)

    sk = sub_keys[0].reshape(2 * PEER_HEADS, PEER_KEYS, PEER_HALF).astype(BF16)
    h, xn, sc = _mid_call(oa, ob, x2, w_out[0].astype(BF16), norm_ffn_g, w_query[0].astype(BF16), sk, t["tm"])
    e1, e2, tau = _topk_call(sc, t["tt_topk"])
    out = _peer_call(xn, expert_u[0].astype(BF16), expert_v[0].T.astype(BF16), e1, e2, tau, h,
                     norm_final_g.reshape(1, d), t["tt_peer"], t["eb"])
    return out.reshape(batch, seq, d)
```

```python
import functools
import math

import jax
import jax.numpy as jnp
import numpy as np
from jax import lax
from jax.experimental import pallas as pl
from jax.experimental.pallas import tpu as pltpu

F32 = jnp.float32
BF16 = jnp.bfloat16

D_MODEL = 1024
HEAD_DIM = 64
A_Q_HEADS = 8
A_KV_HEADS = 2
A_GROUP = A_Q_HEADS // A_KV_HEADS
B_HEADS = 4
B_V_DIM = 2 * HEAD_DIM
GRID_W = 64
AXIAL_THETA = 10000.0
AXIAL_HALF = HEAD_DIM // 2
ROPE_THETA = 500000.0
ROPE_DIMS = HEAD_DIM // 4
NORM_EPS = 1e-6
SUBLN_EPS = 1e-5
A_Q_W = A_Q_HEADS * HEAD_DIM
A_KV_W = A_KV_HEADS * HEAD_DIM
B_QK_W = 2 * B_HEADS * HEAD_DIM
B_V_W = B_HEADS * B_V_DIM
IN_COLS = A_Q_W + 2 * A_KV_W + 2 * B_QK_W + B_V_W
PEER_HEADS = 8
PEER_KEYS = 128
PEER_HALF = 128
PEER_TOPK = 16
LANES = 128

NT_DIMS = (((1,), (1,)), ((), ()))
Q_SCALE = (HEAD_DIM ** -0.5) * math.log2(math.e)

VMEM_LIMIT = 56 * 1024 * 1024


def _cparams(sem, flags=None):
    return pltpu.CompilerParams(dimension_semantics=sem, vmem_limit_bytes=VMEM_LIMIT, flags=flags)


def _rms(x, g, eps):
    return x * lax.rsqrt(jnp.mean(x * x, axis=-1, keepdims=True) + eps) * g


def _group_mean_sq(v, bd):
    v2 = v * v
    hi = v2.astype(BF16)
    lo = (v2 - hi.astype(F32)).astype(BF16)
    return (jnp.dot(hi, bd, preferred_element_type=F32)
            + jnp.dot(lo, bd, preferred_element_type=F32))


def _rot_half(v, half, group):
    width = v.shape[-1]
    lane = lax.broadcasted_iota(jnp.int32, v.shape, 1)
    fwd = pltpu.roll(v, width - half, 1)
    bwd = pltpu.roll(v, half, 1)
    return jnp.where((lane % group) < half, fwd, bwd)


def _tile4(t):
    return jnp.concatenate([t, t, t, t], axis=1)


def _proj_kernel(x_ref, g_ref, w_ref, qg_ref, kg_ref, tab_ref, bd_ref,
                 qa_ref, ka_ref, va_ref, qb_ref, kb_ref, vb_ref):
    xn = _rms(x_ref[...], g_ref[...], NORM_EPS)
    proj = jnp.dot(xn.astype(BF16), w_ref[...], preferred_element_type=F32)
    tab = tab_ref[...]
    cos_a, sin_a = tab[:, 0:128], tab[:, 128:256]
    cos_b, sin_b = tab[:, 256:384], tab[:, 384:512]
    bd = bd_ref[...]

    c0 = 0
    qa = proj[:, c0:c0 + A_Q_W]
    qa = qa * lax.rsqrt(_group_mean_sq(qa, bd) + NORM_EPS) * qg_ref[...]
    qa = qa * _tile4(cos_a) + _rot_half(qa, AXIAL_HALF // 2, AXIAL_HALF) * _tile4(sin_a)
    qa_ref[...] = qa.astype(BF16)
    c0 += A_Q_W

    ka = proj[:, c0:c0 + A_KV_W]
    ka = ka * lax.rsqrt(_group_mean_sq(ka, bd[:A_KV_W, :A_KV_W]) + NORM_EPS) * kg_ref[...]
    ka = ka * cos_a + _rot_half(ka, AXIAL_HALF // 2, AXIAL_HALF) * sin_a
    ka_ref[...] = ka.astype(BF16)
    c0 += A_KV_W

    va_ref[0] = proj[:, c0:c0 + A_KV_W].T.astype(BF16)
    c0 += A_KV_W

    qb = proj[:, c0:c0 + B_QK_W]
    qb = qb * _tile4(cos_b) + _rot_half(qb, ROPE_DIMS // 2, HEAD_DIM) * _tile4(sin_b)
    qb_ref[...] = (qb * Q_SCALE).astype(BF16)
    c0 += B_QK_W

    kb = proj[:, c0:c0 + B_QK_W]
    kb = kb * _tile4(cos_b) + _rot_half(kb, ROPE_DIMS // 2, HEAD_DIM) * _tile4(sin_b)
    kb_ref[...] = kb.astype(BF16)
    c0 += B_QK_W

    vb_ref[0] = proj[:, c0:c0 + B_V_W].T.astype(BF16)


def _rope_tables(seq):
    f32 = np.float32
    rows = seq // GRID_W
    row = np.repeat(np.arange(rows, dtype=f32), GRID_W)
    col = np.tile(np.arange(GRID_W, dtype=f32), rows)
    pos = np.arange(seq, dtype=f32)
    inv_ax = (f32(AXIAL_THETA) ** (-np.arange(0, AXIAL_HALF, 2, dtype=f32) / f32(AXIAL_HALF))).astype(f32)
    inv_p = (f32(ROPE_THETA) ** (-np.arange(0, ROPE_DIMS, 2, dtype=f32) / f32(ROPE_DIMS))).astype(f32)
    row_ang = row[:, None] * inv_ax[None, :]
    col_ang = col[:, None] * inv_ax[None, :]
    pos_ang = pos[:, None] * inv_p[None, :]
    cr, sr = np.cos(row_ang), np.sin(row_ang)
    cc, sc = np.cos(col_ang), np.sin(col_ang)
    cp, sp = np.cos(pos_ang), np.sin(pos_ang)
    rest = HEAD_DIM - ROPE_DIMS
    cos_a = np.concatenate([cr, cr, cc, cc], axis=1)
    sin_a = np.concatenate([-sr, sr, -sc, sc], axis=1)
    cos_b = np.concatenate([cp, cp, np.ones((seq, rest), f32)], axis=1)
    sin_b = np.concatenate([-sp, sp, np.zeros((seq, rest), f32)], axis=1)
    two = lambda t: np.concatenate([t, t], axis=1)
    tab = np.concatenate([two(cos_a), two(sin_a), two(cos_b), two(sin_b)], axis=1).astype(f32)
    return jnp.asarray(tab)


def _proj_call(x, g, w_in, qg, kg, tab, bd, tm):
    seq = x.shape[0]
    row = lambda i: (i, 0)
    fix = lambda i: (0, 0)
    n = seq // tm
    rows_out = lambda w: (pl.BlockSpec((tm, w), row), jax.ShapeDtypeStruct((seq, w), BF16))
    cols_out = lambda w: (pl.BlockSpec((1, w, tm), lambda i: (i, 0, 0)), jax.ShapeDtypeStruct((n, w, tm), BF16))
    outs = [rows_out(A_Q_W), rows_out(A_KV_W), cols_out(A_KV_W), rows_out(B_QK_W), rows_out(B_QK_W), cols_out(B_V_W)]
    return pl.pallas_call(
        _proj_kernel,
        grid=(seq // tm,),
        in_specs=[
            pl.BlockSpec((tm, D_MODEL), row),
            pl.BlockSpec((1, D_MODEL), fix),
            pl.BlockSpec((D_MODEL, IN_COLS), fix),
            pl.BlockSpec((1, A_Q_W), fix),
            pl.BlockSpec((1, A_KV_W), fix),
            pl.BlockSpec((tm, 512), row),
            pl.BlockSpec((A_Q_W, A_Q_W), fix),
        ],
        out_specs=[o[0] for o in outs],
        out_shape=[o[1] for o in outs],
        compiler_params=_cparams(("parallel",)),
        name="proj",
    )(x, g, w_in, qg, kg, tab, bd)


SOFTMAX_ROWS = 32


N_SLOTS = 3
STEP_PARTS = 1
STEPS_PER_TRIP = 3
ST_M, ST_ALPHA, ST_NEXT_MAX, ST_L = 0, 1, 2, 8


def _flash_scratch(nq, tk, dv):
    return ([pltpu.VMEM((tk, nq), F32)] * N_SLOTS + [pltpu.VMEM((tk, nq), BF16)] * N_SLOTS
            + [pltpu.VMEM((dv, nq), F32), pltpu.VMEM((16, nq), F32)])


def _flash_cols(qst, k_ref, vt_ref, scratch):
    s_ref, p_ref = scratch[:N_SLOTS], scratch[N_SLOTS:2 * N_SLOTS]
    acc_ref, st_ref = scratch[2 * N_SLOTS:]
    return _flash_cols_impl(qst, k_ref, vt_ref, s_ref, p_ref, acc_ref, st_ref)


def _flash_cols_impl(qst, k_ref, vt_ref, s_ref, p_ref, acc_ref, st_ref):
    nq = qst.shape[1]
    tk = s_ref[0].shape[0]
    per = tk // vt_ref.shape[2]
    n_chunks = vt_ref.shape[0] // per
    assert n_chunks >= N_SLOTS and tk % SOFTMAX_ROWS == 0
    n_sub = tk // SOFTMAX_ROWS
    row = lambda r, n=1: slice(r, r + n)
    sub = lambda slot, b: s_ref[slot][b * SOFTMAX_ROWS:(b + 1) * SOFTMAX_ROWS, :]

    def scores(j, slot):
        k = k_ref[pl.ds(j * tk, tk), :]
        s_ref[slot][...] = jnp.dot(k, qst, preferred_element_type=F32)

    def column_max(slot):
        mx = sub(slot, 0)
        for b in range(1, n_sub):
            mx = jnp.maximum(mx, sub(slot, b))
        st_ref[row(ST_NEXT_MAX), :] = jnp.max(mx, axis=0, keepdims=True)

    def weighted(j, slot, alpha):
        vt = jnp.concatenate([vt_ref[per * j + c] for c in range(per)], axis=1)
        acc_ref[...] = alpha * acc_ref[...] + jnp.dot(vt, p_ref[slot][...], preferred_element_type=F32)

    def step(j, slot, with_scores=True, with_max=True):
        nxt1, nxt2 = (slot + 1) % N_SLOTS, (slot + 2) % N_SLOTS
        a_prev = st_ref[row(ST_ALPHA), :]
        m_old = st_ref[row(ST_M), :]
        m_new = jnp.maximum(m_old, st_ref[row(ST_NEXT_MAX), :])
        alpha = jnp.exp2(m_old - m_new)
        j_prev = jnp.maximum(j - 1, 0)
        rows_q = tk // STEP_PARTS
        subs_q = n_sub // STEP_PARTS
        slab = vt_ref.shape[2]
        psum = jnp.zeros((8, nq), F32)
        mx = None
        part = None
        for q in range(STEP_PARTS):
            r0 = q * rows_q
            if with_scores:
                start = (j + 2) * tk + r0
                start = start if isinstance(start, int) else pl.multiple_of(start, rows_q)
                k = k_ref[pl.ds(start, rows_q), :]
                s_ref[nxt2][r0:r0 + rows_q, :] = jnp.dot(k, qst, preferred_element_type=F32)
            for b in range(q * subs_q, (q + 1) * subs_q):
                p = jnp.exp2(sub(slot, b) - m_new)
                psum = psum + jnp.sum(p.reshape(SOFTMAX_ROWS // 8, 8, nq), axis=0)
                p_ref[slot][b * SOFTMAX_ROWS:(b + 1) * SOFTMAX_ROWS, :] = p.astype(BF16)
            pieces = [vt_ref[per * j_prev + c // slab][:, c % slab:c % slab + min(slab, rows_q)]
                      for c in range(r0, r0 + rows_q, min(slab, rows_q))]
            vt = pieces[0] if len(pieces) == 1 else jnp.concatenate(pieces, axis=1)
            d = jnp.dot(vt, p_ref[nxt2][r0:r0 + rows_q, :], preferred_element_type=F32)
            part = d if part is None else part + d
            if with_max:
                for b in range(q * subs_q, (q + 1) * subs_q):
                    mx = sub(nxt1, b) if mx is None else jnp.maximum(mx, sub(nxt1, b))
        acc_ref[...] = a_prev * acc_ref[...] + part
        st_ref[row(ST_L, 8), :] = alpha * st_ref[row(ST_L, 8), :] + psum
        st_ref[row(ST_M), :] = m_new
        st_ref[row(ST_ALPHA), :] = alpha
        if with_max:
            st_ref[row(ST_NEXT_MAX), :] = jnp.max(mx, axis=0, keepdims=True)

    scores(0, 0)
    scores(1, 1)
    st_ref[...] = jnp.zeros_like(st_ref)
    st_ref[row(ST_M), :] = jnp.full((1, nq), -jnp.inf, F32)
    acc_ref[...] = jnp.zeros_like(acc_ref)
    p_ref[N_SLOTS - 1][...] = jnp.zeros((tk, nq), BF16)
    column_max(0)
    n_full = n_chunks - 2
    n_trips = n_full // STEPS_PER_TRIP
    lead = n_full - n_trips * STEPS_PER_TRIP
    for j in range(lead):
        step(j, j % N_SLOTS)

    def body(t, carry):
        for u in range(STEPS_PER_TRIP):
            step(lead + t * STEPS_PER_TRIP + u, (lead + u) % N_SLOTS)
        return carry

    lax.fori_loop(0, n_trips, body, 0)
    step(n_chunks - 2, (n_chunks - 2) % N_SLOTS, with_scores=False)
    step(n_chunks - 1, (n_chunks - 1) % N_SLOTS, with_scores=False, with_max=False)
    weighted(n_chunks - 1, (n_chunks - 1) % N_SLOTS, st_ref[row(ST_ALPHA), :])
    return acc_ref[...], jnp.sum(st_ref[row(ST_L, 8), :], axis=0, keepdims=True)


def _gqa_kernel(q_ref, k_ref, vt_ref, o_ref, *scratch, tq):
    g = pl.program_id(0)
    lane = lax.broadcasted_iota(jnp.int32, (tq, LANES), 1)
    in_g = (lane // HEAD_DIM) == g
    rows = []
    for hh in range(A_GROUP):
        qp = q_ref[:, LANES * (hh // 2):LANES * (hh // 2 + 1)].astype(F32)
        aligned = jnp.where((hh % 2) == g, qp, pltpu.roll(qp, HEAD_DIM, 1))
        rows.append(jnp.where(in_g, aligned, 0.0))
    qs = jnp.concatenate(rows, axis=0)
    acc_t, l = _flash_cols(qs.T.astype(BF16), k_ref, vt_ref, scratch)
    o_t = acc_t / l
    o = jnp.concatenate([o_t, jnp.zeros_like(o_t)], axis=0).T
    left_half = lane < HEAD_DIM
    for p in range(A_GROUP // 2):
        a = o[(2 * p) * tq:(2 * p + 1) * tq]
        b = o[(2 * p + 1) * tq:(2 * p + 2) * tq]
        o_ref[:, LANES * p:LANES * (p + 1)] = jnp.where(left_half, a, pltpu.roll(b, HEAD_DIM, 1)).astype(BF16)


def _gqa_call(qa, ka, vat, tq, tk):
    seq = qa.shape[0]
    return pl.pallas_call(
        functools.partial(_gqa_kernel, tq=tq),
        grid=(A_KV_HEADS, seq // tq),
        in_specs=[
            pl.BlockSpec((tq, A_GROUP * HEAD_DIM), lambda g, i: (i, g)),
            pl.BlockSpec((seq, A_KV_W), lambda g, i: (0, 0)),
            pl.BlockSpec((vat.shape[0], HEAD_DIM, vat.shape[2]), lambda g, i: (0, g, 0)),
        ],
        out_specs=pl.BlockSpec((tq, A_GROUP * HEAD_DIM), lambda g, i: (i, g)),
        out_shape=jax.ShapeDtypeStruct((seq, A_Q_W), BF16),
        scratch_shapes=_flash_scratch(A_GROUP * tq, tk, HEAD_DIM),
        compiler_params=_cparams(("parallel", "parallel")),
        name="gqa",
    )(qa, ka, vat)


def _diff_kernel(lam_ref, q_ref, k_ref, vt_ref, sg_ref, o_ref, *scratch, tq, lambda_init):
    lane = lax.broadcasted_iota(jnp.int32, (tq, LANES), 1)
    q = q_ref[...].astype(F32)
    qs = jnp.concatenate([jnp.where(lane < HEAD_DIM, q, 0.0),
                          jnp.where(lane >= HEAD_DIM, q, 0.0)], axis=0)
    acc_t, l = _flash_cols(qs.T.astype(BF16), k_ref, vt_ref, scratch)
    o = (acc_t / l).T
    lv = lam_ref[...]
    lam = (jnp.exp(jnp.sum(lv[0:1] * lv[1:2], axis=-1, keepdims=True))
           - jnp.exp(jnp.sum(lv[2:3] * lv[3:4], axis=-1, keepdims=True)) + lambda_init)
    ob = o[:tq] - lam * o[tq:]
    ob = _rms(ob, sg_ref[...], SUBLN_EPS) * (1.0 - lambda_init)
    o_ref[...] = ob.astype(BF16)


def _diff_call(lam_vecs, qb, kb, vbt, subln_g, tq, tk, lambda_init):
    seq = qb.shape[0]
    n_slabs, _, slab = vbt.shape
    return pl.pallas_call(
        functools.partial(_diff_kernel, tq=tq, lambda_init=lambda_init),
        grid=(B_HEADS, seq // tq),
        in_specs=[
            pl.BlockSpec((4, HEAD_DIM), lambda h, i: (0, 0)),
            pl.BlockSpec((tq, LANES), lambda h, i: (i, h)),
            pl.BlockSpec((seq, LANES), lambda h, i: (0, h)),
            pl.BlockSpec((n_slabs, B_V_DIM, slab), lambda h, i: (0, h, 0)),
            pl.BlockSpec((1, B_V_DIM), lambda h, i: (0, 0)),
        ],
        out_specs=pl.BlockSpec((tq, LANES), lambda h, i: (i, h)),
        out_shape=jax.ShapeDtypeStruct((seq, B_V_W), BF16),
        scratch_shapes=_flash_scratch(2 * tq, tk, B_V_DIM),
        compiler_params=_cparams(("parallel", "parallel")),
        name="diff",
    )(lam_vecs, qb, kb, vbt, subln_g)


def _mid_kernel(oa_ref, ob_ref, x_ref, wo_ref, g_ref, wq_ref, sk_ref, h_ref, xn_ref, sc_ref):
    o = jnp.concatenate([oa_ref[...], ob_ref[...]], axis=1)
    h = x_ref[...] + jnp.dot(o, wo_ref[...], preferred_element_type=F32)
    h_ref[...] = h
    xn = _rms(h, g_ref[...], NORM_EPS).astype(BF16)
    xn_ref[...] = xn
    q = jnp.dot(xn, wq_ref[...], preferred_element_type=F32).astype(BF16)
    for hp in range(2 * PEER_HEADS):
        sc_ref[hp] = lax.dot_general(sk_ref[hp], q[:, PEER_HALF * hp:PEER_HALF * (hp + 1)],
                                     NT_DIMS, preferred_element_type=F32)


def _mid_call(oa, ob, x, w_out, g, w_query, sub_keys, tm):
    seq = x.shape[0]
    row = lambda i: (i, 0)
    fix = lambda i: (0, 0)
    nq = 2 * PEER_HEADS * PEER_HALF
    return pl.pallas_call(
        _mid_kernel,
        grid=(seq // tm,),
        in_specs=[
            pl.BlockSpec((tm, A_Q_W), row),
            pl.BlockSpec((tm, B_V_W), row),
            pl.BlockSpec((tm, D_MODEL), row),
            pl.BlockSpec((D_MODEL, D_MODEL), fix),
            pl.BlockSpec((1, D_MODEL), fix),
            pl.BlockSpec((D_MODEL, nq), fix),
            pl.BlockSpec((2 * PEER_HEADS, PEER_KEYS, PEER_HALF), lambda i: (0, 0, 0)),
        ],
        out_specs=[
            pl.BlockSpec((tm, D_MODEL), row),
            pl.BlockSpec((tm, D_MODEL), row),
            pl.BlockSpec((2 * PEER_HEADS, PEER_KEYS, tm), lambda i: (0, 0, i)),
        ],
        out_shape=[
            jax.ShapeDtypeStruct((seq, D_MODEL), F32),
            jax.ShapeDtypeStruct((seq, D_MODEL), BF16),
            jax.ShapeDtypeStruct((2 * PEER_HEADS, PEER_KEYS, seq), F32),
        ],
        compiler_params=_cparams(("parallel",)),
        name="mid",
    )(oa, ob, x, w_out, g, w_query, sub_keys)


N_TOP = PEER_TOPK + 1
_CAND = [(a, b) for a in range(N_TOP) for b in range(N_TOP) if (a + 1) * (b + 1) <= N_TOP]


def _top_vals(s, n):
    vals = []
    for r in range(n):
        m = jnp.max(s, axis=0, keepdims=True)
        vals.append(m)
        if r + 1 < n:
            s = jnp.where(s >= m, -jnp.inf, s)
    return vals


def _topk_kernel(sc_ref, e1_ref, e2_ref, tau_ref):
    t = sc_ref.shape[-1]
    taus = []
    for h in range(PEER_HEADS):
        s1 = sc_ref[2 * h]
        s2 = sc_ref[2 * h + 1]
        v1 = _top_vals(s1, N_TOP)
        v2 = _top_vals(s2, N_TOP)
        cands = [v1[a] + v2[b] for (a, b) in _CAND]
        pad = (-len(cands)) % 8
        cands += [jnp.full((1, t), -jnp.inf, F32)] * pad
        c = jnp.concatenate(cands, axis=0)
        top = _top_vals(c, N_TOP)
        t16, t17 = top[PEER_TOPK - 1], top[PEER_TOPK]
        best = v1[0] + v2[0]
        z = jnp.sum(jnp.where(c >= t16, jnp.exp(c - best), 0.0), axis=0, keepdims=True)
        rz = 1.0 / z
        e1_ref[h] = jnp.exp(s1 - v1[0]) * rz
        e2_ref[h] = jnp.exp(s2 - v2[0])
        taus.append(0.5 * (jnp.exp(t16 - best) + jnp.exp(t17 - best)) * rz)
    tau_ref[...] = jnp.concatenate(taus, axis=0)


def _topk_call(sc, tt):
    seq = sc.shape[-1]
    blk = lambda i: (0, 0, i)
    return pl.pallas_call(
        _topk_kernel,
        grid=(seq // tt,),
        in_specs=[pl.BlockSpec((2 * PEER_HEADS, PEER_KEYS, tt), blk)],
        out_specs=[
            pl.BlockSpec((PEER_HEADS, PEER_KEYS, tt), blk),
            pl.BlockSpec((PEER_HEADS, PEER_KEYS, tt), blk),
            pl.BlockSpec((PEER_HEADS, tt), lambda i: (0, i)),
        ],
        out_shape=[
            jax.ShapeDtypeStruct((PEER_HEADS, PEER_KEYS, seq), F32),
            jax.ShapeDtypeStruct((PEER_HEADS, PEER_KEYS, seq), F32),
            jax.ShapeDtypeStruct((PEER_HEADS, seq), F32),
        ],
        compiler_params=_cparams(("parallel",)),
        name="topk",
    )(sc)


def _peer_kernel(xn_ref, u_ref, vt_ref, e1_ref, e2_ref, tau_ref, h_ref, g_ref, o_ref,
                 acc_ref, a_ref, w_ref, *, eb):
    e = pl.program_id(1)
    tt = xn_ref.shape[0]
    inv_sqrt2 = 1.0 / math.sqrt(2.0)
    a_ref[...] = lax.dot_general(u_ref[...], xn_ref[...], NT_DIMS,
                                 preferred_element_type=F32)
    for ii in range(eb // PEER_KEYS):
        i = e * (eb // PEER_KEYS) + ii
        rows = slice(PEER_KEYS * ii, PEER_KEYS * (ii + 1))
        e1_rows = [e1_ref[h, pl.ds(i, 1), :] for h in range(PEER_HEADS)]
        for tc in range(tt // LANES):
            cols = slice(LANES * tc, LANES * (tc + 1))
            a = a_ref[rows, cols]
            act = 0.5 * a * (1.0 + lax.erf(a * inv_sqrt2))
            gate = None
            for h in range(PEER_HEADS):
                p = e2_ref[h, :, cols] * e1_rows[h][:, cols]
                term = jnp.where(p >= tau_ref[h:h + 1, cols], p, 0.0)
                gate = term if gate is None else gate + term
            w_ref[rows, cols] = (gate * act).astype(BF16)
    part = jnp.dot(vt_ref[...], w_ref[...], preferred_element_type=F32)

    @pl.when(e == 0)
    def _():
        acc_ref[...] = part

    @pl.when(e > 0)
    def _():
        acc_ref[...] += part

    @pl.when(e == pl.num_programs(1) - 1)
    def _():
        out = h_ref[...] + acc_ref[...].T
        o_ref[...] = _rms(out, g_ref[...], NORM_EPS)


def _peer_call(xn, u, vt, e1, e2, tau, h, g, tt, eb):
    seq = xn.shape[0]
    n_exp = u.shape[0]
    tok = lambda t, e: (t, 0)
    tok3 = lambda t, e: (0, 0, t)
    return pl.pallas_call(
        functools.partial(_peer_kernel, eb=eb),
        grid=(seq // tt, n_exp // eb),
        in_specs=[
            pl.BlockSpec((tt, D_MODEL), tok),
            pl.BlockSpec((eb, D_MODEL), lambda t, e: (e, 0)),
            pl.BlockSpec((D_MODEL, eb), lambda t, e: (0, e)),
            pl.BlockSpec((PEER_HEADS, PEER_KEYS, tt), tok3),
            pl.BlockSpec((PEER_HEADS, PEER_KEYS, tt), tok3),
            pl.BlockSpec((PEER_HEADS, tt), lambda t, e: (0, t)),
            pl.BlockSpec((tt, D_MODEL), tok),
            pl.BlockSpec((1, D_MODEL), lambda t, e: (0, 0)),
        ],
        out_specs=pl.BlockSpec((tt, D_MODEL), tok),
        out_shape=jax.ShapeDtypeStruct((seq, D_MODEL), F32),
        scratch_shapes=[pltpu.VMEM((D_MODEL, tt), F32), pltpu.VMEM((eb, tt), F32), pltpu.VMEM((eb, tt), BF16)],
        compiler_params=_cparams(("parallel", "arbitrary")),
        name="peer",
    )(xn, u, vt, e1, e2, tau, h, g)


def _tiles(seq):
    big = seq >= 4096
    return dict(
        tm=512 if big else 256,
        tq_a=128, tq_b=256,
        tk=1024 if big else 512,
        tt_topk=256, tt_peer=512 if big else 256, eb=1024,
    )


def kernel(x, norm_attn_g, w_in, q_norm_g, k_norm_g, lambda_q1, lambda_k1, lambda_q2, lambda_k2,
           subln_g, w_out, norm_ffn_g, w_query, sub_keys, expert_u, expert_v, norm_final_g):
    batch, seq, d = x.shape
    assert batch == 1 and d == D_MODEL and norm_attn_g.shape[0] == 1
    t = _tiles(seq)
    lambda_init = 0.8 - 0.6 * math.exp(-0.3 * 0)
    x2 = x.reshape(seq, d)

    tab = _rope_tables(seq)
    blk = np.arange(A_Q_W) // HEAD_DIM
    bd = jnp.asarray(np.where(blk[:, None] == blk[None, :], 1.0 / HEAD_DIM, 0.0), dtype=BF16)
    qg = (jnp.tile(q_norm_g[0], A_Q_HEADS) * Q_SCALE).reshape(1, A_Q_W)
    kg = jnp.tile(k_norm_g[0], A_KV_HEADS).reshape(1, A_KV_W)

    qa, ka, vat, qb, kb, vbt = _proj_call(x2, norm_attn_g, w_in[0].astype(BF16), qg, kg, tab, bd, t["tm"])
    oa = _gqa_call(qa, ka, vat, t["tq_a"], t["tk"])
    lam_vecs = jnp.concatenate([lambda_q1, lambda_k1, lambda_q2, lambda_k2], axis=0)
    ob = _diff_call(lam_vecs, qb, kb, vbt, subln_g, t["tq_b"], t["tk"], lambda_init)

    sk = sub_keys[0].reshape(2 * PEER_HEADS, PEER_KEYS, PEER_HALF).astype(BF16)
    h, xn, sc = _mid_call(oa, ob, x2, w_out[0].astype(BF16), norm_ffn_g, w_query[0].astype(BF16), sk, t["tm"])
    e1, e2, tau = _topk_call(sc, t["tt_topk"])
    out = _peer_call(xn, expert_u[0].astype(BF16), expert_v[0].T.astype(BF16), e1, e2, tau, h,
                     norm_final_g.reshape(1, d), t["tt_peer"], t["eb"])
    return out.reshape(batch, seq, d)
```

```python
import functools
import math

import jax
import jax.numpy as jnp
import numpy as np
from jax import lax
from jax.experimental import pallas as pl
from jax.experimental.pallas import tpu as pltpu

F32 = jnp.float32
BF16 = jnp.bfloat16

D_MODEL = 1024
HEAD_DIM = 64
A_Q_HEADS = 8
A_KV_HEADS = 2
A_GROUP = A_Q_HEADS // A_KV_HEADS
B_HEADS = 4
B_V_DIM = 2 * HEAD_DIM
GRID_W = 64
AXIAL_THETA = 10000.0
AXIAL_HALF = HEAD_DIM // 2
ROPE_THETA = 500000.0
ROPE_DIMS = HEAD_DIM // 4
NORM_EPS = 1e-6
SUBLN_EPS = 1e-5
A_Q_W = A_Q_HEADS * HEAD_DIM
A_KV_W = A_KV_HEADS * HEAD_DIM
B_QK_W = 2 * B_HEADS * HEAD_DIM
B_V_W = B_HEADS * B_V_DIM
IN_COLS = A_Q_W + 2 * A_KV_W + 2 * B_QK_W + B_V_W
PEER_HEADS = 8
PEER_KEYS = 128
PEER_HALF = 128
PEER_TOPK = 16
LANES = 128

NT_DIMS = (((1,), (1,)), ((), ()))
Q_SCALE = (HEAD_DIM ** -0.5) * math.log2(math.e)

VMEM_LIMIT = 56 * 1024 * 1024


def _cparams(sem, flags=None):
    return pltpu.CompilerParams(dimension_semantics=sem, vmem_limit_bytes=VMEM_LIMIT, flags=flags)


def _rms(x, g, eps):
    return x * lax.rsqrt(jnp.mean(x * x, axis=-1, keepdims=True) + eps) * g


def _group_mean_sq(v, bd):
    v2 = v * v
    hi = v2.astype(BF16)
    lo = (v2 - hi.astype(F32)).astype(BF16)
    return (jnp.dot(hi, bd, preferred_element_type=F32)
            + jnp.dot(lo, bd, preferred_element_type=F32))


def _rot_half(v, half, group):
    width = v.shape[-1]
    lane = lax.broadcasted_iota(jnp.int32, v.shape, 1)
    fwd = pltpu.roll(v, width - half, 1)
    bwd = pltpu.roll(v, half, 1)
    return jnp.where((lane % group) < half, fwd, bwd)


def _tile4(t):
    return jnp.concatenate([t, t, t, t], axis=1)


def _proj_kernel(x_ref, g_ref, w_ref, qg_ref, kg_ref, tab_ref, bd_ref,
                 qa_ref, ka_ref, va_ref, qb_ref, kb_ref, vb_ref):
    xn = _rms(x_ref[...], g_ref[...], NORM_EPS)
    proj = jnp.dot(xn.astype(BF16), w_ref[...], preferred_element_type=F32)
    tab = tab_ref[...]
    cos_a, sin_a = tab[:, 0:128], tab[:, 128:256]
    cos_b, sin_b = tab[:, 256:384], tab[:, 384:512]
    bd = bd_ref[...]

    c0 = 0
    qa = proj[:, c0:c0 + A_Q_W]
    qa = qa * lax.rsqrt(_group_mean_sq(qa, bd) + NORM_EPS) * qg_ref[...]
    qa = qa * _tile4(cos_a) + _rot_half(qa, AXIAL_HALF // 2, AXIAL_HALF) * _tile4(sin_a)
    qa_ref[...] = qa.astype(BF16)
    c0 += A_Q_W

    ka = proj[:, c0:c0 + A_KV_W]
    ka = ka * lax.rsqrt(_group_mean_sq(ka, bd[:A_KV_W, :A_KV_W]) + NORM_EPS) * kg_ref[...]
    ka = ka * cos_a + _rot_half(ka, AXIAL_HALF // 2, AXIAL_HALF) * sin_a
    ka_ref[...] = ka.astype(BF16)
    c0 += A_KV_W

    va_ref[0] = proj[:, c0:c0 + A_KV_W].T.astype(BF16)
    c0 += A_KV_W

    qb = proj[:, c0:c0 + B_QK_W]
    qb = qb * _tile4(cos_b) + _rot_half(qb, ROPE_DIMS // 2, HEAD_DIM) * _tile4(sin_b)
    qb_ref[...] = (qb * Q_SCALE).astype(BF16)
    c0 += B_QK_W

    kb = proj[:, c0:c0 + B_QK_W]
    kb = kb * _tile4(cos_b) + _rot_half(kb, ROPE_DIMS // 2, HEAD_DIM) * _tile4(sin_b)
    kb_ref[...] = kb.astype(BF16)
    c0 += B_QK_W

    vb_ref[0] = proj[:, c0:c0 + B_V_W].T.astype(BF16)


def _rope_tables(seq):
    f32 = np.float32
    rows = seq // GRID_W
    row = np.repeat(np.arange(rows, dtype=f32), GRID_W)
    col = np.tile(np.arange(GRID_W, dtype=f32), rows)
    pos = np.arange(seq, dtype=f32)
    inv_ax = (f32(AXIAL_THETA) ** (-np.arange(0, AXIAL_HALF, 2, dtype=f32) / f32(AXIAL_HALF))).astype(f32)
    inv_p = (f32(ROPE_THETA) ** (-np.arange(0, ROPE_DIMS, 2, dtype=f32) / f32(ROPE_DIMS))).astype(f32)
    row_ang = row[:, None] * inv_ax[None, :]
    col_ang = col[:, None] * inv_ax[None, :]
    pos_ang = pos[:, None] * inv_p[None, :]
    cr, sr = np.cos(row_ang), np.sin(row_ang)
    cc, sc = np.cos(col_ang), np.sin(col_ang)
    cp, sp = np.cos(pos_ang), np.sin(pos_ang)
    rest = HEAD_DIM - ROPE_DIMS
    cos_a = np.concatenate([cr, cr, cc, cc], axis=1)
    sin_a = np.concatenate([-sr, sr, -sc, sc], axis=1)
    cos_b = np.concatenate([cp, cp, np.ones((seq, rest), f32)], axis=1)
    sin_b = np.concatenate([-sp, sp, np.zeros((seq, rest), f32)], axis=1)
    two = lambda t: np.concatenate([t, t], axis=1)
    tab = np.concatenate([two(cos_a), two(sin_a), two(cos_b), two(sin_b)], axis=1).astype(f32)
    return jnp.asarray(tab)


def _proj_call(x, g, w_in, qg, kg, tab, bd, tm):
    seq = x.shape[0]
    row = lambda i: (i, 0)
    fix = lambda i: (0, 0)
    n = seq // tm
    rows_out = lambda w: (pl.BlockSpec((tm, w), row), jax.ShapeDtypeStruct((seq, w), BF16))
    cols_out = lambda w: (pl.BlockSpec((1, w, tm), lambda i: (i, 0, 0)), jax.ShapeDtypeStruct((n, w, tm), BF16))
    outs = [rows_out(A_Q_W), rows_out(A_KV_W), cols_out(A_KV_W), rows_out(B_QK_W), rows_out(B_QK_W), cols_out(B_V_W)]
    return pl.pallas_call(
        _proj_kernel,
        grid=(seq // tm,),
        in_specs=[
            pl.BlockSpec((tm, D_MODEL), row),
            pl.BlockSpec((1, D_MODEL), fix),
            pl.BlockSpec((D_MODEL, IN_COLS), fix),
            pl.BlockSpec((1, A_Q_W), fix),
            pl.BlockSpec((1, A_KV_W), fix),
            pl.BlockSpec((tm, 512), row),
            pl.BlockSpec((A_Q_W, A_Q_W), fix),
        ],
        out_specs=[o[0] for o in outs],
        out_shape=[o[1] for o in outs],
        compiler_params=_cparams(("parallel",)),
        name="proj",
    )(x, g, w_in, qg, kg, tab, bd)


SOFTMAX_ROWS = 32


N_SLOTS = 3
STEP_PARTS = 1
STEPS_PER_TRIP = 3
ST_M, ST_ALPHA, ST_NEXT_MAX, ST_L = 0, 1, 2, 8


def _flash_scratch(nq, tk, dv):
    return ([pltpu.VMEM((tk, nq), F32)] * N_SLOTS + [pltpu.VMEM((tk, nq), BF16)] * N_SLOTS
            + [pltpu.VMEM((dv, nq), F32), pltpu.VMEM((16, nq), F32)])


def _flash_cols(qst, k_ref, vt_ref, scratch):
    s_ref, p_ref = scratch[:N_SLOTS], scratch[N_SLOTS:2 * N_SLOTS]
    acc_ref, st_ref = scratch[2 * N_SLOTS:]
    return _flash_cols_impl(qst, k_ref, vt_ref, s_ref, p_ref, acc_ref, st_ref)


def _flash_cols_impl(qst, k_ref, vt_ref, s_ref, p_ref, acc_ref, st_ref):
    nq = qst.shape[1]
    tk = s_ref[0].shape[0]
    per = tk // vt_ref.shape[2]
    n_chunks = vt_ref.shape[0] // per
    assert n_chunks >= N_SLOTS and tk % SOFTMAX_ROWS == 0
    n_sub = tk // SOFTMAX_ROWS
    row = lambda r, n=1: slice(r, r + n)
    sub = lambda slot, b: s_ref[slot][b * SOFTMAX_ROWS:(b + 1) * SOFTMAX_ROWS, :]

    def scores(j, slot):
        k = k_ref[pl.ds(j * tk, tk), :]
        s_ref[slot][...] = jnp.dot(k, qst, preferred_element_type=F32)

    def column_max(slot):
        mx = sub(slot, 0)
        for b in range(1, n_sub):
            mx = jnp.maximum(mx, sub(slot, b))
        st_ref[row(ST_NEXT_MAX), :] = jnp.max(mx, axis=0, keepdims=True)

    def weighted(j, slot, alpha):
        vt = jnp.concatenate([vt_ref[per * j + c] for c in range(per)], axis=1)
        acc_ref[...] = alpha * acc_ref[...] + jnp.dot(vt, p_ref[slot][...], preferred_element_type=F32)

    def step(j, slot, with_scores=True, with_max=True):
        nxt1, nxt2 = (slot + 1) % N_SLOTS, (slot + 2) % N_SLOTS
        a_prev = st_ref[row(ST_ALPHA), :]
        m_old = st_ref[row(ST_M), :]
        m_new = jnp.maximum(m_old, st_ref[row(ST_NEXT_MAX), :])
        alpha = jnp.exp2(m_old - m_new)
        j_prev = jnp.maximum(j - 1, 0)
        rows_q = tk // STEP_PARTS
        subs_q = n_sub // STEP_PARTS
        slab = vt_ref.shape[2]
        psum = jnp.zeros((8, nq), F32)
        mx = None
        part = None
        for q in range(STEP_PARTS):
            r0 = q * rows_q
            if with_scores:
                start = (j + 2) * tk + r0
                start = start if isinstance(start, int) else pl.multiple_of(start, rows_q)
                k = k_ref[pl.ds(start, rows_q), :]
                s_ref[nxt2][r0:r0 + rows_q, :] = jnp.dot(k, qst, preferred_element_type=F32)
            for b in range(q * subs_q, (q + 1) * subs_q):
                p = jnp.exp2(sub(slot, b) - m_new)
                psum = psum + jnp.sum(p.reshape(SOFTMAX_ROWS // 8, 8, nq), axis=0)
                p_ref[slot][b * SOFTMAX_ROWS:(b + 1) * SOFTMAX_ROWS, :] = p.astype(BF16)
            pieces = [vt_ref[per * j_prev + c // slab][:, c % slab:c % slab + min(slab, rows_q)]
                      for c in range(r0, r0 + rows_q, min(slab, rows_q))]
            vt = pieces[0] if len(pieces) == 1 else jnp.concatenate(pieces, axis=1)
            d = jnp.dot(vt, p_ref[nxt2][r0:r0 + rows_q, :], preferred_element_type=F32)
            part = d if part is None else part + d
            if with_max:
                for b in range(q * subs_q, (q + 1) * subs_q):
                    mx = sub(nxt1, b) if mx is None else jnp.maximum(mx, sub(nxt1, b))
        acc_ref[...] = a_prev * acc_ref[...] + part
        st_ref[row(ST_L, 8), :] = alpha * st_ref[row(ST_L, 8), :] + psum
        st_ref[row(ST_M), :] = m_new
        st_ref[row(ST_ALPHA), :] = alpha
        if with_max:
            st_ref[row(ST_NEXT_MAX), :] = jnp.max(mx, axis=0, keepdims=True)

    scores(0, 0)
    scores(1, 1)
    st_ref[...] = jnp.zeros_like(st_ref)
    st_ref[row(ST_M), :] = jnp.full((1, nq), -jnp.inf, F32)
    acc_ref[...] = jnp.zeros_like(acc_ref)
    p_ref[N_SLOTS - 1][...] = jnp.zeros((tk, nq), BF16)
    column_max(0)
    n_full = n_chunks - 2
    n_trips = n_full // STEPS_PER_TRIP
    lead = n_full - n_trips * STEPS_PER_TRIP
    for j in range(lead):
        step(j, j % N_SLOTS)

    def body(t, carry):
        for u in range(STEPS_PER_TRIP):
            step(lead + t * STEPS_PER_TRIP + u, (lead + u) % N_SLOTS)
        return carry

    lax.fori_loop(0, n_trips, body, 0)
    step(n_chunks - 2, (n_chunks - 2) % N_SLOTS, with_scores=False)
    step(n_chunks - 1, (n_chunks - 1) % N_SLOTS, with_scores=False, with_max=False)
    weighted(n_chunks - 1, (n_chunks - 1) % N_SLOTS, st_ref[row(ST_ALPHA), :])
    return acc_ref[...], jnp.sum(st_ref[row(ST_L, 8), :], axis=0, keepdims=True)


def _gqa_kernel(q_ref, k_ref, vt_ref, o_ref, *scratch, tq):
    g = pl.program_id(0)
    lane = lax.broadcasted_iota(jnp.int32, (tq, LANES), 1)
    in_g = (lane // HEAD_DIM) == g
    rows = []
    for hh in range(A_GROUP):
        qp = q_ref[:, LANES * (hh // 2):LANES * (hh // 2 + 1)].astype(F32)
        aligned = jnp.where((hh % 2) == g, qp, pltpu.roll(qp, HEAD_DIM, 1))
        rows.append(jnp.where(in_g, aligned, 0.0))
    qs = jnp.concatenate(rows, axis=0)
    acc_t, l = _flash_cols(qs.T.astype(BF16), k_ref, vt_ref, scratch)
    o_t = acc_t / l
    o = jnp.concatenate([o_t, jnp.zeros_like(o_t)], axis=0).T
    left_half = lane < HEAD_DIM
    for p in range(A_GROUP // 2):
        a = o[(2 * p) * tq:(2 * p + 1) * tq]
        b = o[(2 * p + 1) * tq:(2 * p + 2) * tq]
        o_ref[:, LANES * p:LANES * (p + 1)] = jnp.where(left_half, a, pltpu.roll(b, HEAD_DIM, 1)).astype(BF16)


def _gqa_call(qa, ka, vat, tq, tk):
    seq = qa.shape[0]
    return pl.pallas_call(
        functools.partial(_gqa_kernel, tq=tq),
        grid=(A_KV_HEADS, seq // tq),
        in_specs=[
            pl.BlockSpec((tq, A_GROUP * HEAD_DIM), lambda g, i: (i, g)),
            pl.BlockSpec((seq, A_KV_W), lambda g, i: (0, 0)),
            pl.BlockSpec((vat.shape[0], HEAD_DIM, vat.shape[2]), lambda g, i: (0, g, 0)),
        ],
        out_specs=pl.BlockSpec((tq, A_GROUP * HEAD_DIM), lambda g, i: (i, g)),
        out_shape=jax.ShapeDtypeStruct((seq, A_Q_W), BF16),
        scratch_shapes=_flash_scratch(A_GROUP * tq, tk, HEAD_DIM),
        compiler_params=_cparams(("parallel", "parallel")),
        name="gqa",
    )(qa, ka, vat)


def _diff_kernel(lam_ref, q_ref, k_ref, vt_ref, sg_ref, o_ref, *scratch, tq, lambda_init):
    lane = lax.broadcasted_iota(jnp.int32, (tq, LANES), 1)
    q = q_ref[...].astype(F32)
    qs = jnp.concatenate([jnp.where(lane < HEAD_DIM, q, 0.0),
                          jnp.where(lane >= HEAD_DIM, q, 0.0)], axis=0)
    acc_t, l = _flash_cols(qs.T.astype(BF16), k_ref, vt_ref, scratch)
    o = (acc_t / l).T
    lv = lam_ref[...]
    lam = (jnp.exp(jnp.sum(lv[0:1] * lv[1:2], axis=-1, keepdims=True))
           - jnp.exp(jnp.sum(lv[2:3] * lv[3:4], axis=-1, keepdims=True)) + lambda_init)
    ob = o[:tq] - lam * o[tq:]
    ob = _rms(ob, sg_ref[...], SUBLN_EPS) * (1.0 - lambda_init)
    o_ref[...] = ob.astype(BF16)


def _diff_call(lam_vecs, qb, kb, vbt, subln_g, tq, tk, lambda_init):
    seq = qb.shape[0]
    n_slabs, _, slab = vbt.shape
    return pl.pallas_call(
        functools.partial(_diff_kernel, tq=tq, lambda_init=lambda_init),
        grid=(B_HEADS, seq // tq),
        in_specs=[
            pl.BlockSpec((4, HEAD_DIM), lambda h, i: (0, 0)),
            pl.BlockSpec((tq, LANES), lambda h, i: (i, h)),
            pl.BlockSpec((seq, LANES), lambda h, i: (0, h)),
            pl.BlockSpec((n_slabs, B_V_DIM, slab), lambda h, i: (0, h, 0)),
            pl.BlockSpec((1, B_V_DIM), lambda h, i: (0, 0)),
        ],
        out_specs=pl.BlockSpec((tq, LANES), lambda h, i: (i, h)),
        out_shape=jax.ShapeDtypeStruct((seq, B_V_W), BF16),
        scratch_shapes=_flash_scratch(2 * tq, tk, B_V_DIM),
        compiler_params=_cparams(("parallel", "parallel")),
        name="diff",
    )(lam_vecs, qb, kb, vbt, subln_g)


def _mid_kernel(oa_ref, ob_ref, x_ref, wo_ref, g_ref, wq_ref, sk_ref, h_ref, xn_ref, sc_ref):
    o = jnp.concatenate([oa_ref[...], ob_ref[...]], axis=1)
    h = x_ref[...] + jnp.dot(o, wo_ref[...], preferred_element_type=F32)
    h_ref[...] = h
    xn = _rms(h, g_ref[...], NORM_EPS).astype(BF16)
    xn_ref[...] = xn
    q = jnp.dot(xn, wq_ref[...], preferred_element_type=F32).astype(BF16)
    for hp in range(2 * PEER_HEADS):
        sc_ref[hp] = lax.dot_general(sk_ref[hp], q[:, PEER_HALF * hp:PEER_HALF * (hp + 1)],
                                     NT_DIMS, preferred_element_type=F32)


def _mid_call(oa, ob, x, w_out, g, w_query, sub_keys, tm):
    seq = x.shape[0]
    row = lambda i: (i, 0)
    fix = lambda i: (0, 0)
    nq = 2 * PEER_HEADS * PEER_HALF
    return pl.pallas_call(
        _mid_kernel,
        grid=(seq // tm,),
        in_specs=[
            pl.BlockSpec((tm, A_Q_W), row),
            pl.BlockSpec((tm, B_V_W), row),
            pl.BlockSpec((tm, D_MODEL), row),
            pl.BlockSpec((D_MODEL, D_MODEL), fix),
            pl.BlockSpec((1, D_MODEL), fix),
            pl.BlockSpec((D_MODEL, nq), fix),
            pl.BlockSpec((2 * PEER_HEADS, PEER_KEYS, PEER_HALF), lambda i: (0, 0, 0)),
        ],
        out_specs=[
            pl.BlockSpec((tm, D_MODEL), row),
            pl.BlockSpec((tm, D_MODEL), row),
            pl.BlockSpec((2 * PEER_HEADS, PEER_KEYS, tm), lambda i: (0, 0, i)),
        ],
        out_shape=[
            jax.ShapeDtypeStruct((seq, D_MODEL), F32),
            jax.ShapeDtypeStruct((seq, D_MODEL), BF16),
            jax.ShapeDtypeStruct((2 * PEER_HEADS, PEER_KEYS, seq), F32),
        ],
        compiler_params=_cparams(("parallel",)),
        name="mid",
    )(oa, ob, x, w_out, g, w_query, sub_keys)


N_TOP = PEER_TOPK + 1
_CAND = [(a, b) for a in range(N_TOP) for b in range(N_TOP) if (a + 1) * (b + 1) <= N_TOP]


def _top_vals(s, n):
    vals = []
    for r in range(n):
        m = jnp.max(s, axis=0, keepdims=True)
        vals.append(m)
        if r + 1 < n:
            s = jnp.where(s >= m, -jnp.inf, s)
    return vals


def _topk_kernel(sc_ref, e1_ref, e2_ref, tau_ref):
    t = sc_ref.shape[-1]
    taus = []
    for h in range(PEER_HEADS):
        s1 = sc_ref[2 * h]
        s2 = sc_ref[2 * h + 1]
        v1 = _top_vals(s1, N_TOP)
        v2 = _top_vals(s2, N_TOP)
        cands = [v1[a] + v2[b] for (a, b) in _CAND]
        pad = (-len(cands)) % 8
        cands += [jnp.full((1, t), -jnp.inf, F32)] * pad
        c = jnp.concatenate(cands, axis=0)
        top = _top_vals(c, N_TOP)
        t16, t17 = top[PEER_TOPK - 1], top[PEER_TOPK]
        best = v1[0] + v2[0]
        z = jnp.sum(jnp.where(c >= t16, jnp.exp(c - best), 0.0), axis=0, keepdims=True)
        rz = 1.0 / z
        e1_ref[h] = jnp.exp(s1 - v1[0]) * rz
        e2_ref[h] = jnp.exp(s2 - v2[0])
        taus.append(0.5 * (jnp.exp(t16 - best) + jnp.exp(t17 - best)) * rz)
    tau_ref[...] = jnp.concatenate(taus, axis=0)


def _topk_call(sc, tt):
    seq = sc.shape[-1]
    blk = lambda i: (0, 0, i)
    return pl.pallas_call(
        _topk_kernel,
        grid=(seq // tt,),
        in_specs=[pl.BlockSpec((2 * PEER_HEADS, PEER_KEYS, tt), blk)],
        out_specs=[
            pl.BlockSpec((PEER_HEADS, PEER_KEYS, tt), blk),
            pl.BlockSpec((PEER_HEADS, PEER_KEYS, tt), blk),
            pl.BlockSpec((PEER_HEADS, tt), lambda i: (0, i)),
        ],
        out_shape=[
            jax.ShapeDtypeStruct((PEER_HEADS, PEER_KEYS, seq), F32),
            jax.ShapeDtypeStruct((PEER_HEADS, PEER_KEYS, seq), F32),
            jax.ShapeDtypeStruct((PEER_HEADS, seq), F32),
        ],
        compiler_params=_cparams(("parallel",)),
        name="topk",
    )(sc)


def _peer_kernel(xn_ref, u_ref, vt_ref, e1_ref, e2_ref, tau_ref, h_ref, g_ref, o_ref,
                 acc_ref, a0_ref, a1_ref, w_ref, *, eb, n_eb):
    s = pl.program_id(1)
    tt = xn_ref.shape[0]
    inv_sqrt2 = 1.0 / math.sqrt(2.0)
    a_ref = (a0_ref, a1_ref)
    n_i = eb // PEER_KEYS
    piece = 2 * PEER_KEYS

    def expert_scores(slot):
        a_ref[slot][...] = lax.dot_general(u_ref[...], xn_ref[...], NT_DIMS,
                                           preferred_element_type=F32)

    def gate_and_activate(slot, ii):
        i = (s - 1) * n_i + ii
        rows = slice(PEER_KEYS * ii, PEER_KEYS * (ii + 1))
        e1_rows = [e1_ref[h, pl.ds(i, 1), :] for h in range(PEER_HEADS)]
        for tc in range(tt // LANES):
            cols = slice(LANES * tc, LANES * (tc + 1))
            a = a_ref[slot][rows, cols]
            act = 0.5 * a * (1.0 + lax.erf(a * inv_sqrt2))
            gate = None
            for h in range(PEER_HEADS):
                p = e2_ref[h, :, cols] * e1_rows[h][:, cols]
                term = jnp.where(p >= tau_ref[h:h + 1, cols], p, 0.0)
                gate = term if gate is None else gate + term
            w_ref[rows, cols] = (gate * act).astype(BF16)

    def block_step(slot):
        part = None
        for ii in range(n_i):
            gate_and_activate(slot, ii)
            if (ii + 1) * PEER_KEYS % piece == 0:
                rows = slice((ii + 1) * PEER_KEYS - piece, (ii + 1) * PEER_KEYS)
                d = jnp.dot(vt_ref[:, rows], w_ref[rows, :], preferred_element_type=F32)
                part = d if part is None else part + d
            if ii == n_i // 2 - 1:
                expert_scores(1 - slot)
        acc_ref[...] += part

    @pl.when(s == 0)
    def _():
        acc_ref[...] = jnp.zeros_like(acc_ref)
        expert_scores(0)

    for parity in range(2):
        pl.when((s >= 1) & ((s - 1) % 2 == parity))(functools.partial(block_step, parity))

    @pl.when(s == n_eb)
    def _():
        out = h_ref[...] + acc_ref[...].T
        o_ref[...] = _rms(out, g_ref[...], NORM_EPS)


def _peer_call(xn, u, vt, e1, e2, tau, h, g, tt, eb):
    seq = xn.shape[0]
    n_eb = u.shape[0] // eb
    tok = lambda t, e: (t, 0)
    tok3 = lambda t, e: (0, 0, t)
    return pl.pallas_call(
        functools.partial(_peer_kernel, eb=eb, n_eb=n_eb),
        grid=(seq // tt, n_eb + 1),
        in_specs=[
            pl.BlockSpec((tt, D_MODEL), tok),
            pl.BlockSpec((eb, D_MODEL), lambda t, s: (jnp.minimum(s, n_eb - 1), 0)),
            pl.BlockSpec((D_MODEL, eb), lambda t, s: (0, jnp.maximum(s - 1, 0))),
            pl.BlockSpec((PEER_HEADS, PEER_KEYS, tt), tok3),
            pl.BlockSpec((PEER_HEADS, PEER_KEYS, tt), tok3),
            pl.BlockSpec((PEER_HEADS, tt), lambda t, e: (0, t)),
            pl.BlockSpec((tt, D_MODEL), tok),
            pl.BlockSpec((1, D_MODEL), lambda t, e: (0, 0)),
        ],
        out_specs=pl.BlockSpec((tt, D_MODEL), tok),
        out_shape=jax.ShapeDtypeStruct((seq, D_MODEL), F32),
        scratch_shapes=[pltpu.VMEM((D_MODEL, tt), F32), pltpu.VMEM((eb, tt), F32), pltpu.VMEM((eb, tt), F32),
                        pltpu.VMEM((eb, tt), BF16)],
        compiler_params=_cparams(("parallel", "arbitrary")),
        name="peer",
    )(xn, u, vt, e1, e2, tau, h, g)


def _tiles(seq):
    big = seq >= 4096
    return dict(
        tm=512 if big else 256,
        tq_a=128, tq_b=256,
        tk=1024 if big else 512,
        tt_topk=256, tt_peer=512 if big else 256, eb=1024,
    )


def kernel(x, norm_attn_g, w_in, q_norm_g, k_norm_g, lambda_q1, lambda_k1, lambda_q2, lambda_k2,
           subln_g, w_out, norm_ffn_g, w_query, sub_keys, expert_u, expert_v, norm_final_g):
    batch, seq, d = x.shape
    assert batch == 1 and d == D_MODEL and norm_attn_g.shape[0] == 1
    t = _tiles(seq)
    lambda_init = 0.8 - 0.6 * math.exp(-0.3 * 0)
    x2 = x.reshape(seq, d)

    tab = _rope_tables(seq)
    blk = np.arange(A_Q_W) // HEAD_DIM
    bd = jnp.asarray(np.where(blk[:, None] == blk[None, :], 1.0 / HEAD_DIM, 0.0), dtype=BF16)
    qg = (jnp.tile(q_norm_g[0], A_Q_HEADS) * Q_SCALE).reshape(1, A_Q_W)
    kg = jnp.tile(k_norm_g[0], A_KV_HEADS).reshape(1, A_KV_W)

    qa, ka, vat, qb, kb, vbt = _proj_call(x2, norm_attn_g, w_in[0].astype(BF16), qg, kg, tab, bd, t["tm"])
    oa = _gqa_call(qa, ka, vat, t["tq_a"], t["tk"])
    lam_vecs = jnp.concatenate([lambda_q1, lambda_k1, lambda_q2, lambda_k2], axis=0)
    ob = _diff_call(lam_vecs, qb, kb, vbt, subln_g, t["tq_b"], t["tk"], lambda_init)

    sk = sub_keys[0].reshape(2 * PEER_HEADS, PEER_KEYS, PEER_HALF).astype(BF16)
    h, xn, sc = _mid_call(oa, ob, x2, w_out[0].astype(BF16), norm_ffn_g, w_query[0].astype(BF16), sk, t["tm"])
    e1, e2, tau = _topk_call(sc, t["tt_topk"])
    out = _peer_call(xn, expert_u[0].astype(BF16), expert_v[0].T.astype(BF16), e1, e2, tau, h,
                     norm_final_g.reshape(1, d), t["tt_peer"], t["eb"])
    return out.reshape(batch, seq, d)
```

```python
import functools
import math

import jax
import jax.numpy as jnp
import numpy as np
from jax import lax
from jax.experimental import pallas as pl
from jax.experimental.pallas import tpu as pltpu

F32 = jnp.float32
BF16 = jnp.bfloat16

D_MODEL = 1024
HEAD_DIM = 64
A_Q_HEADS = 8
A_KV_HEADS = 2
A_GROUP = A_Q_HEADS // A_KV_HEADS
B_HEADS = 4
B_V_DIM = 2 * HEAD_DIM
GRID_W = 64
AXIAL_THETA = 10000.0
AXIAL_HALF = HEAD_DIM // 2
ROPE_THETA = 500000.0
ROPE_DIMS = HEAD_DIM // 4
NORM_EPS = 1e-6
SUBLN_EPS = 1e-5
A_Q_W = A_Q_HEADS * HEAD_DIM
A_KV_W = A_KV_HEADS * HEAD_DIM
B_QK_W = 2 * B_HEADS * HEAD_DIM
B_V_W = B_HEADS * B_V_DIM
IN_COLS = A_Q_W + 2 * A_KV_W + 2 * B_QK_W + B_V_W
PEER_HEADS = 8
PEER_KEYS = 128
PEER_HALF = 128
PEER_TOPK = 16
LANES = 128

NT_DIMS = (((1,), (1,)), ((), ()))
Q_SCALE = (HEAD_DIM ** -0.5) * math.log2(math.e)

VMEM_LIMIT = 56 * 1024 * 1024


def _cparams(sem, flags=None):
    return pltpu.CompilerParams(dimension_semantics=sem, vmem_limit_bytes=VMEM_LIMIT, flags=flags)


def _rms(x, g, eps):
    return x * lax.rsqrt(jnp.mean(x * x, axis=-1, keepdims=True) + eps) * g


def _group_mean_sq(v, bd):
    v2 = v * v
    hi = v2.astype(BF16)
    lo = (v2 - hi.astype(F32)).astype(BF16)
    return (jnp.dot(hi, bd, preferred_element_type=F32)
            + jnp.dot(lo, bd, preferred_element_type=F32))


def _rot_half(v, half, group):
    width = v.shape[-1]
    lane = lax.broadcasted_iota(jnp.int32, v.shape, 1)
    fwd = pltpu.roll(v, width - half, 1)
    bwd = pltpu.roll(v, half, 1)
    return jnp.where((lane % group) < half, fwd, bwd)


def _tile4(t):
    return jnp.concatenate([t, t, t, t], axis=1)


ONES_ROWS = 16


def _vt_with_ones(v, dv):
    vt = v.T
    ones = jnp.ones((ONES_ROWS, v.shape[0]), F32)
    parts = []
    for h in range(v.shape[1] // dv):
        parts += [vt[h * dv:(h + 1) * dv], ones]
    return jnp.concatenate(parts, axis=0).astype(BF16)


def _proj_kernel(x_ref, g_ref, w_ref, qg_ref, kg_ref, tab_ref, bd_ref,
                 qa_ref, ka_ref, va_ref, qb_ref, kb_ref, vb_ref):
    xn = _rms(x_ref[...], g_ref[...], NORM_EPS)
    proj = jnp.dot(xn.astype(BF16), w_ref[...], preferred_element_type=F32)
    tab = tab_ref[...]
    cos_a, sin_a = tab[:, 0:128], tab[:, 128:256]
    cos_b, sin_b = tab[:, 256:384], tab[:, 384:512]
    bd = bd_ref[...]

    c0 = 0
    qa = proj[:, c0:c0 + A_Q_W]
    qa = qa * lax.rsqrt(_group_mean_sq(qa, bd) + NORM_EPS) * qg_ref[...]
    qa = qa * _tile4(cos_a) + _rot_half(qa, AXIAL_HALF // 2, AXIAL_HALF) * _tile4(sin_a)
    qa_ref[...] = qa.astype(BF16)
    c0 += A_Q_W

    ka = proj[:, c0:c0 + A_KV_W]
    ka = ka * lax.rsqrt(_group_mean_sq(ka, bd[:A_KV_W, :A_KV_W]) + NORM_EPS) * kg_ref[...]
    ka = ka * cos_a + _rot_half(ka, AXIAL_HALF // 2, AXIAL_HALF) * sin_a
    ka_ref[...] = ka.astype(BF16)
    c0 += A_KV_W

    va_ref[0] = _vt_with_ones(proj[:, c0:c0 + A_KV_W], HEAD_DIM)
    c0 += A_KV_W

    qb = proj[:, c0:c0 + B_QK_W]
    qb = qb * _tile4(cos_b) + _rot_half(qb, ROPE_DIMS // 2, HEAD_DIM) * _tile4(sin_b)
    qb_ref[...] = (qb * Q_SCALE).astype(BF16)
    c0 += B_QK_W

    kb = proj[:, c0:c0 + B_QK_W]
    kb = kb * _tile4(cos_b) + _rot_half(kb, ROPE_DIMS // 2, HEAD_DIM) * _tile4(sin_b)
    kb_ref[...] = kb.astype(BF16)
    c0 += B_QK_W

    vb_ref[0] = _vt_with_ones(proj[:, c0:c0 + B_V_W], B_V_DIM)


def _rope_tables(seq):
    f32 = np.float32
    rows = seq // GRID_W
    row = np.repeat(np.arange(rows, dtype=f32), GRID_W)
    col = np.tile(np.arange(GRID_W, dtype=f32), rows)
    pos = np.arange(seq, dtype=f32)
    inv_ax = (f32(AXIAL_THETA) ** (-np.arange(0, AXIAL_HALF, 2, dtype=f32) / f32(AXIAL_HALF))).astype(f32)
    inv_p = (f32(ROPE_THETA) ** (-np.arange(0, ROPE_DIMS, 2, dtype=f32) / f32(ROPE_DIMS))).astype(f32)
    row_ang = row[:, None] * inv_ax[None, :]
    col_ang = col[:, None] * inv_ax[None, :]
    pos_ang = pos[:, None] * inv_p[None, :]
    cr, sr = np.cos(row_ang), np.sin(row_ang)
    cc, sc = np.cos(col_ang), np.sin(col_ang)
    cp, sp = np.cos(pos_ang), np.sin(pos_ang)
    rest = HEAD_DIM - ROPE_DIMS
    cos_a = np.concatenate([cr, cr, cc, cc], axis=1)
    sin_a = np.concatenate([-sr, sr, -sc, sc], axis=1)
    cos_b = np.concatenate([cp, cp, np.ones((seq, rest), f32)], axis=1)
    sin_b = np.concatenate([-sp, sp, np.zeros((seq, rest), f32)], axis=1)
    two = lambda t: np.concatenate([t, t], axis=1)
    tab = np.concatenate([two(cos_a), two(sin_a), two(cos_b), two(sin_b)], axis=1).astype(f32)
    return jnp.asarray(tab)


def _proj_call(x, g, w_in, qg, kg, tab, bd, tm):
    seq = x.shape[0]
    row = lambda i: (i, 0)
    fix = lambda i: (0, 0)
    n = seq // tm
    rows_out = lambda w: (pl.BlockSpec((tm, w), row), jax.ShapeDtypeStruct((seq, w), BF16))
    cols_out = lambda w: (pl.BlockSpec((1, w, tm), lambda i: (i, 0, 0)), jax.ShapeDtypeStruct((n, w, tm), BF16))
    outs = [rows_out(A_Q_W), rows_out(A_KV_W), cols_out(A_KV_HEADS * (HEAD_DIM + ONES_ROWS)),
            rows_out(B_QK_W), rows_out(B_QK_W), cols_out(B_HEADS * (B_V_DIM + ONES_ROWS))]
    return pl.pallas_call(
        _proj_kernel,
        grid=(seq // tm,),
        in_specs=[
            pl.BlockSpec((tm, D_MODEL), row),
            pl.BlockSpec((1, D_MODEL), fix),
            pl.BlockSpec((D_MODEL, IN_COLS), fix),
            pl.BlockSpec((1, A_Q_W), fix),
            pl.BlockSpec((1, A_KV_W), fix),
            pl.BlockSpec((tm, 512), row),
            pl.BlockSpec((A_Q_W, A_Q_W), fix),
        ],
        out_specs=[o[0] for o in outs],
        out_shape=[o[1] for o in outs],
        compiler_params=_cparams(("parallel",)),
        name="proj",
    )(x, g, w_in, qg, kg, tab, bd)


SOFTMAX_ROWS = 32


N_SLOTS = 4
P_SLOTS = 4
AHEAD = N_SLOTS - 1
STEPS_PER_TRIP = 4
ST_M, ST_ALPHA, ST_NEXT_MAX = 0, 1, 2


def _flash_scratch(nq, tk, dv):
    return ([pltpu.VMEM((tk, nq), F32)] * N_SLOTS + [pltpu.VMEM((tk, nq), BF16)] * P_SLOTS
            + [pltpu.VMEM((dv, nq), F32), pltpu.VMEM((16, nq), F32)])


def _flash_cols(qst, k_ref, vt_ref, scratch):
    s_ref, p_ref = scratch[:N_SLOTS], scratch[N_SLOTS:N_SLOTS + P_SLOTS]
    acc_ref, st_ref = scratch[N_SLOTS + P_SLOTS:]
    return _flash_cols_impl(qst, k_ref, vt_ref, s_ref, p_ref, acc_ref, st_ref)


def _flash_cols_impl(qst, k_ref, vt_ref, s_ref, p_ref, acc_ref, st_ref):
    nq = qst.shape[1]
    tk = s_ref[0].shape[0]
    per = tk // vt_ref.shape[2]
    n_chunks = vt_ref.shape[0] // per
    assert n_chunks > AHEAD and tk % SOFTMAX_ROWS == 0
    n_sub = tk // SOFTMAX_ROWS
    row = lambda r, n=1: slice(r, r + n)
    sub = lambda slot, b: s_ref[slot][b * SOFTMAX_ROWS:(b + 1) * SOFTMAX_ROWS, :]

    def scores(j, slot):
        start = j * tk
        start = start if isinstance(start, int) else pl.multiple_of(start, tk)
        s_ref[slot][...] = jnp.dot(k_ref[pl.ds(start, tk), :], qst, preferred_element_type=F32)

    def column_max(slot):
        mx = sub(slot, 0)
        for b in range(1, n_sub):
            mx = jnp.maximum(mx, sub(slot, b))
        st_ref[row(ST_NEXT_MAX), :] = jnp.max(mx, axis=0, keepdims=True)

    def weighted(j, slot, alpha):
        vt = jnp.concatenate([vt_ref[per * j + c] for c in range(per)], axis=1)
        acc_ref[...] = alpha * acc_ref[...] + jnp.dot(vt, p_ref[slot][...], preferred_element_type=F32)

    def step(j, slot, pslot, with_scores=True, with_max=True):
        a_prev = st_ref[row(ST_ALPHA), :]
        m_old = st_ref[row(ST_M), :]
        m_new = jnp.maximum(m_old, st_ref[row(ST_NEXT_MAX), :])
        alpha = jnp.exp2(m_old - m_new)
        if with_scores:
            scores(j + AHEAD, (slot + AHEAD) % N_SLOTS)
        for b in range(n_sub):
            p = jnp.exp2(sub(slot, b) - m_new)
            p_ref[pslot][b * SOFTMAX_ROWS:(b + 1) * SOFTMAX_ROWS, :] = p.astype(BF16)
        weighted(jnp.maximum(j - 1, 0), (pslot - 1) % P_SLOTS, a_prev)
        st_ref[row(ST_M), :] = m_new
        st_ref[row(ST_ALPHA), :] = alpha
        if with_max:
            column_max((slot + 1) % N_SLOTS)

    for c in range(AHEAD):
        scores(c, c)
    st_ref[...] = jnp.zeros_like(st_ref)
    st_ref[row(ST_M), :] = jnp.full((1, nq), -jnp.inf, F32)
    acc_ref[...] = jnp.zeros_like(acc_ref)
    p_ref[P_SLOTS - 1][...] = jnp.zeros((tk, nq), BF16)
    column_max(0)
    n_full = n_chunks - AHEAD
    n_trips = n_full // STEPS_PER_TRIP
    lead = n_full - n_trips * STEPS_PER_TRIP
    for j in range(lead):
        step(j, j % N_SLOTS, j % P_SLOTS)

    def body(t, carry):
        for u in range(STEPS_PER_TRIP):
            step(lead + t * STEPS_PER_TRIP + u, (lead + u) % N_SLOTS, (lead + u) % P_SLOTS)
        return carry

    lax.fori_loop(0, n_trips, body, 0)
    for j in range(n_full, n_chunks):
        step(j, j % N_SLOTS, j % P_SLOTS, with_scores=False, with_max=j + 1 < n_chunks)
    weighted(n_chunks - 1, (n_chunks - 1) % P_SLOTS, st_ref[row(ST_ALPHA), :])
    dv = acc_ref.shape[0] - ONES_ROWS
    return acc_ref[0:dv, :], acc_ref[dv:dv + 1, :]


def _gqa_kernel(q_ref, k_ref, vt_ref, o_ref, *scratch, tq):
    g = pl.program_id(0)
    lane = lax.broadcasted_iota(jnp.int32, (tq, LANES), 1)
    in_g = (lane // HEAD_DIM) == g
    rows = []
    for hh in range(A_GROUP):
        qp = q_ref[:, LANES * (hh // 2):LANES * (hh // 2 + 1)].astype(F32)
        aligned = jnp.where((hh % 2) == g, qp, pltpu.roll(qp, HEAD_DIM, 1))
        rows.append(jnp.where(in_g, aligned, 0.0))
    qs = jnp.concatenate(rows, axis=0)
    acc_t, l = _flash_cols(qs.T.astype(BF16), k_ref, vt_ref, scratch)
    o_t = acc_t / l
    o = jnp.concatenate([o_t, jnp.zeros_like(o_t)], axis=0).T
    left_half = lane < HEAD_DIM
    for p in range(A_GROUP // 2):
        a = o[(2 * p) * tq:(2 * p + 1) * tq]
        b = o[(2 * p + 1) * tq:(2 * p + 2) * tq]
        o_ref[:, LANES * p:LANES * (p + 1)] = jnp.where(left_half, a, pltpu.roll(b, HEAD_DIM, 1)).astype(BF16)


def _gqa_call(qa, ka, vat, tq, tk):
    seq = qa.shape[0]
    return pl.pallas_call(
        functools.partial(_gqa_kernel, tq=tq),
        grid=(A_KV_HEADS, seq // tq),
        in_specs=[
            pl.BlockSpec((tq, A_GROUP * HEAD_DIM), lambda g, i: (i, g)),
            pl.BlockSpec((seq, A_KV_W), lambda g, i: (0, 0)),
            pl.BlockSpec((vat.shape[0], HEAD_DIM + ONES_ROWS, vat.shape[2]), lambda g, i: (0, g, 0)),
        ],
        out_specs=pl.BlockSpec((tq, A_GROUP * HEAD_DIM), lambda g, i: (i, g)),
        out_shape=jax.ShapeDtypeStruct((seq, A_Q_W), BF16),
        scratch_shapes=_flash_scratch(A_GROUP * tq, tk, HEAD_DIM + ONES_ROWS),
        compiler_params=_cparams(("parallel", "parallel")),
        name="gqa",
    )(qa, ka, vat)


def _diff_kernel(lam_ref, q_ref, k_ref, vt_ref, sg_ref, o_ref, *scratch, tq, lambda_init):
    lane = lax.broadcasted_iota(jnp.int32, (tq, LANES), 1)
    q = q_ref[...].astype(F32)
    qs = jnp.concatenate([jnp.where(lane < HEAD_DIM, q, 0.0),
                          jnp.where(lane >= HEAD_DIM, q, 0.0)], axis=0)
    acc_t, l = _flash_cols(qs.T.astype(BF16), k_ref, vt_ref, scratch)
    o = (acc_t / l).T
    lv = lam_ref[...]
    lam = (jnp.exp(jnp.sum(lv[0:1] * lv[1:2], axis=-1, keepdims=True))
           - jnp.exp(jnp.sum(lv[2:3] * lv[3:4], axis=-1, keepdims=True)) + lambda_init)
    ob = o[:tq] - lam * o[tq:]
    ob = _rms(ob, sg_ref[...], SUBLN_EPS) * (1.0 - lambda_init)
    o_ref[...] = ob.astype(BF16)


def _diff_call(lam_vecs, qb, kb, vbt, subln_g, tq, tk, lambda_init):
    seq = qb.shape[0]
    n_slabs, _, slab = vbt.shape
    return pl.pallas_call(
        functools.partial(_diff_kernel, tq=tq, lambda_init=lambda_init),
        grid=(B_HEADS, seq // tq),
        in_specs=[
            pl.BlockSpec((4, HEAD_DIM), lambda h, i: (0, 0)),
            pl.BlockSpec((tq, LANES), lambda h, i: (i, h)),
            pl.BlockSpec((seq, LANES), lambda h, i: (0, h)),
            pl.BlockSpec((n_slabs, B_V_DIM + ONES_ROWS, slab), lambda h, i: (0, h, 0)),
            pl.BlockSpec((1, B_V_DIM), lambda h, i: (0, 0)),
        ],
        out_specs=pl.BlockSpec((tq, LANES), lambda h, i: (i, h)),
        out_shape=jax.ShapeDtypeStruct((seq, B_V_W), BF16),
        scratch_shapes=_flash_scratch(2 * tq, tk, B_V_DIM + ONES_ROWS),
        compiler_params=_cparams(("parallel", "parallel")),
        name="diff",
    )(lam_vecs, qb, kb, vbt, subln_g)


def _mid_kernel(oa_ref, ob_ref, x_ref, wo_ref, g_ref, wq_ref, sk_ref, h_ref, xn_ref, sc_ref):
    o = jnp.concatenate([oa_ref[...], ob_ref[...]], axis=1)
    h = x_ref[...] + jnp.dot(o, wo_ref[...], preferred_element_type=F32)
    h_ref[...] = h
    xn = _rms(h, g_ref[...], NORM_EPS).astype(BF16)
    xn_ref[...] = xn
    q = jnp.dot(xn, wq_ref[...], preferred_element_type=F32).astype(BF16)
    for hp in range(2 * PEER_HEADS):
        sc_ref[hp] = lax.dot_general(sk_ref[hp], q[:, PEER_HALF * hp:PEER_HALF * (hp + 1)],
                                     NT_DIMS, preferred_element_type=F32)


def _mid_call(oa, ob, x, w_out, g, w_query, sub_keys, tm):
    seq = x.shape[0]
    row = lambda i: (i, 0)
    fix = lambda i: (0, 0)
    nq = 2 * PEER_HEADS * PEER_HALF
    return pl.pallas_call(
        _mid_kernel,
        grid=(seq // tm,),
        in_specs=[
            pl.BlockSpec((tm, A_Q_W), row),
            pl.BlockSpec((tm, B_V_W), row),
            pl.BlockSpec((tm, D_MODEL), row),
            pl.BlockSpec((D_MODEL, D_MODEL), fix),
            pl.BlockSpec((1, D_MODEL), fix),
            pl.BlockSpec((D_MODEL, nq), fix),
            pl.BlockSpec((2 * PEER_HEADS, PEER_KEYS, PEER_HALF), lambda i: (0, 0, 0)),
        ],
        out_specs=[
            pl.BlockSpec((tm, D_MODEL), row),
            pl.BlockSpec((tm, D_MODEL), row),
            pl.BlockSpec((2 * PEER_HEADS, PEER_KEYS, tm), lambda i: (0, 0, i)),
        ],
        out_shape=[
            jax.ShapeDtypeStruct((seq, D_MODEL), F32),
            jax.ShapeDtypeStruct((seq, D_MODEL), BF16),
            jax.ShapeDtypeStruct((2 * PEER_HEADS, PEER_KEYS, seq), F32),
        ],
        compiler_params=_cparams(("parallel",)),
        name="mid",
    )(oa, ob, x, w_out, g, w_query, sub_keys)


N_TOP = PEER_TOPK + 1
_CAND = [(a, b) for a in range(N_TOP) for b in range(N_TOP) if (a + 1) * (b + 1) <= N_TOP]


def _top_vals(s, n):
    vals = []
    for r in range(n):
        m = jnp.max(s, axis=0, keepdims=True)
        vals.append(m)
        if r + 1 < n:
            s = jnp.where(s >= m, -jnp.inf, s)
    return vals


def _topk_kernel(sc_ref, e1_ref, e2_ref, tau_ref):
    t = sc_ref.shape[-1]
    taus = []
    for h in range(PEER_HEADS):
        s1 = sc_ref[2 * h]
        s2 = sc_ref[2 * h + 1]
        v1 = _top_vals(s1, N_TOP)
        v2 = _top_vals(s2, N_TOP)
        cands = [v1[a] + v2[b] for (a, b) in _CAND]
        pad = (-len(cands)) % 8
        cands += [jnp.full((1, t), -jnp.inf, F32)] * pad
        c = jnp.concatenate(cands, axis=0)
        top = _top_vals(c, N_TOP)
        t16, t17 = top[PEER_TOPK - 1], top[PEER_TOPK]
        best = v1[0] + v2[0]
        z = jnp.sum(jnp.where(c >= t16, jnp.exp(c - best), 0.0), axis=0, keepdims=True)
        rz = 1.0 / z
        e1_ref[h] = jnp.exp(s1 - v1[0]) * rz
        e2_ref[h] = jnp.exp(s2 - v2[0])
        taus.append(0.5 * (jnp.exp(t16 - best) + jnp.exp(t17 - best)) * rz)
    tau_ref[...] = jnp.concatenate(taus, axis=0)


def _topk_call(sc, tt):
    seq = sc.shape[-1]
    blk = lambda i: (0, 0, i)
    return pl.pallas_call(
        _topk_kernel,
        grid=(seq // tt,),
        in_specs=[pl.BlockSpec((2 * PEER_HEADS, PEER_KEYS, tt), blk)],
        out_specs=[
            pl.BlockSpec((PEER_HEADS, PEER_KEYS, tt), blk),
            pl.BlockSpec((PEER_HEADS, PEER_KEYS, tt), blk),
            pl.BlockSpec((PEER_HEADS, tt), lambda i: (0, i)),
        ],
        out_shape=[
            jax.ShapeDtypeStruct((PEER_HEADS, PEER_KEYS, seq), F32),
            jax.ShapeDtypeStruct((PEER_HEADS, PEER_KEYS, seq), F32),
            jax.ShapeDtypeStruct((PEER_HEADS, seq), F32),
        ],
        compiler_params=_cparams(("parallel",)),
        name="topk",
    )(sc)


def _peer_kernel(xn_ref, u_ref, vt_ref, e1_ref, e2_ref, tau_ref, h_ref, g_ref, o_ref,
                 acc_ref, a_ref, w_ref, *, eb):
    e = pl.program_id(1)
    tt = xn_ref.shape[0]
    inv_sqrt2 = 1.0 / math.sqrt(2.0)
    a_ref[...] = lax.dot_general(u_ref[...], xn_ref[...], NT_DIMS,
                                 preferred_element_type=F32)
    for ii in range(eb // PEER_KEYS):
        i = e * (eb // PEER_KEYS) + ii
        rows = slice(PEER_KEYS * ii, PEER_KEYS * (ii + 1))
        e1_rows = [e1_ref[h, pl.ds(i, 1), :] for h in range(PEER_HEADS)]
        for tc in range(tt // LANES):
            cols = slice(LANES * tc, LANES * (tc + 1))
            a = a_ref[rows, cols]
            act = 0.5 * a * (1.0 + lax.erf(a * inv_sqrt2))
            gate = None
            for h in range(PEER_HEADS):
                p = e2_ref[h, :, cols] * e1_rows[h][:, cols]
                term = jnp.where(p >= tau_ref[h:h + 1, cols], p, 0.0)
                gate = term if gate is None else gate + term
            w_ref[rows, cols] = (gate * act).astype(BF16)
    part = jnp.dot(vt_ref[...], w_ref[...], preferred_element_type=F32)

    @pl.when(e == 0)
    def _():
        acc_ref[...] = part

    @pl.when(e > 0)
    def _():
        acc_ref[...] += part

    @pl.when(e == pl.num_programs(1) - 1)
    def _():
        out = h_ref[...] + acc_ref[...].T
        o_ref[...] = _rms(out, g_ref[...], NORM_EPS)


def _peer_call(xn, u, vt, e1, e2, tau, h, g, tt, eb):
    seq = xn.shape[0]
    n_exp = u.shape[0]
    tok = lambda t, e: (t, 0)
    tok3 = lambda t, e: (0, 0, t)
    return pl.pallas_call(
        functools.partial(_peer_kernel, eb=eb),
        grid=(seq // tt, n_exp // eb),
        in_specs=[
            pl.BlockSpec((tt, D_MODEL), tok),
            pl.BlockSpec((eb, D_MODEL), lambda t, e: (e, 0)),
            pl.BlockSpec((D_MODEL, eb), lambda t, e: (0, e)),
            pl.BlockSpec((PEER_HEADS, PEER_KEYS, tt), tok3),
            pl.BlockSpec((PEER_HEADS, PEER_KEYS, tt), tok3),
            pl.BlockSpec((PEER_HEADS, tt), lambda t, e: (0, t)),
            pl.BlockSpec((tt, D_MODEL), tok),
            pl.BlockSpec((1, D_MODEL), lambda t, e: (0, 0)),
        ],
        out_specs=pl.BlockSpec((tt, D_MODEL), tok),
        out_shape=jax.ShapeDtypeStruct((seq, D_MODEL), F32),
        scratch_shapes=[pltpu.VMEM((D_MODEL, tt), F32), pltpu.VMEM((eb, tt), F32), pltpu.VMEM((eb, tt), BF16)],
        compiler_params=_cparams(("parallel", "arbitrary")),
        name="peer",
    )(xn, u, vt, e1, e2, tau, h, g)


def _tiles(seq):
    big = seq >= 4096
    return dict(
        tm=512 if big else 256,
        tq_a=128, tq_b=256,
        tk=1024 if big else 512,
        tt_topk=256, tt_peer=512 if big else 256, eb=1024,
    )


def kernel(x, norm_attn_g, w_in, q_norm_g, k_norm_g, lambda_q1, lambda_k1, lambda_q2, lambda_k2,
           subln_g, w_out, norm_ffn_g, w_query, sub_keys, expert_u, expert_v, norm_final_g):
    batch, seq, d = x.shape
    assert batch == 1 and d == D_MODEL and norm_attn_g.shape[0] == 1
    t = _tiles(seq)
    lambda_init = 0.8 - 0.6 * math.exp(-0.3 * 0)
    x2 = x.reshape(seq, d)

    tab = _rope_tables(seq)
    blk = np.arange(A_Q_W) // HEAD_DIM
    bd = jnp.asarray(np.where(blk[:, None] == blk[None, :], 1.0 / HEAD_DIM, 0.0), dtype=BF16)
    qg = (jnp.tile(q_norm_g[0], A_Q_HEADS) * Q_SCALE).reshape(1, A_Q_W)
    kg = jnp.tile(k_norm_g[0], A_KV_HEADS).reshape(1, A_KV_W)

    qa, ka, vat, qb, kb, vbt = _proj_call(x2, norm_attn_g, w_in[0].astype(BF16), qg, kg, tab, bd, t["tm"])
    oa = _gqa_call(qa, ka, vat, t["tq_a"], t["tk"])
    lam_vecs = jnp.concatenate([lambda_q1, lambda_k1, lambda_q2, lambda_k2], axis=0)
    ob = _diff_call(lam_vecs, qb, kb, vbt, subln_g, t["tq_b"], t["tk"], lambda_init)

    sk = sub_keys[0].reshape(2 * PEER_HEADS, PEER_KEYS, PEER_HALF).astype(BF16)
    h, xn, sc = _mid_call(oa, ob, x2, w_out[0].astype(BF16), norm_ffn_g, w_query[0].astype(BF16), sk, t["tm"])
    e1, e2, tau = _topk_call(sc, t["tt_topk"])
    out = _peer_call(xn, expert_u[0].astype(BF16), expert_v[0].T.astype(BF16), e1, e2, tau, h,
                     norm_final_g.reshape(1, d), t["tt_peer"], t["eb"])
    return out.reshape(batch, seq, d)
```

```python
import functools
import math

import jax
import jax.numpy as jnp
import numpy as np
from jax import lax
from jax.experimental import pallas as pl
from jax.experimental.pallas import tpu as pltpu

F32 = jnp.float32
BF16 = jnp.bfloat16

D_MODEL = 1024
HEAD_DIM = 64
A_Q_HEADS = 8
A_KV_HEADS = 2
A_GROUP = A_Q_HEADS // A_KV_HEADS
B_HEADS = 4
B_V_DIM = 2 * HEAD_DIM
GRID_W = 64
AXIAL_THETA = 10000.0
AXIAL_HALF = HEAD_DIM // 2
ROPE_THETA = 500000.0
ROPE_DIMS = HEAD_DIM // 4
NORM_EPS = 1e-6
SUBLN_EPS = 1e-5
A_Q_W = A_Q_HEADS * HEAD_DIM
A_KV_W = A_KV_HEADS * HEAD_DIM
B_QK_W = 2 * B_HEADS * HEAD_DIM
B_V_W = B_HEADS * B_V_DIM
IN_COLS = A_Q_W + 2 * A_KV_W + 2 * B_QK_W + B_V_W
PEER_HEADS = 8
PEER_KEYS = 128
PEER_HALF = 128
PEER_TOPK = 16
LANES = 128

NT_DIMS = (((1,), (1,)), ((), ()))
Q_SCALE = (HEAD_DIM ** -0.5) * math.log2(math.e)

VMEM_LIMIT = 56 * 1024 * 1024


def _cparams(sem):
    return pltpu.CompilerParams(dimension_semantics=sem, vmem_limit_bytes=VMEM_LIMIT)


def _rms(x, g, eps):
    return x * lax.rsqrt(jnp.mean(x * x, axis=-1, keepdims=True) + eps) * g


def _group_mean_sq(v, bd):
    v2 = v * v
    hi = v2.astype(BF16)
    lo = (v2 - hi.astype(F32)).astype(BF16)
    return (jnp.dot(hi, bd, preferred_element_type=F32)
            + jnp.dot(lo, bd, preferred_element_type=F32))


def _rot_half(v, half, group):
    width = v.shape[-1]
    lane = lax.broadcasted_iota(jnp.int32, v.shape, 1)
    fwd = pltpu.roll(v, width - half, 1)
    bwd = pltpu.roll(v, half, 1)
    return jnp.where((lane % group) < half, fwd, bwd)


def _tile4(t):
    return jnp.concatenate([t, t, t, t], axis=1)


def _proj_kernel(x_ref, g_ref, w_ref, qg_ref, kg_ref, tab_ref, bd_ref,
                 qa_ref, ka_ref, va_ref, qb_ref, kb_ref, vb_ref):
    xn = _rms(x_ref[...], g_ref[...], NORM_EPS)
    proj = jnp.dot(xn.astype(BF16), w_ref[...], preferred_element_type=F32)
    tab = tab_ref[...]
    cos_a, sin_a = tab[:, 0:128], tab[:, 128:256]
    cos_b, sin_b = tab[:, 256:384], tab[:, 384:512]
    bd = bd_ref[...]

    c0 = 0
    qa = proj[:, c0:c0 + A_Q_W]
    qa = qa * lax.rsqrt(_group_mean_sq(qa, bd) + NORM_EPS) * qg_ref[...]
    qa = qa * _tile4(cos_a) + _rot_half(qa, AXIAL_HALF // 2, AXIAL_HALF) * _tile4(sin_a)
    qa_ref[...] = qa.astype(BF16)
    c0 += A_Q_W

    ka = proj[:, c0:c0 + A_KV_W]
    ka = ka * lax.rsqrt(_group_mean_sq(ka, bd[:A_KV_W, :A_KV_W]) + NORM_EPS) * kg_ref[...]
    ka = ka * cos_a + _rot_half(ka, AXIAL_HALF // 2, AXIAL_HALF) * sin_a
    ka_ref[...] = ka.astype(BF16)
    c0 += A_KV_W

    va_ref[0] = proj[:, c0:c0 + A_KV_W].T.astype(BF16)
    c0 += A_KV_W

    qb = proj[:, c0:c0 + B_QK_W]
    qb = qb * _tile4(cos_b) + _rot_half(qb, ROPE_DIMS // 2, HEAD_DIM) * _tile4(sin_b)
    qb_ref[...] = (qb * Q_SCALE).astype(BF16)
    c0 += B_QK_W

    kb = proj[:, c0:c0 + B_QK_W]
    kb = kb * _tile4(cos_b) + _rot_half(kb, ROPE_DIMS // 2, HEAD_DIM) * _tile4(sin_b)
    kb_ref[...] = kb.astype(BF16)
    c0 += B_QK_W

    vb_ref[0] = proj[:, c0:c0 + B_V_W].T.astype(BF16)


def _rope_tables(seq):
    f32 = np.float32
    rows = seq // GRID_W
    row = np.repeat(np.arange(rows, dtype=f32), GRID_W)
    col = np.tile(np.arange(GRID_W, dtype=f32), rows)
    pos = np.arange(seq, dtype=f32)
    inv_ax = (f32(AXIAL_THETA) ** (-np.arange(0, AXIAL_HALF, 2, dtype=f32) / f32(AXIAL_HALF))).astype(f32)
    inv_p = (f32(ROPE_THETA) ** (-np.arange(0, ROPE_DIMS, 2, dtype=f32) / f32(ROPE_DIMS))).astype(f32)
    row_ang = row[:, None] * inv_ax[None, :]
    col_ang = col[:, None] * inv_ax[None, :]
    pos_ang = pos[:, None] * inv_p[None, :]
    cr, sr = np.cos(row_ang), np.sin(row_ang)
    cc, sc = np.cos(col_ang), np.sin(col_ang)
    cp, sp = np.cos(pos_ang), np.sin(pos_ang)
    rest = HEAD_DIM - ROPE_DIMS
    cos_a = np.concatenate([cr, cr, cc, cc], axis=1)
    sin_a = np.concatenate([-sr, sr, -sc, sc], axis=1)
    cos_b = np.concatenate([cp, cp, np.ones((seq, rest), f32)], axis=1)
    sin_b = np.concatenate([-sp, sp, np.zeros((seq, rest), f32)], axis=1)
    two = lambda t: np.concatenate([t, t], axis=1)
    tab = np.concatenate([two(cos_a), two(sin_a), two(cos_b), two(sin_b)], axis=1).astype(f32)
    return jnp.asarray(tab)


def _proj_call(x, g, w_in, qg, kg, tab, bd, tm):
    seq = x.shape[0]
    row = lambda i: (i, 0)
    fix = lambda i: (0, 0)
    n = seq // tm
    rows_out = lambda w: (pl.BlockSpec((tm, w), row), jax.ShapeDtypeStruct((seq, w), BF16))
    cols_out = lambda w: (pl.BlockSpec((1, w, tm), lambda i: (i, 0, 0)), jax.ShapeDtypeStruct((n, w, tm), BF16))
    outs = [rows_out(A_Q_W), rows_out(A_KV_W), cols_out(A_KV_W), rows_out(B_QK_W), rows_out(B_QK_W), cols_out(B_V_W)]
    return pl.pallas_call(
        _proj_kernel,
        grid=(seq // tm,),
        in_specs=[
            pl.BlockSpec((tm, D_MODEL), row),
            pl.BlockSpec((1, D_MODEL), fix),
            pl.BlockSpec((D_MODEL, IN_COLS), fix),
            pl.BlockSpec((1, A_Q_W), fix),
            pl.BlockSpec((1, A_KV_W), fix),
            pl.BlockSpec((tm, 512), row),
            pl.BlockSpec((A_Q_W, A_Q_W), fix),
        ],
        out_specs=[o[0] for o in outs],
        out_shape=[o[1] for o in outs],
        compiler_params=_cparams(("parallel",)),
        name="proj",
    )(x, g, w_in, qg, kg, tab, bd)


SOFTMAX_ROWS = 16
N_SLOTS = 3
STEPS_PER_TRIP = 3
ST_M, ST_ALPHA, ST_NEXT_MAX, ST_L = 0, 1, 2, 8


def _flash_scratch(nq, tk, dv):
    return ([pltpu.VMEM((tk, nq), F32)] * N_SLOTS + [pltpu.VMEM((tk, nq), BF16)] * N_SLOTS
            + [pltpu.VMEM((dv, nq), F32), pltpu.VMEM((16, nq), F32)])


def _flash_cols(qst, k_ref, vt_ref, scratch):
    s_ref, p_ref = scratch[:N_SLOTS], scratch[N_SLOTS:2 * N_SLOTS]
    acc_ref, st_ref = scratch[2 * N_SLOTS:]
    nq = qst.shape[1]
    tk = s_ref[0].shape[0]
    per = tk // vt_ref.shape[2]
    n_chunks = vt_ref.shape[0] // per
    assert n_chunks >= N_SLOTS and tk % SOFTMAX_ROWS == 0
    n_sub = tk // SOFTMAX_ROWS
    row = lambda r, n=1: slice(r, r + n)
    sub = lambda slot, b: s_ref[slot][b * SOFTMAX_ROWS:(b + 1) * SOFTMAX_ROWS, :]

    def scores(j, slot):
        start = j * tk
        start = start if isinstance(start, int) else pl.multiple_of(start, tk)
        s_ref[slot][...] = jnp.dot(k_ref[pl.ds(start, tk), :], qst, preferred_element_type=F32)

    def column_max(slot):
        mx = sub(slot, 0)
        for b in range(1, n_sub):
            mx = jnp.maximum(mx, sub(slot, b))
        st_ref[row(ST_NEXT_MAX), :] = jnp.max(mx, axis=0, keepdims=True)

    def weighted(j, slot, alpha):
        vt = jnp.concatenate([vt_ref[per * j + c] for c in range(per)], axis=1)
        acc_ref[...] = alpha * acc_ref[...] + jnp.dot(vt, p_ref[slot][...], preferred_element_type=F32)

    def step(j, slot, with_scores=True, with_max=True):
        a_prev = st_ref[row(ST_ALPHA), :]
        m_old = st_ref[row(ST_M), :]
        m_new = jnp.maximum(m_old, st_ref[row(ST_NEXT_MAX), :])
        alpha = jnp.exp2(m_old - m_new)
        if with_scores:
            scores(j + 2, (slot + 2) % N_SLOTS)
        psum = jnp.zeros((8, nq), F32)
        for b in range(n_sub):
            p = jnp.exp2(sub(slot, b) - m_new)
            psum = psum + jnp.sum(p.reshape(SOFTMAX_ROWS // 8, 8, nq), axis=0)
            p_ref[slot][b * SOFTMAX_ROWS:(b + 1) * SOFTMAX_ROWS, :] = p.astype(BF16)
        weighted(jnp.maximum(j - 1, 0), (slot + 2) % N_SLOTS, a_prev)
        st_ref[row(ST_L, 8), :] = alpha * st_ref[row(ST_L, 8), :] + psum
        st_ref[row(ST_M), :] = m_new
        st_ref[row(ST_ALPHA), :] = alpha
        if with_max:
            column_max((slot + 1) % N_SLOTS)

    scores(0, 0)
    scores(1, 1)
    st_ref[...] = jnp.zeros_like(st_ref)
    st_ref[row(ST_M), :] = jnp.full((1, nq), -jnp.inf, F32)
    acc_ref[...] = jnp.zeros_like(acc_ref)
    p_ref[N_SLOTS - 1][...] = jnp.zeros((tk, nq), BF16)
    column_max(0)
    n_full = n_chunks - 2
    n_trips = n_full // STEPS_PER_TRIP
    lead = n_full - n_trips * STEPS_PER_TRIP
    for j in range(lead):
        step(j, j % N_SLOTS)

    def body(t, carry):
        for u in range(STEPS_PER_TRIP):
            step(lead + t * STEPS_PER_TRIP + u, (lead + u) % N_SLOTS)
        return carry

    lax.fori_loop(0, n_trips, body, 0)
    step(n_chunks - 2, (n_chunks - 2) % N_SLOTS, with_scores=False)
    step(n_chunks - 1, (n_chunks - 1) % N_SLOTS, with_scores=False, with_max=False)
    weighted(n_chunks - 1, (n_chunks - 1) % N_SLOTS, st_ref[row(ST_ALPHA), :])
    return acc_ref[...], jnp.sum(st_ref[row(ST_L, 8), :], axis=0, keepdims=True)


def _gqa_kernel(q_ref, k_ref, vt_ref, o_ref, *scratch, tq):
    g = pl.program_id(0)
    lane = lax.broadcasted_iota(jnp.int32, (tq, LANES), 1)
    in_g = (lane // HEAD_DIM) == g
    rows = []
    for hh in range(A_GROUP):
        qp = q_ref[:, LANES * (hh // 2):LANES * (hh // 2 + 1)].astype(F32)
        aligned = jnp.where((hh % 2) == g, qp, pltpu.roll(qp, HEAD_DIM, 1))
        rows.append(jnp.where(in_g, aligned, 0.0))
    qs = jnp.concatenate(rows, axis=0)
    acc_t, l = _flash_cols(qs.T.astype(BF16), k_ref, vt_ref, scratch)
    o_t = acc_t / l
    o = jnp.concatenate([o_t, jnp.zeros_like(o_t)], axis=0).T
    left_half = lane < HEAD_DIM
    for p in range(A_GROUP // 2):
        a = o[(2 * p) * tq:(2 * p + 1) * tq]
        b = o[(2 * p + 1) * tq:(2 * p + 2) * tq]
        o_ref[:, LANES * p:LANES * (p + 1)] = jnp.where(left_half, a, pltpu.roll(b, HEAD_DIM, 1)).astype(BF16)


def _gqa_call(qa, ka, vat, tq, tk):
    seq = qa.shape[0]
    return pl.pallas_call(
        functools.partial(_gqa_kernel, tq=tq),
        grid=(A_KV_HEADS, seq // tq),
        in_specs=[
            pl.BlockSpec((tq, A_GROUP * HEAD_DIM), lambda g, i: (i, g)),
            pl.BlockSpec((seq, A_KV_W), lambda g, i: (0, 0)),
            pl.BlockSpec((vat.shape[0], HEAD_DIM, vat.shape[2]), lambda g, i: (0, g, 0)),
        ],
        out_specs=pl.BlockSpec((tq, A_GROUP * HEAD_DIM), lambda g, i: (i, g)),
        out_shape=jax.ShapeDtypeStruct((seq, A_Q_W), BF16),
        scratch_shapes=_flash_scratch(A_GROUP * tq, tk, HEAD_DIM),
        compiler_params=_cparams(("parallel", "parallel")),
        name="gqa",
    )(qa, ka, vat)


def _diff_kernel(lam_ref, q_ref, k_ref, vt_ref, sg_ref, o_ref, *scratch, tq, lambda_init):
    lane = lax.broadcasted_iota(jnp.int32, (tq, LANES), 1)
    q = q_ref[...].astype(F32)
    qs = jnp.concatenate([jnp.where(lane < HEAD_DIM, q, 0.0),
                          jnp.where(lane >= HEAD_DIM, q, 0.0)], axis=0)
    acc_t, l = _flash_cols(qs.T.astype(BF16), k_ref, vt_ref, scratch)
    o = (acc_t / l).T
    lv = lam_ref[...]
    lam = (jnp.exp(jnp.sum(lv[0:1] * lv[1:2], axis=-1, keepdims=True))
           - jnp.exp(jnp.sum(lv[2:3] * lv[3:4], axis=-1, keepdims=True)) + lambda_init)
    ob = o[:tq] - lam * o[tq:]
    ob = _rms(ob, sg_ref[...], SUBLN_EPS) * (1.0 - lambda_init)
    o_ref[...] = ob.astype(BF16)


def _diff_call(lam_vecs, qb, kb, vbt, subln_g, tq, tk, lambda_init):
    seq = qb.shape[0]
    n_slabs, _, slab = vbt.shape
    return pl.pallas_call(
        functools.partial(_diff_kernel, tq=tq, lambda_init=lambda_init),
        grid=(B_HEADS, seq // tq),
        in_specs=[
            pl.BlockSpec((4, HEAD_DIM), lambda h, i: (0, 0)),
            pl.BlockSpec((tq, LANES), lambda h, i: (i, h)),
            pl.BlockSpec((seq, LANES), lambda h, i: (0, h)),
            pl.BlockSpec((n_slabs, B_V_DIM, slab), lambda h, i: (0, h, 0)),
            pl.BlockSpec((1, B_V_DIM), lambda h, i: (0, 0)),
        ],
        out_specs=pl.BlockSpec((tq, LANES), lambda h, i: (i, h)),
        out_shape=jax.ShapeDtypeStruct((seq, B_V_W), BF16),
        scratch_shapes=_flash_scratch(2 * tq, tk, B_V_DIM),
        compiler_params=_cparams(("parallel", "parallel")),
        name="diff",
    )(lam_vecs, qb, kb, vbt, subln_g)


def _mid_kernel(oa_ref, ob_ref, x_ref, wo_ref, g_ref, wq_ref, sk_ref, h_ref, xn_ref, sc_ref):
    o = jnp.concatenate([oa_ref[...], ob_ref[...]], axis=1)
    h = x_ref[...] + jnp.dot(o, wo_ref[...], preferred_element_type=F32)
    h_ref[...] = h
    xn = _rms(h, g_ref[...], NORM_EPS).astype(BF16)
    xn_ref[...] = xn
    q = jnp.dot(xn, wq_ref[...], preferred_element_type=F32).astype(BF16)
    for hp in range(2 * PEER_HEADS):
        sc_ref[hp] = lax.dot_general(sk_ref[hp], q[:, PEER_HALF * hp:PEER_HALF * (hp + 1)],
                                     NT_DIMS, preferred_element_type=F32)


def _mid_call(oa, ob, x, w_out, g, w_query, sub_keys, tm):
    seq = x.shape[0]
    row = lambda i: (i, 0)
    fix = lambda i: (0, 0)
    nq = 2 * PEER_HEADS * PEER_HALF
    return pl.pallas_call(
        _mid_kernel,
        grid=(seq // tm,),
        in_specs=[
            pl.BlockSpec((tm, A_Q_W), row),
            pl.BlockSpec((tm, B_V_W), row),
            pl.BlockSpec((tm, D_MODEL), row),
            pl.BlockSpec((D_MODEL, D_MODEL), fix),
            pl.BlockSpec((1, D_MODEL), fix),
            pl.BlockSpec((D_MODEL, nq), fix),
            pl.BlockSpec((2 * PEER_HEADS, PEER_KEYS, PEER_HALF), lambda i: (0, 0, 0)),
        ],
        out_specs=[
            pl.BlockSpec((tm, D_MODEL), row),
            pl.BlockSpec((tm, D_MODEL), row),
            pl.BlockSpec((2 * PEER_HEADS, PEER_KEYS, tm), lambda i: (0, 0, i)),
        ],
        out_shape=[
            jax.ShapeDtypeStruct((seq, D_MODEL), F32),
            jax.ShapeDtypeStruct((seq, D_MODEL), BF16),
            jax.ShapeDtypeStruct((2 * PEER_HEADS, PEER_KEYS, seq), F32),
        ],
        compiler_params=_cparams(("parallel",)),
        name="mid",
    )(oa, ob, x, w_out, g, w_query, sub_keys)


N_TOP = PEER_TOPK + 1
_CAND = [(a, b) for a in range(N_TOP) for b in range(N_TOP) if (a + 1) * (b + 1) <= N_TOP]


def _top_vals(s, n):
    vals = []
    for r in range(n):
        m = jnp.max(s, axis=0, keepdims=True)
        vals.append(m)
        if r + 1 < n:
            s = jnp.where(s >= m, -jnp.inf, s)
    return vals


def _topk_kernel(sc_ref, e1_ref, e2_ref, tau_ref):
    t = sc_ref.shape[-1]
    taus = []
    for h in range(PEER_HEADS):
        s1 = sc_ref[2 * h]
        s2 = sc_ref[2 * h + 1]
        v1 = _top_vals(s1, N_TOP)
        v2 = _top_vals(s2, N_TOP)
        cands = [v1[a] + v2[b] for (a, b) in _CAND]
        pad = (-len(cands)) % 8
        cands += [jnp.full((1, t), -jnp.inf, F32)] * pad
        c = jnp.concatenate(cands, axis=0)
        top = _top_vals(c, N_TOP)
        t16, t17 = top[PEER_TOPK - 1], top[PEER_TOPK]
        best = v1[0] + v2[0]
        z = jnp.sum(jnp.where(c >= t16, jnp.exp(c - best), 0.0), axis=0, keepdims=True)
        rz = 1.0 / z
        e1_ref[h] = jnp.exp(s1 - v1[0]) * rz
        e2_ref[h] = jnp.exp(s2 - v2[0])
        taus.append(0.5 * (jnp.exp(t16 - best) + jnp.exp(t17 - best)) * rz)
    tau_ref[...] = jnp.concatenate(taus, axis=0)


def _topk_call(sc, tt):
    seq = sc.shape[-1]
    blk = lambda i: (0, 0, i)
    return pl.pallas_call(
        _topk_kernel,
        grid=(seq // tt,),
        in_specs=[pl.BlockSpec((2 * PEER_HEADS, PEER_KEYS, tt), blk)],
        out_specs=[
            pl.BlockSpec((PEER_HEADS, PEER_KEYS, tt), blk),
            pl.BlockSpec((PEER_HEADS, PEER_KEYS, tt), blk),
            pl.BlockSpec((PEER_HEADS, tt), lambda i: (0, i)),
        ],
        out_shape=[
            jax.ShapeDtypeStruct((PEER_HEADS, PEER_KEYS, seq), F32),
            jax.ShapeDtypeStruct((PEER_HEADS, PEER_KEYS, seq), F32),
            jax.ShapeDtypeStruct((PEER_HEADS, seq), F32),
        ],
        compiler_params=_cparams(("parallel",)),
        name="topk",
    )(sc)


def _peer_kernel(xn_ref, u_ref, vt_ref, e1_ref, e2_ref, tau_ref, h_ref, g_ref, o_ref,
                 acc_ref, a_ref, w_ref, *, eb):
    e = pl.program_id(1)
    tt = xn_ref.shape[0]
    inv_sqrt2 = 1.0 / math.sqrt(2.0)
    a_ref[...] = lax.dot_general(u_ref[...], xn_ref[...], NT_DIMS,
                                 preferred_element_type=F32)
    for ii in range(eb // PEER_KEYS):
        i = e * (eb // PEER_KEYS) + ii
        rows = slice(PEER_KEYS * ii, PEER_KEYS * (ii + 1))
        e1_rows = [e1_ref[h, pl.ds(i, 1), :] for h in range(PEER_HEADS)]
        for tc in range(tt // LANES):
            cols = slice(LANES * tc, LANES * (tc + 1))
            a = a_ref[rows, cols]
            act = 0.5 * a * (1.0 + lax.erf(a * inv_sqrt2))
            gate = None
            for h in range(PEER_HEADS):
                p = e2_ref[h, :, cols] * e1_rows[h][:, cols]
                term = jnp.where(p >= tau_ref[h:h + 1, cols], p, 0.0)
                gate = term if gate is None else gate + term
            w_ref[rows, cols] = (gate * act).astype(BF16)
    part = jnp.dot(vt_ref[...], w_ref[...], preferred_element_type=F32)

    @pl.when(e == 0)
    def _():
        acc_ref[...] = part

    @pl.when(e > 0)
    def _():
        acc_ref[...] += part

    @pl.when(e == pl.num_programs(1) - 1)
    def _():
        out = h_ref[...] + acc_ref[...].T
        o_ref[...] = _rms(out, g_ref[...], NORM_EPS)


def _peer_call(xn, u, vt, e1, e2, tau, h, g, tt, eb):
    seq = xn.shape[0]
    n_exp = u.shape[0]
    tok = lambda t, e: (t, 0)
    tok3 = lambda t, e: (0, 0, t)
    return pl.pallas_call(
        functools.partial(_peer_kernel, eb=eb),
        grid=(seq // tt, n_exp // eb),
        in_specs=[
            pl.BlockSpec((tt, D_MODEL), tok),
            pl.BlockSpec((eb, D_MODEL), lambda t, e: (e, 0)),
            pl.BlockSpec((D_MODEL, eb), lambda t, e: (0, e)),
            pl.BlockSpec((PEER_HEADS, PEER_KEYS, tt), tok3),
            pl.BlockSpec((PEER_HEADS, PEER_KEYS, tt), tok3),
            pl.BlockSpec((PEER_HEADS, tt), lambda t, e: (0, t)),
            pl.BlockSpec((tt, D_MODEL), tok),
            pl.BlockSpec((1, D_MODEL), lambda t, e: (0, 0)),
        ],
        out_specs=pl.BlockSpec((tt, D_MODEL), tok),
        out_shape=jax.ShapeDtypeStruct((seq, D_MODEL), F32),
        scratch_shapes=[pltpu.VMEM((D_MODEL, tt), F32), pltpu.VMEM((eb, tt), F32), pltpu.VMEM((eb, tt), BF16)],
        compiler_params=_cparams(("parallel", "arbitrary")),
        name="peer",
    )(xn, u, vt, e1, e2, tau, h, g)


def _tiles(seq):
    big = seq >= 4096
    return dict(
        tm=512 if big else 256,
        tq_a=128, tq_b=256,
        tk=1024 if big else 512,
        tt_topk=256, tt_peer=512 if big else 256, eb=2048,
    )


def kernel(x, norm_attn_g, w_in, q_norm_g, k_norm_g, lambda_q1, lambda_k1, lambda_q2, lambda_k2,
           subln_g, w_out, norm_ffn_g, w_query, sub_keys, expert_u, expert_v, norm_final_g):
    batch, seq, d = x.shape
    assert batch == 1 and d == D_MODEL and norm_attn_g.shape[0] == 1
    t = _tiles(seq)
    lambda_init = 0.8 - 0.6 * math.exp(-0.3 * 0)
    x2 = x.reshape(seq, d)

    tab = _rope_tables(seq)
    blk = np.arange(A_Q_W) // HEAD_DIM
    bd = jnp.asarray(np.where(blk[:, None] == blk[None, :], 1.0 / HEAD_DIM, 0.0), dtype=BF16)
    qg = (jnp.tile(q_norm_g[0], A_Q_HEADS) * Q_SCALE).reshape(1, A_Q_W)
    kg = jnp.tile(k_norm_g[0], A_KV_HEADS).reshape(1, A_KV_W)

    qa, ka, vat, qb, kb, vbt = _proj_call(x2, norm_attn_g, w_in[0].astype(BF16), qg, kg, tab, bd, t["tm"])
    oa = _gqa_call(qa, ka, vat, t["tq_a"], t["tk"])
    lam_vecs = jnp.concatenate([lambda_q1, lambda_k1, lambda_q2, lambda_k2], axis=0)
    ob = _diff_call(lam_vecs, qb, kb, vbt, subln_g, t["tq_b"], t["tk"], lambda_init)

    sk = sub_keys[0].reshape(2 * PEER_HEADS, PEER_KEYS, PEER_HALF).astype(BF16)
    h, xn, sc = _mid_call(oa, ob, x2, w_out[0].astype(BF16), norm_ffn_g, w_query[0].astype(BF16), sk, t["tm"])
    e1, e2, tau = _topk_call(sc, t["tt_topk"])
    out = _peer_call(xn, expert_u[0].astype(BF16), expert_v[0].T.astype(BF16), e1, e2, tau, h,
                     norm_final_g.reshape(1, d), t["tt_peer"], t["eb"])
    return out.reshape(batch, seq, d)
```

```python
import functools
import math

import jax
import jax.numpy as jnp
import numpy as np
from jax import lax
from jax.experimental import pallas as pl
from jax.experimental.pallas import tpu as pltpu

F32 = jnp.float32
BF16 = jnp.bfloat16

D_MODEL = 1024
HEAD_DIM = 64
A_Q_HEADS = 8
A_KV_HEADS = 2
A_GROUP = A_Q_HEADS // A_KV_HEADS
B_HEADS = 4
B_V_DIM = 2 * HEAD_DIM
GRID_W = 64
AXIAL_THETA = 10000.0
AXIAL_HALF = HEAD_DIM // 2
ROPE_THETA = 500000.0
ROPE_DIMS = HEAD_DIM // 4
NORM_EPS = 1e-6
SUBLN_EPS = 1e-5
A_Q_W = A_Q_HEADS * HEAD_DIM
A_KV_W = A_KV_HEADS * HEAD_DIM
B_QK_W = 2 * B_HEADS * HEAD_DIM
B_V_W = B_HEADS * B_V_DIM
IN_COLS = A_Q_W + 2 * A_KV_W + 2 * B_QK_W + B_V_W
PEER_HEADS = 8
PEER_KEYS = 128
PEER_HALF = 128
PEER_TOPK = 16
LANES = 128

NT_DIMS = (((1,), (1,)), ((), ()))
Q_SCALE = (HEAD_DIM ** -0.5) * math.log2(math.e)

VMEM_LIMIT = 56 * 1024 * 1024


def _cparams(sem):
    return pltpu.CompilerParams(dimension_semantics=sem, vmem_limit_bytes=VMEM_LIMIT)


def _rms(x, g, eps):
    return x * lax.rsqrt(jnp.mean(x * x, axis=-1, keepdims=True) + eps) * g


def _group_mean_sq(v, bd):
    v2 = v * v
    hi = v2.astype(BF16)
    lo = (v2 - hi.astype(F32)).astype(BF16)
    return (jnp.dot(hi, bd, preferred_element_type=F32)
            + jnp.dot(lo, bd, preferred_element_type=F32))


def _rot_half(v, half, group):
    width = v.shape[-1]
    lane = lax.broadcasted_iota(jnp.int32, v.shape, 1)
    fwd = pltpu.roll(v, width - half, 1)
    bwd = pltpu.roll(v, half, 1)
    return jnp.where((lane % group) < half, fwd, bwd)


def _tile4(t):
    return jnp.concatenate([t, t, t, t], axis=1)


def _proj_kernel(x_ref, g_ref, w_ref, qg_ref, kg_ref, tab_ref, bd_ref,
                 qa_ref, ka_ref, va_ref, qb_ref, kb_ref, vb_ref):
    xn = _rms(x_ref[...], g_ref[...], NORM_EPS)
    proj = jnp.dot(xn.astype(BF16), w_ref[...], preferred_element_type=F32)
    tab = tab_ref[...]
    cos_a, sin_a = tab[:, 0:128], tab[:, 128:256]
    cos_b, sin_b = tab[:, 256:384], tab[:, 384:512]
    bd = bd_ref[...]

    c0 = 0
    qa = proj[:, c0:c0 + A_Q_W]
    qa = qa * lax.rsqrt(_group_mean_sq(qa, bd) + NORM_EPS) * qg_ref[...]
    qa = qa * _tile4(cos_a) + _rot_half(qa, AXIAL_HALF // 2, AXIAL_HALF) * _tile4(sin_a)
    qa_ref[...] = qa.astype(BF16)
    c0 += A_Q_W

    ka = proj[:, c0:c0 + A_KV_W]
    ka = ka * lax.rsqrt(_group_mean_sq(ka, bd[:A_KV_W, :A_KV_W]) + NORM_EPS) * kg_ref[...]
    ka = ka * cos_a + _rot_half(ka, AXIAL_HALF // 2, AXIAL_HALF) * sin_a
    ka_ref[...] = ka.astype(BF16)
    c0 += A_KV_W

    va_ref[0] = proj[:, c0:c0 + A_KV_W].T.astype(BF16)
    c0 += A_KV_W

    qb = proj[:, c0:c0 + B_QK_W]
    qb = qb * _tile4(cos_b) + _rot_half(qb, ROPE_DIMS // 2, HEAD_DIM) * _tile4(sin_b)
    qb_ref[...] = (qb * Q_SCALE).astype(BF16)
    c0 += B_QK_W

    kb = proj[:, c0:c0 + B_QK_W]
    kb = kb * _tile4(cos_b) + _rot_half(kb, ROPE_DIMS // 2, HEAD_DIM) * _tile4(sin_b)
    kb_ref[...] = kb.astype(BF16)
    c0 += B_QK_W

    vb_ref[0] = proj[:, c0:c0 + B_V_W].T.astype(BF16)


def _rope_tables(seq):
    f32 = np.float32
    rows = seq // GRID_W
    row = np.repeat(np.arange(rows, dtype=f32), GRID_W)
    col = np.tile(np.arange(GRID_W, dtype=f32), rows)
    pos = np.arange(seq, dtype=f32)
    inv_ax = (f32(AXIAL_THETA) ** (-np.arange(0, AXIAL_HALF, 2, dtype=f32) / f32(AXIAL_HALF))).astype(f32)
    inv_p = (f32(ROPE_THETA) ** (-np.arange(0, ROPE_DIMS, 2, dtype=f32) / f32(ROPE_DIMS))).astype(f32)
    row_ang = row[:, None] * inv_ax[None, :]
    col_ang = col[:, None] * inv_ax[None, :]
    pos_ang = pos[:, None] * inv_p[None, :]
    cr, sr = np.cos(row_ang), np.sin(row_ang)
    cc, sc = np.cos(col_ang), np.sin(col_ang)
    cp, sp = np.cos(pos_ang), np.sin(pos_ang)
    rest = HEAD_DIM - ROPE_DIMS
    cos_a = np.concatenate([cr, cr, cc, cc], axis=1)
    sin_a = np.concatenate([-sr, sr, -sc, sc], axis=1)
    cos_b = np.concatenate([cp, cp, np.ones((seq, rest), f32)], axis=1)
    sin_b = np.concatenate([-sp, sp, np.zeros((seq, rest), f32)], axis=1)
    two = lambda t: np.concatenate([t, t], axis=1)
    tab = np.concatenate([two(cos_a), two(sin_a), two(cos_b), two(sin_b)], axis=1).astype(f32)
    return jnp.asarray(tab)


def _proj_call(x, g, w_in, qg, kg, tab, bd, tm):
    seq = x.shape[0]
    row = lambda i: (i, 0)
    fix = lambda i: (0, 0)
    n = seq // tm
    rows_out = lambda w: (pl.BlockSpec((tm, w), row), jax.ShapeDtypeStruct((seq, w), BF16))
    cols_out = lambda w: (pl.BlockSpec((1, w, tm), lambda i: (i, 0, 0)), jax.ShapeDtypeStruct((n, w, tm), BF16))
    outs = [rows_out(A_Q_W), rows_out(A_KV_W), cols_out(A_KV_W), rows_out(B_QK_W), rows_out(B_QK_W), cols_out(B_V_W)]
    return pl.pallas_call(
        _proj_kernel,
        grid=(seq // tm,),
        in_specs=[
            pl.BlockSpec((tm, D_MODEL), row),
            pl.BlockSpec((1, D_MODEL), fix),
            pl.BlockSpec((D_MODEL, IN_COLS), fix),
            pl.BlockSpec((1, A_Q_W), fix),
            pl.BlockSpec((1, A_KV_W), fix),
            pl.BlockSpec((tm, 512), row),
            pl.BlockSpec((A_Q_W, A_Q_W), fix),
        ],
        out_specs=[o[0] for o in outs],
        out_shape=[o[1] for o in outs],
        compiler_params=_cparams(("parallel",)),
        name="proj",
    )(x, g, w_in, qg, kg, tab, bd)


SOFTMAX_ROWS = 16
N_SLOTS = 3
STEPS_PER_TRIP = 3
ST_M, ST_ALPHA, ST_COL_MAX, ST_L = 0, 1, 2, 8


def _flash_scratch(nq, tk, dv):
    return ([pltpu.VMEM((tk, nq), F32)] * N_SLOTS + [pltpu.VMEM((tk, nq), BF16)] * N_SLOTS
            + [pltpu.VMEM((dv, nq), F32), pltpu.VMEM((16, nq), F32)])


def _flash_cols(qst, k_ref, vt_ref, scratch):
    s_ref, p_ref = scratch[:N_SLOTS], scratch[N_SLOTS:2 * N_SLOTS]
    acc_ref, st_ref = scratch[2 * N_SLOTS:]
    nq = qst.shape[1]
    tk = s_ref[0].shape[0]
    per = tk // vt_ref.shape[2]
    n_chunks = vt_ref.shape[0] // per
    assert n_chunks >= N_SLOTS and tk % SOFTMAX_ROWS == 0
    n_sub = tk // SOFTMAX_ROWS
    row = lambda r, n=1: slice(r, r + n)
    sub = lambda slot, b: s_ref[slot][b * SOFTMAX_ROWS:(b + 1) * SOFTMAX_ROWS, :]

    def scores(j, slot):
        start = j * tk
        start = start if isinstance(start, int) else pl.multiple_of(start, tk)
        s = jnp.dot(k_ref[pl.ds(start, tk), :], qst, preferred_element_type=F32)
        s_ref[slot][...] = s
        st_ref[row(ST_COL_MAX + slot), :] = jnp.max(s, axis=0, keepdims=True)

    def weighted(j, slot, alpha):
        vt = jnp.concatenate([vt_ref[per * j + c] for c in range(per)], axis=1)
        acc_ref[...] = alpha * acc_ref[...] + jnp.dot(vt, p_ref[slot][...], preferred_element_type=F32)

    def step(j, slot, with_scores=True):
        a_prev = st_ref[row(ST_ALPHA), :]
        m_old = st_ref[row(ST_M), :]
        m_new = jnp.maximum(m_old, st_ref[row(ST_COL_MAX + slot), :])
        alpha = jnp.exp2(m_old - m_new)
        if with_scores:
            scores(j + 2, (slot + 2) % N_SLOTS)
        psum = jnp.zeros((8, nq), F32)
        for b in range(n_sub):
            p = jnp.exp2(sub(slot, b) - m_new)
            psum = psum + jnp.sum(p.reshape(SOFTMAX_ROWS // 8, 8, nq), axis=0)
            p_ref[slot][b * SOFTMAX_ROWS:(b + 1) * SOFTMAX_ROWS, :] = p.astype(BF16)
        weighted(jnp.maximum(j - 1, 0), (slot + 2) % N_SLOTS, a_prev)
        st_ref[row(ST_L, 8), :] = alpha * st_ref[row(ST_L, 8), :] + psum
        st_ref[row(ST_M), :] = m_new
        st_ref[row(ST_ALPHA), :] = alpha

    st_ref[...] = jnp.zeros_like(st_ref)
    st_ref[row(ST_M), :] = jnp.full((1, nq), -jnp.inf, F32)
    acc_ref[...] = jnp.zeros_like(acc_ref)
    p_ref[N_SLOTS - 1][...] = jnp.zeros((tk, nq), BF16)
    scores(0, 0)
    scores(1, 1)
    n_full = n_chunks - 2
    n_trips = n_full // STEPS_PER_TRIP
    lead = n_full - n_trips * STEPS_PER_TRIP
    for j in range(lead):
        step(j, j % N_SLOTS)

    def body(t, carry):
        for u in range(STEPS_PER_TRIP):
            step(lead + t * STEPS_PER_TRIP + u, (lead + u) % N_SLOTS)
        return carry

    lax.fori_loop(0, n_trips, body, 0)
    step(n_chunks - 2, (n_chunks - 2) % N_SLOTS, with_scores=False)
    step(n_chunks - 1, (n_chunks - 1) % N_SLOTS, with_scores=False)
    weighted(n_chunks - 1, (n_chunks - 1) % N_SLOTS, st_ref[row(ST_ALPHA), :])
    return acc_ref[...], jnp.sum(st_ref[row(ST_L, 8), :], axis=0, keepdims=True)


def _gqa_kernel(q_ref, k_ref, vt_ref, o_ref, *scratch, tq):
    g = pl.program_id(0)
    lane = lax.broadcasted_iota(jnp.int32, (tq, LANES), 1)
    in_g = (lane // HEAD_DIM) == g
    rows = []
    for hh in range(A_GROUP):
        qp = q_ref[:, LANES * (hh // 2):LANES * (hh // 2 + 1)].astype(F32)
        aligned = jnp.where((hh % 2) == g, qp, pltpu.roll(qp, HEAD_DIM, 1))
        rows.append(jnp.where(in_g, aligned, 0.0))
    qs = jnp.concatenate(rows, axis=0)
    acc_t, l = _flash_cols(qs.T.astype(BF16), k_ref, vt_ref, scratch)
    o_t = acc_t / l
    o = jnp.concatenate([o_t, jnp.zeros_like(o_t)], axis=0).T
    left_half = lane < HEAD_DIM
    for p in range(A_GROUP // 2):
        a = o[(2 * p) * tq:(2 * p + 1) * tq]
        b = o[(2 * p + 1) * tq:(2 * p + 2) * tq]
        o_ref[:, LANES * p:LANES * (p + 1)] = jnp.where(left_half, a, pltpu.roll(b, HEAD_DIM, 1)).astype(BF16)


def _gqa_call(qa, ka, vat, tq, tk):
    seq = qa.shape[0]
    return pl.pallas_call(
        functools.partial(_gqa_kernel, tq=tq),
        grid=(A_KV_HEADS, seq // tq),
        in_specs=[
            pl.BlockSpec((tq, A_GROUP * HEAD_DIM), lambda g, i: (i, g)),
            pl.BlockSpec((seq, A_KV_W), lambda g, i: (0, 0)),
            pl.BlockSpec((vat.shape[0], HEAD_DIM, vat.shape[2]), lambda g, i: (0, g, 0)),
        ],
        out_specs=pl.BlockSpec((tq, A_GROUP * HEAD_DIM), lambda g, i: (i, g)),
        out_shape=jax.ShapeDtypeStruct((seq, A_Q_W), BF16),
        scratch_shapes=_flash_scratch(A_GROUP * tq, tk, HEAD_DIM),
        compiler_params=_cparams(("parallel", "parallel")),
        name="gqa",
    )(qa, ka, vat)


def _diff_kernel(lam_ref, q_ref, k_ref, vt_ref, sg_ref, o_ref, *scratch, tq, lambda_init):
    lane = lax.broadcasted_iota(jnp.int32, (tq, LANES), 1)
    q = q_ref[...].astype(F32)
    qs = jnp.concatenate([jnp.where(lane < HEAD_DIM, q, 0.0),
                          jnp.where(lane >= HEAD_DIM, q, 0.0)], axis=0)
    acc_t, l = _flash_cols(qs.T.astype(BF16), k_ref, vt_ref, scratch)
    o = (acc_t / l).T
    lv = lam_ref[...]
    lam = (jnp.exp(jnp.sum(lv[0:1] * lv[1:2], axis=-1, keepdims=True))
           - jnp.exp(jnp.sum(lv[2:3] * lv[3:4], axis=-1, keepdims=True)) + lambda_init)
    ob = o[:tq] - lam * o[tq:]
    ob = _rms(ob, sg_ref[...], SUBLN_EPS) * (1.0 - lambda_init)
    o_ref[...] = ob.astype(BF16)


def _diff_call(lam_vecs, qb, kb, vbt, subln_g, tq, tk, lambda_init):
    seq = qb.shape[0]
    n_slabs, _, slab = vbt.shape
    return pl.pallas_call(
        functools.partial(_diff_kernel, tq=tq, lambda_init=lambda_init),
        grid=(B_HEADS, seq // tq),
        in_specs=[
            pl.BlockSpec((4, HEAD_DIM), lambda h, i: (0, 0)),
            pl.BlockSpec((tq, LANES), lambda h, i: (i, h)),
            pl.BlockSpec((seq, LANES), lambda h, i: (0, h)),
            pl.BlockSpec((n_slabs, B_V_DIM, slab), lambda h, i: (0, h, 0)),
            pl.BlockSpec((1, B_V_DIM), lambda h, i: (0, 0)),
        ],
        out_specs=pl.BlockSpec((tq, LANES), lambda h, i: (i, h)),
        out_shape=jax.ShapeDtypeStruct((seq, B_V_W), BF16),
        scratch_shapes=_flash_scratch(2 * tq, tk, B_V_DIM),
        compiler_params=_cparams(("parallel", "parallel")),
        name="diff",
    )(lam_vecs, qb, kb, vbt, subln_g)


def _mid_kernel(oa_ref, ob_ref, x_ref, wo_ref, g_ref, wq_ref, sk_ref, h_ref, xn_ref, sc_ref):
    o = jnp.concatenate([oa_ref[...], ob_ref[...]], axis=1)
    h = x_ref[...] + jnp.dot(o, wo_ref[...], preferred_element_type=F32)
    h_ref[...] = h
    xn = _rms(h, g_ref[...], NORM_EPS).astype(BF16)
    xn_ref[...] = xn
    q = jnp.dot(xn, wq_ref[...], preferred_element_type=F32).astype(BF16)
    for hp in range(2 * PEER_HEADS):
        sc_ref[hp] = lax.dot_general(sk_ref[hp], q[:, PEER_HALF * hp:PEER_HALF * (hp + 1)],
                                     NT_DIMS, preferred_element_type=F32)


def _mid_call(oa, ob, x, w_out, g, w_query, sub_keys, tm):
    seq = x.shape[0]
    row = lambda i: (i, 0)
    fix = lambda i: (0, 0)
    nq = 2 * PEER_HEADS * PEER_HALF
    return pl.pallas_call(
        _mid_kernel,
        grid=(seq // tm,),
        in_specs=[
            pl.BlockSpec((tm, A_Q_W), row),
            pl.BlockSpec((tm, B_V_W), row),
            pl.BlockSpec((tm, D_MODEL), row),
            pl.BlockSpec((D_MODEL, D_MODEL), fix),
            pl.BlockSpec((1, D_MODEL), fix),
            pl.BlockSpec((D_MODEL, nq), fix),
            pl.BlockSpec((2 * PEER_HEADS, PEER_KEYS, PEER_HALF), lambda i: (0, 0, 0)),
        ],
        out_specs=[
            pl.BlockSpec((tm, D_MODEL), row),
            pl.BlockSpec((tm, D_MODEL), row),
            pl.BlockSpec((2 * PEER_HEADS, PEER_KEYS, tm), lambda i: (0, 0, i)),
        ],
        out_shape=[
            jax.ShapeDtypeStruct((seq, D_MODEL), F32),
            jax.ShapeDtypeStruct((seq, D_MODEL), BF16),
            jax.ShapeDtypeStruct((2 * PEER_HEADS, PEER_KEYS, seq), F32),
        ],
        compiler_params=_cparams(("parallel",)),
        name="mid",
    )(oa, ob, x, w_out, g, w_query, sub_keys)


N_TOP = PEER_TOPK + 1
_CAND = [(a, b) for a in range(N_TOP) for b in range(N_TOP) if (a + 1) * (b + 1) <= N_TOP]


def _top_vals(s, n):
    vals = []
    for r in range(n):
        m = jnp.max(s, axis=0, keepdims=True)
        vals.append(m)
        if r + 1 < n:
            s = jnp.where(s >= m, -jnp.inf, s)
    return vals


def _topk_kernel(sc_ref, e1_ref, e2_ref, tau_ref):
    t = sc_ref.shape[-1]
    taus = []
    for h in range(PEER_HEADS):
        s1 = sc_ref[2 * h]
        s2 = sc_ref[2 * h + 1]
        v1 = _top_vals(s1, N_TOP)
        v2 = _top_vals(s2, N_TOP)
        cands = [v1[a] + v2[b] for (a, b) in _CAND]
        pad = (-len(cands)) % 8
        cands += [jnp.full((1, t), -jnp.inf, F32)] * pad
        c = jnp.concatenate(cands, axis=0)
        top = _top_vals(c, N_TOP)
        t16, t17 = top[PEER_TOPK - 1], top[PEER_TOPK]
        best = v1[0] + v2[0]
        z = jnp.sum(jnp.where(c >= t16, jnp.exp(c - best), 0.0), axis=0, keepdims=True)
        rz = 1.0 / z
        e1_ref[h] = jnp.exp(s1 - v1[0]) * rz
        e2_ref[h] = jnp.exp(s2 - v2[0])
        taus.append(0.5 * (jnp.exp(t16 - best) + jnp.exp(t17 - best)) * rz)
    tau_ref[...] = jnp.concatenate(taus, axis=0)


def _topk_call(sc, tt):
    seq = sc.shape[-1]
    blk = lambda i: (0, 0, i)
    return pl.pallas_call(
        _topk_kernel,
        grid=(seq // tt,),
        in_specs=[pl.BlockSpec((2 * PEER_HEADS, PEER_KEYS, tt), blk)],
        out_specs=[
            pl.BlockSpec((PEER_HEADS, PEER_KEYS, tt), blk),
            pl.BlockSpec((PEER_HEADS, PEER_KEYS, tt), blk),
            pl.BlockSpec((PEER_HEADS, tt), lambda i: (0, i)),
        ],
        out_shape=[
            jax.ShapeDtypeStruct((PEER_HEADS, PEER_KEYS, seq), F32),
            jax.ShapeDtypeStruct((PEER_HEADS, PEER_KEYS, seq), F32),
            jax.ShapeDtypeStruct((PEER_HEADS, seq), F32),
        ],
        compiler_params=_cparams(("parallel",)),
        name="topk",
    )(sc)


def _peer_kernel(xn_ref, u_ref, vt_ref, e1_ref, e2_ref, tau_ref, h_ref, g_ref, o_ref,
                 acc_ref, a_ref, w_ref, *, eb):
    e = pl.program_id(1)
    tt = xn_ref.shape[0]
    inv_sqrt2 = 1.0 / math.sqrt(2.0)
    a_ref[...] = lax.dot_general(u_ref[...], xn_ref[...], NT_DIMS,
                                 preferred_element_type=F32)
    for ii in range(eb // PEER_KEYS):
        i = e * (eb // PEER_KEYS) + ii
        rows = slice(PEER_KEYS * ii, PEER_KEYS * (ii + 1))
        e1_rows = [e1_ref[h, pl.ds(i, 1), :] for h in range(PEER_HEADS)]
        for tc in range(tt // LANES):
            cols = slice(LANES * tc, LANES * (tc + 1))
            a = a_ref[rows, cols]
            act = 0.5 * a * (1.0 + lax.erf(a * inv_sqrt2))
            gate = None
            for h in range(PEER_HEADS):
                p = e2_ref[h, :, cols] * e1_rows[h][:, cols]
                term = jnp.where(p >= tau_ref[h:h + 1, cols], p, 0.0)
                gate = term if gate is None else gate + term
            w_ref[rows, cols] = (gate * act).astype(BF16)
    part = jnp.dot(vt_ref[...], w_ref[...], preferred_element_type=F32)

    @pl.when(e == 0)
    def _():
        acc_ref[...] = part

    @pl.when(e > 0)
    def _():
        acc_ref[...] += part

    @pl.when(e == pl.num_programs(1) - 1)
    def _():
        out = h_ref[...] + acc_ref[...].T
        o_ref[...] = _rms(out, g_ref[...], NORM_EPS)


def _peer_call(xn, u, vt, e1, e2, tau, h, g, tt, eb):
    seq = xn.shape[0]
    n_exp = u.shape[0]
    tok = lambda t, e: (t, 0)
    tok3 = lambda t, e: (0, 0, t)
    return pl.pallas_call(
        functools.partial(_peer_kernel, eb=eb),
        grid=(seq // tt, n_exp // eb),
        in_specs=[
            pl.BlockSpec((tt, D_MODEL), tok),
            pl.BlockSpec((eb, D_MODEL), lambda t, e: (e, 0)),
            pl.BlockSpec((D_MODEL, eb), lambda t, e: (0, e)),
            pl.BlockSpec((PEER_HEADS, PEER_KEYS, tt), tok3),
            pl.BlockSpec((PEER_HEADS, PEER_KEYS, tt), tok3),
            pl.BlockSpec((PEER_HEADS, tt), lambda t, e: (0, t)),
            pl.BlockSpec((tt, D_MODEL), tok),
            pl.BlockSpec((1, D_MODEL), lambda t, e: (0, 0)),
        ],
        out_specs=pl.BlockSpec((tt, D_MODEL), tok),
        out_shape=jax.ShapeDtypeStruct((seq, D_MODEL), F32),
        scratch_shapes=[pltpu.VMEM((D_MODEL, tt), F32), pltpu.VMEM((eb, tt), F32), pltpu.VMEM((eb, tt), BF16)],
        compiler_params=_cparams(("parallel", "arbitrary")),
        name="peer",
    )(xn, u, vt, e1, e2, tau, h, g)


def _tiles(seq):
    big = seq >= 4096
    return dict(
        tm=512 if big else 256,
        tq_a=128, tq_b=256,
        tk=1024 if big else 512,
        tt_topk=256, tt_peer=512 if big else 256, eb=2048,
    )


def kernel(x, norm_attn_g, w_in, q_norm_g, k_norm_g, lambda_q1, lambda_k1, lambda_q2, lambda_k2,
           subln_g, w_out, norm_ffn_g, w_query, sub_keys, expert_u, expert_v, norm_final_g):
    batch, seq, d = x.shape
    assert batch == 1 and d == D_MODEL and norm_attn_g.shape[0] == 1
    t = _tiles(seq)
    lambda_init = 0.8 - 0.6 * math.exp(-0.3 * 0)
    x2 = x.reshape(seq, d)

    tab = _rope_tables(seq)
    blk = np.arange(A_Q_W) // HEAD_DIM
    bd = jnp.asarray(np.where(blk[:, None] == blk[None, :], 1.0 / HEAD_DIM, 0.0), dtype=BF16)
    qg = (jnp.tile(q_norm_g[0], A_Q_HEADS) * Q_SCALE).reshape(1, A_Q_W)
    kg = jnp.tile(k_norm_g[0], A_KV_HEADS).reshape(1, A_KV_W)

    qa, ka, vat, qb, kb, vbt = _proj_call(x2, norm_attn_g, w_in[0].astype(BF16), qg, kg, tab, bd, t["tm"])
    oa = _gqa_call(qa, ka, vat, t["tq_a"], t["tk"])
    lam_vecs = jnp.concatenate([lambda_q1, lambda_k1, lambda_q2, lambda_k2], axis=0)
    ob = _diff_call(lam_vecs, qb, kb, vbt, subln_g, t["tq_b"], t["tk"], lambda_init)

    sk = sub_keys[0].reshape(2 * PEER_HEADS, PEER_KEYS, PEER_HALF).astype(BF16)
    h, xn, sc = _mid_call(oa, ob, x2, w_out[0].astype(BF16), norm_ffn_g, w_query[0].astype(BF16), sk, t["tm"])
    e1, e2, tau = _topk_call(sc, t["tt_topk"])
    out = _peer_call(xn, expert_u[0].astype(BF16), expert_v[0].T.astype(BF16), e1, e2, tau, h,
                     norm_final_g.reshape(1, d), t["tt_peer"], t["eb"])
    return out.reshape(batch, seq, d)
```

```python
import functools
import math

import jax
import jax.numpy as jnp
import numpy as np
from jax import lax
from jax.experimental import pallas as pl
from jax.experimental.pallas import tpu as pltpu

F32 = jnp.float32
BF16 = jnp.bfloat16

D_MODEL = 1024
HEAD_DIM = 64
A_Q_HEADS = 8
A_KV_HEADS = 2
A_GROUP = A_Q_HEADS // A_KV_HEADS
B_HEADS = 4
B_V_DIM = 2 * HEAD_DIM
GRID_W = 64
AXIAL_THETA = 10000.0
AXIAL_HALF = HEAD_DIM // 2
ROPE_THETA = 500000.0
ROPE_DIMS = HEAD_DIM // 4
NORM_EPS = 1e-6
SUBLN_EPS = 1e-5
A_Q_W = A_Q_HEADS * HEAD_DIM
A_KV_W = A_KV_HEADS * HEAD_DIM
B_QK_W = 2 * B_HEADS * HEAD_DIM
B_V_W = B_HEADS * B_V_DIM
IN_COLS = A_Q_W + 2 * A_KV_W + 2 * B_QK_W + B_V_W
PEER_HEADS = 8
PEER_KEYS = 128
PEER_HALF = 128
PEER_TOPK = 16
LANES = 128

NT_DIMS = (((1,), (1,)), ((), ()))
Q_SCALE = (HEAD_DIM ** -0.5) * math.log2(math.e)

VMEM_LIMIT = 56 * 1024 * 1024


def _cparams(sem):
    return pltpu.CompilerParams(dimension_semantics=sem, vmem_limit_bytes=VMEM_LIMIT)


def _rms(x, g, eps):
    return x * lax.rsqrt(jnp.mean(x * x, axis=-1, keepdims=True) + eps) * g


def _group_mean_sq(v, bd):
    v2 = v * v
    hi = v2.astype(BF16)
    lo = (v2 - hi.astype(F32)).astype(BF16)
    return (jnp.dot(hi, bd, preferred_element_type=F32)
            + jnp.dot(lo, bd, preferred_element_type=F32))


def _rot_half(v, half, group):
    width = v.shape[-1]
    lane = lax.broadcasted_iota(jnp.int32, v.shape, 1)
    fwd = pltpu.roll(v, width - half, 1)
    bwd = pltpu.roll(v, half, 1)
    return jnp.where((lane % group) < half, fwd, bwd)


def _tile4(t):
    return jnp.concatenate([t, t, t, t], axis=1)


def _proj_kernel(x_ref, g_ref, w_ref, qg_ref, kg_ref, tab_ref, bd_ref,
                 qa_ref, ka_ref, va_ref, qb_ref, kb_ref, vb_ref):
    xn = _rms(x_ref[...], g_ref[...], NORM_EPS)
    proj = jnp.dot(xn.astype(BF16), w_ref[...], preferred_element_type=F32)
    tab = tab_ref[...]
    cos_a, sin_a = tab[:, 0:128], tab[:, 128:256]
    cos_b, sin_b = tab[:, 256:384], tab[:, 384:512]
    bd = bd_ref[...]

    c0 = 0
    qa = proj[:, c0:c0 + A_Q_W]
    qa = qa * lax.rsqrt(_group_mean_sq(qa, bd) + NORM_EPS) * qg_ref[...]
    qa = qa * _tile4(cos_a) + _rot_half(qa, AXIAL_HALF // 2, AXIAL_HALF) * _tile4(sin_a)
    qa_ref[...] = qa.astype(BF16)
    c0 += A_Q_W

    ka = proj[:, c0:c0 + A_KV_W]
    ka = ka * lax.rsqrt(_group_mean_sq(ka, bd[:A_KV_W, :A_KV_W]) + NORM_EPS) * kg_ref[...]
    ka = ka * cos_a + _rot_half(ka, AXIAL_HALF // 2, AXIAL_HALF) * sin_a
    ka_ref[...] = ka.astype(BF16)
    c0 += A_KV_W

    va_ref[0] = proj[:, c0:c0 + A_KV_W].T.astype(BF16)
    c0 += A_KV_W

    qb = proj[:, c0:c0 + B_QK_W]
    qb = qb * _tile4(cos_b) + _rot_half(qb, ROPE_DIMS // 2, HEAD_DIM) * _tile4(sin_b)
    qb_ref[...] = (qb * Q_SCALE).astype(BF16)
    c0 += B_QK_W

    kb = proj[:, c0:c0 + B_QK_W]
    kb = kb * _tile4(cos_b) + _rot_half(kb, ROPE_DIMS // 2, HEAD_DIM) * _tile4(sin_b)
    kb_ref[...] = kb.astype(BF16)
    c0 += B_QK_W

    vb_ref[0] = proj[:, c0:c0 + B_V_W].T.astype(BF16)


def _rope_tables(seq):
    f32 = np.float32
    rows = seq // GRID_W
    row = np.repeat(np.arange(rows, dtype=f32), GRID_W)
    col = np.tile(np.arange(GRID_W, dtype=f32), rows)
    pos = np.arange(seq, dtype=f32)
    inv_ax = (f32(AXIAL_THETA) ** (-np.arange(0, AXIAL_HALF, 2, dtype=f32) / f32(AXIAL_HALF))).astype(f32)
    inv_p = (f32(ROPE_THETA) ** (-np.arange(0, ROPE_DIMS, 2, dtype=f32) / f32(ROPE_DIMS))).astype(f32)
    row_ang = row[:, None] * inv_ax[None, :]
    col_ang = col[:, None] * inv_ax[None, :]
    pos_ang = pos[:, None] * inv_p[None, :]
    cr, sr = np.cos(row_ang), np.sin(row_ang)
    cc, sc = np.cos(col_ang), np.sin(col_ang)
    cp, sp = np.cos(pos_ang), np.sin(pos_ang)
    rest = HEAD_DIM - ROPE_DIMS
    cos_a = np.concatenate([cr, cr, cc, cc], axis=1)
    sin_a = np.concatenate([-sr, sr, -sc, sc], axis=1)
    cos_b = np.concatenate([cp, cp, np.ones((seq, rest), f32)], axis=1)
    sin_b = np.concatenate([-sp, sp, np.zeros((seq, rest), f32)], axis=1)
    two = lambda t: np.concatenate([t, t], axis=1)
    tab = np.concatenate([two(cos_a), two(sin_a), two(cos_b), two(sin_b)], axis=1).astype(f32)
    return jnp.asarray(tab)


def _proj_call(x, g, w_in, qg, kg, tab, bd, tm):
    seq = x.shape[0]
    row = lambda i: (i, 0)
    fix = lambda i: (0, 0)
    n = seq // tm
    rows_out = lambda w: (pl.BlockSpec((tm, w), row), jax.ShapeDtypeStruct((seq, w), BF16))
    cols_out = lambda w: (pl.BlockSpec((1, w, tm), lambda i: (i, 0, 0)), jax.ShapeDtypeStruct((n, w, tm), BF16))
    outs = [rows_out(A_Q_W), rows_out(A_KV_W), cols_out(A_KV_W), rows_out(B_QK_W), rows_out(B_QK_W), cols_out(B_V_W)]
    return pl.pallas_call(
        _proj_kernel,
        grid=(seq // tm,),
        in_specs=[
            pl.BlockSpec((tm, D_MODEL), row),
            pl.BlockSpec((1, D_MODEL), fix),
            pl.BlockSpec((D_MODEL, IN_COLS), fix),
            pl.BlockSpec((1, A_Q_W), fix),
            pl.BlockSpec((1, A_KV_W), fix),
            pl.BlockSpec((tm, 512), row),
            pl.BlockSpec((A_Q_W, A_Q_W), fix),
        ],
        out_specs=[o[0] for o in outs],
        out_shape=[o[1] for o in outs],
        compiler_params=_cparams(("parallel",)),
        name="proj",
    )(x, g, w_in, qg, kg, tab, bd)


SOFTMAX_ROWS = 16
N_SLOTS = 3
STEPS_PER_TRIP = 3
ST_M, ST_ALPHA, ST_NEXT_MAX, ST_L = 0, 1, 2, 8


def _flash_scratch(nq, tk, dv):
    return ([pltpu.VMEM((tk, nq), F32)] * N_SLOTS + [pltpu.VMEM((tk, nq), BF16)] * N_SLOTS
            + [pltpu.VMEM((dv, nq), F32), pltpu.VMEM((16, nq), F32)])


def _flash_cols(qst, k_ref, vt_ref, scratch):
    s_ref, p_ref = scratch[:N_SLOTS], scratch[N_SLOTS:2 * N_SLOTS]
    acc_ref, st_ref = scratch[2 * N_SLOTS:]
    nq = qst.shape[1]
    tk = s_ref[0].shape[0]
    per = tk // vt_ref.shape[2]
    n_chunks = vt_ref.shape[0] // per
    assert n_chunks >= N_SLOTS and tk % SOFTMAX_ROWS == 0
    n_sub = tk // SOFTMAX_ROWS
    row = lambda r, n=1: slice(r, r + n)
    sub = lambda slot, b: s_ref[slot][b * SOFTMAX_ROWS:(b + 1) * SOFTMAX_ROWS, :]

    def scores(j, slot):
        start = j * tk
        start = start if isinstance(start, int) else pl.multiple_of(start, tk)
        s_ref[slot][...] = jnp.dot(k_ref[pl.ds(start, tk), :], qst, preferred_element_type=F32)

    def column_max(slot):
        mx = sub(slot, 0)
        for b in range(1, n_sub):
            mx = jnp.maximum(mx, sub(slot, b))
        st_ref[row(ST_NEXT_MAX), :] = jnp.max(mx, axis=0, keepdims=True)

    def weighted(j, slot, alpha):
        vt = jnp.concatenate([vt_ref[per * j + c] for c in range(per)], axis=1)
        acc_ref[...] = alpha * acc_ref[...] + jnp.dot(vt, p_ref[slot][...], preferred_element_type=F32)

    def step(j, slot, with_scores=True, with_max=True):
        a_prev = st_ref[row(ST_ALPHA), :]
        m_old = st_ref[row(ST_M), :]
        m_new = jnp.maximum(m_old, st_ref[row(ST_NEXT_MAX), :])
        alpha = jnp.exp2(m_old - m_new)
        if with_scores:
            scores(j + 2, (slot + 2) % N_SLOTS)
        psum = jnp.zeros((8, nq), F32)
        for b in range(n_sub):
            p = jnp.exp2(sub(slot, b) - m_new)
            psum = psum + jnp.sum(p.reshape(SOFTMAX_ROWS // 8, 8, nq), axis=0)
            p_ref[slot][b * SOFTMAX_ROWS:(b + 1) * SOFTMAX_ROWS, :] = p.astype(BF16)
        weighted(jnp.maximum(j - 1, 0), (slot + 2) % N_SLOTS, a_prev)
        st_ref[row(ST_L, 8), :] = alpha * st_ref[row(ST_L, 8), :] + psum
        st_ref[row(ST_M), :] = m_new
        st_ref[row(ST_ALPHA), :] = alpha
        if with_max:
            column_max((slot + 1) % N_SLOTS)

    scores(0, 0)
    scores(1, 1)
    st_ref[...] = jnp.zeros_like(st_ref)
    st_ref[row(ST_M), :] = jnp.full((1, nq), -jnp.inf, F32)
    acc_ref[...] = jnp.zeros_like(acc_ref)
    p_ref[N_SLOTS - 1][...] = jnp.zeros((tk, nq), BF16)
    column_max(0)
    n_full = n_chunks - 2
    n_trips = n_full // STEPS_PER_TRIP
    lead = n_full - n_trips * STEPS_PER_TRIP
    for j in range(lead):
        step(j, j % N_SLOTS)

    def body(t, carry):
        for u in range(STEPS_PER_TRIP):
            step(lead + t * STEPS_PER_TRIP + u, (lead + u) % N_SLOTS)
        return carry

    lax.fori_loop(0, n_trips, body, 0)
    step(n_chunks - 2, (n_chunks - 2) % N_SLOTS, with_scores=False)
    step(n_chunks - 1, (n_chunks - 1) % N_SLOTS, with_scores=False, with_max=False)
    weighted(n_chunks - 1, (n_chunks - 1) % N_SLOTS, st_ref[row(ST_ALPHA), :])
    return acc_ref[...], jnp.sum(st_ref[row(ST_L, 8), :], axis=0, keepdims=True)


def _gqa_kernel(q_ref, k_ref, vt_ref, o_ref, *scratch, tq):
    g = pl.program_id(0)
    lane = lax.broadcasted_iota(jnp.int32, (tq, LANES), 1)
    in_g = (lane // HEAD_DIM) == g
    rows = []
    for hh in range(A_GROUP):
        qp = q_ref[:, LANES * (hh // 2):LANES * (hh // 2 + 1)].astype(F32)
        aligned = jnp.where((hh % 2) == g, qp, pltpu.roll(qp, HEAD_DIM, 1))
        rows.append(jnp.where(in_g, aligned, 0.0))
    qs = jnp.concatenate(rows, axis=0)
    acc_t, l = _flash_cols(qs.T.astype(BF16), k_ref, vt_ref, scratch)
    o_t = acc_t / l
    o = jnp.concatenate([o_t, jnp.zeros_like(o_t)], axis=0).T
    left_half = lane < HEAD_DIM
    for p in range(A_GROUP // 2):
        a = o[(2 * p) * tq:(2 * p + 1) * tq]
        b = o[(2 * p + 1) * tq:(2 * p + 2) * tq]
        o_ref[:, LANES * p:LANES * (p + 1)] = jnp.where(left_half, a, pltpu.roll(b, HEAD_DIM, 1)).astype(BF16)


def _gqa_call(qa, ka, vat, tq, tk):
    seq = qa.shape[0]
    return pl.pallas_call(
        functools.partial(_gqa_kernel, tq=tq),
        grid=(A_KV_HEADS, seq // tq),
        in_specs=[
            pl.BlockSpec((tq, A_GROUP * HEAD_DIM), lambda g, i: (i, g)),
            pl.BlockSpec((seq, A_KV_W), lambda g, i: (0, 0)),
            pl.BlockSpec((vat.shape[0], HEAD_DIM, vat.shape[2]), lambda g, i: (0, g, 0)),
        ],
        out_specs=pl.BlockSpec((tq, A_GROUP * HEAD_DIM), lambda g, i: (i, g)),
        out_shape=jax.ShapeDtypeStruct((seq, A_Q_W), BF16),
        scratch_shapes=_flash_scratch(A_GROUP * tq, tk, HEAD_DIM),
        compiler_params=_cparams(("parallel", "parallel")),
        name="gqa",
    )(qa, ka, vat)


def _diff_kernel(lam_ref, q_ref, k_ref, vt_ref, sg_ref, o_ref, *scratch, tq, lambda_init):
    lane = lax.broadcasted_iota(jnp.int32, (tq, LANES), 1)
    q = q_ref[...].astype(F32)
    qs = jnp.concatenate([jnp.where(lane < HEAD_DIM, q, 0.0),
                          jnp.where(lane >= HEAD_DIM, q, 0.0)], axis=0)
    acc_t, l = _flash_cols(qs.T.astype(BF16), k_ref, vt_ref, scratch)
    o = (acc_t / l).T
    lv = lam_ref[...]
    lam = (jnp.exp(jnp.sum(lv[0:1] * lv[1:2], axis=-1, keepdims=True))
           - jnp.exp(jnp.sum(lv[2:3] * lv[3:4], axis=-1, keepdims=True)) + lambda_init)
    ob = o[:tq] - lam * o[tq:]
    ob = _rms(ob, sg_ref[...], SUBLN_EPS) * (1.0 - lambda_init)
    o_ref[...] = ob.astype(BF16)


def _diff_call(lam_vecs, qb, kb, vbt, subln_g, tq, tk, lambda_init):
    seq = qb.shape[0]
    n_slabs, _, slab = vbt.shape
    return pl.pallas_call(
        functools.partial(_diff_kernel, tq=tq, lambda_init=lambda_init),
        grid=(B_HEADS, seq // tq),
        in_specs=[
            pl.BlockSpec((4, HEAD_DIM), lambda h, i: (0, 0)),
            pl.BlockSpec((tq, LANES), lambda h, i: (i, h)),
            pl.BlockSpec((seq, LANES), lambda h, i: (0, h)),
            pl.BlockSpec((n_slabs, B_V_DIM, slab), lambda h, i: (0, h, 0)),
            pl.BlockSpec((1, B_V_DIM), lambda h, i: (0, 0)),
        ],
        out_specs=pl.BlockSpec((tq, LANES), lambda h, i: (i, h)),
        out_shape=jax.ShapeDtypeStruct((seq, B_V_W), BF16),
        scratch_shapes=_flash_scratch(2 * tq, tk, B_V_DIM),
        compiler_params=_cparams(("parallel", "parallel")),
        name="diff",
    )(lam_vecs, qb, kb, vbt, subln_g)


def _mid_kernel(oa_ref, ob_ref, x_ref, wo_ref, g_ref, wq_ref, sk_ref, h_ref, xn_ref, sc_ref):
    o = jnp.concatenate([oa_ref[...], ob_ref[...]], axis=1)
    h = x_ref[...] + jnp.dot(o, wo_ref[...], preferred_element_type=F32)
    h_ref[...] = h
    xn = _rms(h, g_ref[...], NORM_EPS).astype(BF16)
    xn_ref[...] = xn
    q = jnp.dot(xn, wq_ref[...], preferred_element_type=F32).astype(BF16)
    for hp in range(2 * PEER_HEADS):
        sc_ref[hp] = lax.dot_general(sk_ref[hp], q[:, PEER_HALF * hp:PEER_HALF * (hp + 1)],
                                     NT_DIMS, preferred_element_type=F32)


def _mid_call(oa, ob, x, w_out, g, w_query, sub_keys, tm):
    seq = x.shape[0]
    row = lambda i: (i, 0)
    fix = lambda i: (0, 0)
    nq = 2 * PEER_HEADS * PEER_HALF
    return pl.pallas_call(
        _mid_kernel,
        grid=(seq // tm,),
        in_specs=[
            pl.BlockSpec((tm, A_Q_W), row),
            pl.BlockSpec((tm, B_V_W), row),
            pl.BlockSpec((tm, D_MODEL), row),
            pl.BlockSpec((D_MODEL, D_MODEL), fix),
            pl.BlockSpec((1, D_MODEL), fix),
            pl.BlockSpec((D_MODEL, nq), fix),
            pl.BlockSpec((2 * PEER_HEADS, PEER_KEYS, PEER_HALF), lambda i: (0, 0, 0)),
        ],
        out_specs=[
            pl.BlockSpec((tm, D_MODEL), row),
            pl.BlockSpec((tm, D_MODEL), row),
            pl.BlockSpec((2 * PEER_HEADS, PEER_KEYS, tm), lambda i: (0, 0, i)),
        ],
        out_shape=[
            jax.ShapeDtypeStruct((seq, D_MODEL), F32),
            jax.ShapeDtypeStruct((seq, D_MODEL), BF16),
            jax.ShapeDtypeStruct((2 * PEER_HEADS, PEER_KEYS, seq), F32),
        ],
        compiler_params=_cparams(("parallel",)),
        name="mid",
    )(oa, ob, x, w_out, g, w_query, sub_keys)


N_TOP = PEER_TOPK + 1
_CAND = [(a, b) for a in range(N_TOP) for b in range(N_TOP) if (a + 1) * (b + 1) <= N_TOP]


def _top_vals(s, n):
    vals = []
    for r in range(n):
        m = jnp.max(s, axis=0, keepdims=True)
        vals.append(m)
        if r + 1 < n:
            s = jnp.where(s >= m, -jnp.inf, s)
    return vals


def _topk_kernel(sc_ref, e1_ref, e2_ref, tau_ref):
    t = sc_ref.shape[-1]
    taus = []
    for h in range(PEER_HEADS):
        s1 = sc_ref[2 * h]
        s2 = sc_ref[2 * h + 1]
        v1 = _top_vals(s1, N_TOP)
        v2 = _top_vals(s2, N_TOP)
        cands = [v1[a] + v2[b] for (a, b) in _CAND]
        pad = (-len(cands)) % 8
        cands += [jnp.full((1, t), -jnp.inf, F32)] * pad
        c = jnp.concatenate(cands, axis=0)
        top = _top_vals(c, N_TOP)
        t16, t17 = top[PEER_TOPK - 1], top[PEER_TOPK]
        best = v1[0] + v2[0]
        z = jnp.sum(jnp.where(c >= t16, jnp.exp(c - best), 0.0), axis=0, keepdims=True)
        rz = 1.0 / z
        e1_ref[h] = jnp.exp(s1 - v1[0]) * rz
        e2_ref[h] = jnp.exp(s2 - v2[0])
        taus.append(0.5 * (jnp.exp(t16 - best) + jnp.exp(t17 - best)) * rz)
    tau_ref[...] = jnp.concatenate(taus, axis=0)


def _topk_call(sc, tt):
    seq = sc.shape[-1]
    blk = lambda i: (0, 0, i)
    return pl.pallas_call(
        _topk_kernel,
        grid=(seq // tt,),
        in_specs=[pl.BlockSpec((2 * PEER_HEADS, PEER_KEYS, tt), blk)],
        out_specs=[
            pl.BlockSpec((PEER_HEADS, PEER_KEYS, tt), blk),
            pl.BlockSpec((PEER_HEADS, PEER_KEYS, tt), blk),
            pl.BlockSpec((PEER_HEADS, tt), lambda i: (0, i)),
        ],
        out_shape=[
            jax.ShapeDtypeStruct((PEER_HEADS, PEER_KEYS, seq), F32),
            jax.ShapeDtypeStruct((PEER_HEADS, PEER_KEYS, seq), F32),
            jax.ShapeDtypeStruct((PEER_HEADS, seq), F32),
        ],
        compiler_params=_cparams(("parallel",)),
        name="topk",
    )(sc)


def _peer_kernel(xn_ref, u_ref, vt_ref, e1_ref, e2_ref, tau_ref, h_ref, g_ref, o_ref,
                 acc_ref, a_ref, w_ref, *, eb):
    e = pl.program_id(1)
    tt = xn_ref.shape[0]
    inv_sqrt2 = 1.0 / math.sqrt(2.0)
    a_ref[...] = lax.dot_general(u_ref[...], xn_ref[...], NT_DIMS,
                                 preferred_element_type=F32)
    for ii in range(eb // PEER_KEYS):
        i = e * (eb // PEER_KEYS) + ii
        rows = slice(PEER_KEYS * ii, PEER_KEYS * (ii + 1))
        e1_rows = [e1_ref[h, pl.ds(i, 1), :] for h in range(PEER_HEADS)]
        for tc in range(tt // LANES):
            cols = slice(LANES * tc, LANES * (tc + 1))
            a = a_ref[rows, cols]
            act = 0.5 * a * (1.0 + lax.erf(a * inv_sqrt2))
            gate = None
            for h in range(PEER_HEADS):
                p = e2_ref[h, :, cols] * e1_rows[h][:, cols]
                term = jnp.where(p >= tau_ref[h:h + 1, cols], p, 0.0)
                gate = term if gate is None else gate + term
            w_ref[rows, cols] = (gate * act).astype(BF16)
    part = jnp.dot(vt_ref[...], w_ref[...], preferred_element_type=F32)

    @pl.when(e == 0)
    def _():
        acc_ref[...] = part

    @pl.when(e > 0)
    def _():
        acc_ref[...] += part

    @pl.when(e == pl.num_programs(1) - 1)
    def _():
        out = h_ref[...] + acc_ref[...].T
        o_ref[...] = _rms(out, g_ref[...], NORM_EPS)


def _peer_call(xn, u, vt, e1, e2, tau, h, g, tt, eb):
    seq = xn.shape[0]
    n_exp = u.shape[0]
    tok = lambda t, e: (t, 0)
    tok3 = lambda t, e: (0, 0, t)
    return pl.pallas_call(
        functools.partial(_peer_kernel, eb=eb),
        grid=(seq // tt, n_exp // eb),
        in_specs=[
            pl.BlockSpec((tt, D_MODEL), tok),
            pl.BlockSpec((eb, D_MODEL), lambda t, e: (e, 0)),
            pl.BlockSpec((D_MODEL, eb), lambda t, e: (0, e)),
            pl.BlockSpec((PEER_HEADS, PEER_KEYS, tt), tok3),
            pl.BlockSpec((PEER_HEADS, PEER_KEYS, tt), tok3),
            pl.BlockSpec((PEER_HEADS, tt), lambda t, e: (0, t)),
            pl.BlockSpec((tt, D_MODEL), tok),
            pl.BlockSpec((1, D_MODEL), lambda t, e: (0, 0)),
        ],
        out_specs=pl.BlockSpec((tt, D_MODEL), tok),
        out_shape=jax.ShapeDtypeStruct((seq, D_MODEL), F32),
        scratch_shapes=[pltpu.VMEM((D_MODEL, tt), F32), pltpu.VMEM((eb, tt), F32), pltpu.VMEM((eb, tt), BF16)],
        compiler_params=_cparams(("parallel", "arbitrary")),
        name="peer",
    )(xn, u, vt, e1, e2, tau, h, g)


def _tiles(seq):
    big = seq >= 4096
    return dict(
        tm=512 if big else 256,
        tq_a=256 if big else 128, tq_b=512 if big else 256,
        tk=1024 if big else 512,
        tt_topk=256, tt_peer=512 if big else 256, eb=2048,
    )


def kernel(x, norm_attn_g, w_in, q_norm_g, k_norm_g, lambda_q1, lambda_k1, lambda_q2, lambda_k2,
           subln_g, w_out, norm_ffn_g, w_query, sub_keys, expert_u, expert_v, norm_final_g):
    batch, seq, d = x.shape
    assert batch == 1 and d == D_MODEL and norm_attn_g.shape[0] == 1
    t = _tiles(seq)
    lambda_init = 0.8 - 0.6 * math.exp(-0.3 * 0)
    x2 = x.reshape(seq, d)

    tab = _rope_tables(seq)
    blk = np.arange(A_Q_W) // HEAD_DIM
    bd = jnp.asarray(np.where(blk[:, None] == blk[None, :], 1.0 / HEAD_DIM, 0.0), dtype=BF16)
    qg = (jnp.tile(q_norm_g[0], A_Q_HEADS) * Q_SCALE).reshape(1, A_Q_W)
    kg = jnp.tile(k_norm_g[0], A_KV_HEADS).reshape(1, A_KV_W)

    qa, ka, vat, qb, kb, vbt = _proj_call(x2, norm_attn_g, w_in[0].astype(BF16), qg, kg, tab, bd, t["tm"])
    oa = _gqa_call(qa, ka, vat, t["tq_a"], t["tk"])
    lam_vecs = jnp.concatenate([lambda_q1, lambda_k1, lambda_q2, lambda_k2], axis=0)
    ob = _diff_call(lam_vecs, qb, kb, vbt, subln_g, t["tq_b"], t["tk"], lambda_init)

    sk = sub_keys[0].reshape(2 * PEER_HEADS, PEER_KEYS, PEER_HALF).astype(BF16)
    h, xn, sc = _mid_call(oa, ob, x2, w_out[0].astype(BF16), norm_ffn_g, w_query[0].astype(BF16), sk, t["tm"])
    e1, e2, tau = _topk_call(sc, t["tt_topk"])
    out = _peer_call(xn, expert_u[0].astype(BF16), expert_v[0].T.astype(BF16), e1, e2, tau, h,
                     norm_final_g.reshape(1, d), t["tt_peer"], t["eb"])
    return out.reshape(batch, seq, d)
```

```python
import functools
import math

import jax
import jax.numpy as jnp
import numpy as np
from jax import lax
from jax.experimental import pallas as pl
from jax.experimental.pallas import tpu as pltpu

F32 = jnp.float32
BF16 = jnp.bfloat16

D_MODEL = 1024
HEAD_DIM = 64
A_Q_HEADS = 8
A_KV_HEADS = 2
A_GROUP = A_Q_HEADS // A_KV_HEADS
B_HEADS = 4
B_V_DIM = 2 * HEAD_DIM
GRID_W = 64
AXIAL_THETA = 10000.0
AXIAL_HALF = HEAD_DIM // 2
ROPE_THETA = 500000.0
ROPE_DIMS = HEAD_DIM // 4
NORM_EPS = 1e-6
SUBLN_EPS = 1e-5
A_Q_W = A_Q_HEADS * HEAD_DIM
A_KV_W = A_KV_HEADS * HEAD_DIM
B_QK_W = 2 * B_HEADS * HEAD_DIM
B_V_W = B_HEADS * B_V_DIM
IN_COLS = A_Q_W + 2 * A_KV_W + 2 * B_QK_W + B_V_W
PEER_HEADS = 8
PEER_KEYS = 128
PEER_HALF = 128
PEER_TOPK = 16
LANES = 128

NT_DIMS = (((1,), (1,)), ((), ()))
Q_SCALE = (HEAD_DIM ** -0.5) * math.log2(math.e)

VMEM_LIMIT = 56 * 1024 * 1024


def _cparams(sem):
    return pltpu.CompilerParams(dimension_semantics=sem, vmem_limit_bytes=VMEM_LIMIT)


def _rms(x, g, eps):
    return x * lax.rsqrt(jnp.mean(x * x, axis=-1, keepdims=True) + eps) * g


def _group_mean_sq(v, bd):
    v2 = v * v
    hi = v2.astype(BF16)
    lo = (v2 - hi.astype(F32)).astype(BF16)
    return (jnp.dot(hi, bd, preferred_element_type=F32)
            + jnp.dot(lo, bd, preferred_element_type=F32))


def _rot_half(v, half, group):
    width = v.shape[-1]
    lane = lax.broadcasted_iota(jnp.int32, v.shape, 1)
    fwd = pltpu.roll(v, width - half, 1)
    bwd = pltpu.roll(v, half, 1)
    return jnp.where((lane % group) < half, fwd, bwd)


def _tile4(t):
    return jnp.concatenate([t, t, t, t], axis=1)


def _proj_kernel(x_ref, g_ref, w_ref, qg_ref, kg_ref, tab_ref, bd_ref,
                 qa_ref, ka_ref, va_ref, qb_ref, kb_ref, vb_ref):
    xn = _rms(x_ref[...], g_ref[...], NORM_EPS)
    proj = jnp.dot(xn.astype(BF16), w_ref[...], preferred_element_type=F32)
    tab = tab_ref[...]
    cos_a, sin_a = tab[:, 0:128], tab[:, 128:256]
    cos_b, sin_b = tab[:, 256:384], tab[:, 384:512]
    bd = bd_ref[...]

    c0 = 0
    qa = proj[:, c0:c0 + A_Q_W]
    qa = qa * lax.rsqrt(_group_mean_sq(qa, bd) + NORM_EPS) * qg_ref[...]
    qa = qa * _tile4(cos_a) + _rot_half(qa, AXIAL_HALF // 2, AXIAL_HALF) * _tile4(sin_a)
    qa_ref[...] = qa.astype(BF16)
    c0 += A_Q_W

    ka = proj[:, c0:c0 + A_KV_W]
    ka = ka * lax.rsqrt(_group_mean_sq(ka, bd[:A_KV_W, :A_KV_W]) + NORM_EPS) * kg_ref[...]
    ka = ka * cos_a + _rot_half(ka, AXIAL_HALF // 2, AXIAL_HALF) * sin_a
    ka_ref[...] = ka.astype(BF16)
    c0 += A_KV_W

    va_ref[0] = proj[:, c0:c0 + A_KV_W].T.astype(BF16)
    c0 += A_KV_W

    qb = proj[:, c0:c0 + B_QK_W]
    qb = qb * _tile4(cos_b) + _rot_half(qb, ROPE_DIMS // 2, HEAD_DIM) * _tile4(sin_b)
    qb_ref[...] = (qb * Q_SCALE).astype(BF16)
    c0 += B_QK_W

    kb = proj[:, c0:c0 + B_QK_W]
    kb = kb * _tile4(cos_b) + _rot_half(kb, ROPE_DIMS // 2, HEAD_DIM) * _tile4(sin_b)
    kb_ref[...] = kb.astype(BF16)
    c0 += B_QK_W

    vb_ref[0] = proj[:, c0:c0 + B_V_W].T.astype(BF16)


def _rope_tables(seq):
    f32 = np.float32
    rows = seq // GRID_W
    row = np.repeat(np.arange(rows, dtype=f32), GRID_W)
    col = np.tile(np.arange(GRID_W, dtype=f32), rows)
    pos = np.arange(seq, dtype=f32)
    inv_ax = (f32(AXIAL_THETA) ** (-np.arange(0, AXIAL_HALF, 2, dtype=f32) / f32(AXIAL_HALF))).astype(f32)
    inv_p = (f32(ROPE_THETA) ** (-np.arange(0, ROPE_DIMS, 2, dtype=f32) / f32(ROPE_DIMS))).astype(f32)
    row_ang = row[:, None] * inv_ax[None, :]
    col_ang = col[:, None] * inv_ax[None, :]
    pos_ang = pos[:, None] * inv_p[None, :]
    cr, sr = np.cos(row_ang), np.sin(row_ang)
    cc, sc = np.cos(col_ang), np.sin(col_ang)
    cp, sp = np.cos(pos_ang), np.sin(pos_ang)
    rest = HEAD_DIM - ROPE_DIMS
    cos_a = np.concatenate([cr, cr, cc, cc], axis=1)
    sin_a = np.concatenate([-sr, sr, -sc, sc], axis=1)
    cos_b = np.concatenate([cp, cp, np.ones((seq, rest), f32)], axis=1)
    sin_b = np.concatenate([-sp, sp, np.zeros((seq, rest), f32)], axis=1)
    two = lambda t: np.concatenate([t, t], axis=1)
    tab = np.concatenate([two(cos_a), two(sin_a), two(cos_b), two(sin_b)], axis=1).astype(f32)
    return jnp.asarray(tab)


def _proj_call(x, g, w_in, qg, kg, tab, bd, tm):
    seq = x.shape[0]
    row = lambda i: (i, 0)
    fix = lambda i: (0, 0)
    n = seq // tm
    rows_out = lambda w: (pl.BlockSpec((tm, w), row), jax.ShapeDtypeStruct((seq, w), BF16))
    cols_out = lambda w: (pl.BlockSpec((1, w, tm), lambda i: (i, 0, 0)), jax.ShapeDtypeStruct((n, w, tm), BF16))
    outs = [rows_out(A_Q_W), rows_out(A_KV_W), cols_out(A_KV_W), rows_out(B_QK_W), rows_out(B_QK_W), cols_out(B_V_W)]
    return pl.pallas_call(
        _proj_kernel,
        grid=(seq // tm,),
        in_specs=[
            pl.BlockSpec((tm, D_MODEL), row),
            pl.BlockSpec((1, D_MODEL), fix),
            pl.BlockSpec((D_MODEL, IN_COLS), fix),
            pl.BlockSpec((1, A_Q_W), fix),
            pl.BlockSpec((1, A_KV_W), fix),
            pl.BlockSpec((tm, 512), row),
            pl.BlockSpec((A_Q_W, A_Q_W), fix),
        ],
        out_specs=[o[0] for o in outs],
        out_shape=[o[1] for o in outs],
        compiler_params=_cparams(("parallel",)),
        name="proj",
    )(x, g, w_in, qg, kg, tab, bd)


SOFTMAX_ROWS = 16
N_SLOTS = 3
STEPS_PER_TRIP = 3
ST_M, ST_ALPHA, ST_NEXT_MAX, ST_L = 0, 1, 2, 8


def _flash_scratch(nq, tk, dv):
    return ([pltpu.VMEM((tk, nq), F32)] * N_SLOTS + [pltpu.VMEM((tk, nq), BF16)] * N_SLOTS
            + [pltpu.VMEM((dv, nq), F32), pltpu.VMEM((16, nq), F32)])


def _flash_cols(qst, k_ref, vt_ref, scratch):
    s_ref, p_ref = scratch[:N_SLOTS], scratch[N_SLOTS:2 * N_SLOTS]
    acc_ref, st_ref = scratch[2 * N_SLOTS:]
    nq = qst.shape[1]
    tk = s_ref[0].shape[0]
    per = tk // vt_ref.shape[2]
    n_chunks = vt_ref.shape[0] // per
    assert n_chunks >= N_SLOTS and tk % SOFTMAX_ROWS == 0
    n_sub = tk // SOFTMAX_ROWS
    row = lambda r, n=1: slice(r, r + n)
    sub = lambda slot, b: s_ref[slot][b * SOFTMAX_ROWS:(b + 1) * SOFTMAX_ROWS, :]

    def scores(j, slot):
        start = j * tk
        start = start if isinstance(start, int) else pl.multiple_of(start, tk)
        s_ref[slot][...] = jnp.dot(k_ref[pl.ds(start, tk), :], qst, preferred_element_type=F32)

    def column_max(slot):
        mx = sub(slot, 0)
        for b in range(1, n_sub):
            mx = jnp.maximum(mx, sub(slot, b))
        st_ref[row(ST_NEXT_MAX), :] = jnp.max(mx, axis=0, keepdims=True)

    def weighted(j, slot, alpha):
        vt = jnp.concatenate([vt_ref[per * j + c] for c in range(per)], axis=1)
        acc_ref[...] = alpha * acc_ref[...] + jnp.dot(vt, p_ref[slot][...], preferred_element_type=F32)

    def step(j, slot, with_scores=True, with_max=True):
        a_prev = st_ref[row(ST_ALPHA), :]
        m_old = st_ref[row(ST_M), :]
        m_new = jnp.maximum(m_old, st_ref[row(ST_NEXT_MAX), :])
        alpha = jnp.exp2(m_old - m_new)
        if with_scores:
            scores(j + 2, (slot + 2) % N_SLOTS)
        psum = jnp.zeros((8, nq), F32)
        for b in range(n_sub):
            p = jnp.exp2(sub(slot, b) - m_new)
            psum = psum + jnp.sum(p.reshape(SOFTMAX_ROWS // 8, 8, nq), axis=0)
            p_ref[slot][b * SOFTMAX_ROWS:(b + 1) * SOFTMAX_ROWS, :] = p.astype(BF16)
        weighted(jnp.maximum(j - 1, 0), (slot + 2) % N_SLOTS, a_prev)
        st_ref[row(ST_L, 8), :] = alpha * st_ref[row(ST_L, 8), :] + psum
        st_ref[row(ST_M), :] = m_new
        st_ref[row(ST_ALPHA), :] = alpha
        if with_max:
            column_max((slot + 1) % N_SLOTS)

    scores(0, 0)
    scores(1, 1)
    st_ref[...] = jnp.zeros_like(st_ref)
    st_ref[row(ST_M), :] = jnp.full((1, nq), -jnp.inf, F32)
    acc_ref[...] = jnp.zeros_like(acc_ref)
    p_ref[N_SLOTS - 1][...] = jnp.zeros((tk, nq), BF16)
    column_max(0)
    n_full = n_chunks - 2
    n_trips = n_full // STEPS_PER_TRIP
    lead = n_full - n_trips * STEPS_PER_TRIP
    for j in range(lead):
        step(j, j % N_SLOTS)

    def body(t, carry):
        for u in range(STEPS_PER_TRIP):
            step(lead + t * STEPS_PER_TRIP + u, (lead + u) % N_SLOTS)
        return carry

    lax.fori_loop(0, n_trips, body, 0)
    step(n_chunks - 2, (n_chunks - 2) % N_SLOTS, with_scores=False)
    step(n_chunks - 1, (n_chunks - 1) % N_SLOTS, with_scores=False, with_max=False)
    weighted(n_chunks - 1, (n_chunks - 1) % N_SLOTS, st_ref[row(ST_ALPHA), :])
    return acc_ref[...], jnp.sum(st_ref[row(ST_L, 8), :], axis=0, keepdims=True)


def _gqa_kernel(q_ref, k_ref, vt_ref, o_ref, *scratch, tq):
    g = pl.program_id(0)
    lane = lax.broadcasted_iota(jnp.int32, (tq, LANES), 1)
    in_g = (lane // HEAD_DIM) == g
    rows = []
    for hh in range(A_GROUP):
        qp = q_ref[:, LANES * (hh // 2):LANES * (hh // 2 + 1)].astype(F32)
        aligned = jnp.where((hh % 2) == g, qp, pltpu.roll(qp, HEAD_DIM, 1))
        rows.append(jnp.where(in_g, aligned, 0.0))
    qs = jnp.concatenate(rows, axis=0)
    acc_t, l = _flash_cols(qs.T.astype(BF16), k_ref, vt_ref, scratch)
    o_t = acc_t / l
    o = jnp.concatenate([o_t, jnp.zeros_like(o_t)], axis=0).T
    left_half = lane < HEAD_DIM
    for p in range(A_GROUP // 2):
        a = o[(2 * p) * tq:(2 * p + 1) * tq]
        b = o[(2 * p + 1) * tq:(2 * p + 2) * tq]
        o_ref[:, LANES * p:LANES * (p + 1)] = jnp.where(left_half, a, pltpu.roll(b, HEAD_DIM, 1)).astype(BF16)


def _gqa_call(qa, ka, vat, tq, tk):
    seq = qa.shape[0]
    return pl.pallas_call(
        functools.partial(_gqa_kernel, tq=tq),
        grid=(A_KV_HEADS, seq // tq),
        in_specs=[
            pl.BlockSpec((tq, A_GROUP * HEAD_DIM), lambda g, i: (i, g)),
            pl.BlockSpec((seq, A_KV_W), lambda g, i: (0, 0)),
            pl.BlockSpec((vat.shape[0], HEAD_DIM, vat.shape[2]), lambda g, i: (0, g, 0)),
        ],
        out_specs=pl.BlockSpec((tq, A_GROUP * HEAD_DIM), lambda g, i: (i, g)),
        out_shape=jax.ShapeDtypeStruct((seq, A_Q_W), BF16),
        scratch_shapes=_flash_scratch(A_GROUP * tq, tk, HEAD_DIM),
        compiler_params=_cparams(("parallel", "parallel")),
        name="gqa",
    )(qa, ka, vat)


def _diff_kernel(lam_ref, q_ref, k_ref, vt_ref, sg_ref, o_ref, *scratch, tq, lambda_init):
    lane = lax.broadcasted_iota(jnp.int32, (tq, LANES), 1)
    q = q_ref[...].astype(F32)
    qs = jnp.concatenate([jnp.where(lane < HEAD_DIM, q, 0.0),
                          jnp.where(lane >= HEAD_DIM, q, 0.0)], axis=0)
    acc_t, l = _flash_cols(qs.T.astype(BF16), k_ref, vt_ref, scratch)
    o = (acc_t / l).T
    lv = lam_ref[...]
    lam = (jnp.exp(jnp.sum(lv[0:1] * lv[1:2], axis=-1, keepdims=True))
           - jnp.exp(jnp.sum(lv[2:3] * lv[3:4], axis=-1, keepdims=True)) + lambda_init)
    ob = o[:tq] - lam * o[tq:]
    ob = _rms(ob, sg_ref[...], SUBLN_EPS) * (1.0 - lambda_init)
    o_ref[...] = ob.astype(BF16)


def _diff_call(lam_vecs, qb, kb, vbt, subln_g, tq, tk, lambda_init):
    seq = qb.shape[0]
    n_slabs, _, slab = vbt.shape
    return pl.pallas_call(
        functools.partial(_diff_kernel, tq=tq, lambda_init=lambda_init),
        grid=(B_HEADS, seq // tq),
        in_specs=[
            pl.BlockSpec((4, HEAD_DIM), lambda h, i: (0, 0)),
            pl.BlockSpec((tq, LANES), lambda h, i: (i, h)),
            pl.BlockSpec((seq, LANES), lambda h, i: (0, h)),
            pl.BlockSpec((n_slabs, B_V_DIM, slab), lambda h, i: (0, h, 0)),
            pl.BlockSpec((1, B_V_DIM), lambda h, i: (0, 0)),
        ],
        out_specs=pl.BlockSpec((tq, LANES), lambda h, i: (i, h)),
        out_shape=jax.ShapeDtypeStruct((seq, B_V_W), BF16),
        scratch_shapes=_flash_scratch(2 * tq, tk, B_V_DIM),
        compiler_params=_cparams(("parallel", "parallel")),
        name="diff",
    )(lam_vecs, qb, kb, vbt, subln_g)


def _mid_kernel(oa_ref, ob_ref, x_ref, wo_ref, g_ref, wq_ref, sk_ref, h_ref, xn_ref, sc_ref):
    o = jnp.concatenate([oa_ref[...], ob_ref[...]], axis=1)
    h = x_ref[...] + jnp.dot(o, wo_ref[...], preferred_element_type=F32)
    h_ref[...] = h
    xn = _rms(h, g_ref[...], NORM_EPS).astype(BF16)
    xn_ref[...] = xn
    q = jnp.dot(xn, wq_ref[...], preferred_element_type=F32).astype(BF16)
    for hp in range(2 * PEER_HEADS):
        sc_ref[hp] = lax.dot_general(sk_ref[hp], q[:, PEER_HALF * hp:PEER_HALF * (hp + 1)],
                                     NT_DIMS, preferred_element_type=F32)


def _mid_call(oa, ob, x, w_out, g, w_query, sub_keys, tm):
    seq = x.shape[0]
    row = lambda i: (i, 0)
    fix = lambda i: (0, 0)
    nq = 2 * PEER_HEADS * PEER_HALF
    return pl.pallas_call(
        _mid_kernel,
        grid=(seq // tm,),
        in_specs=[
            pl.BlockSpec((tm, A_Q_W), row),
            pl.BlockSpec((tm, B_V_W), row),
            pl.BlockSpec((tm, D_MODEL), row),
            pl.BlockSpec((D_MODEL, D_MODEL), fix),
            pl.BlockSpec((1, D_MODEL), fix),
            pl.BlockSpec((D_MODEL, nq), fix),
            pl.BlockSpec((2 * PEER_HEADS, PEER_KEYS, PEER_HALF), lambda i: (0, 0, 0)),
        ],
        out_specs=[
            pl.BlockSpec((tm, D_MODEL), row),
            pl.BlockSpec((tm, D_MODEL), row),
            pl.BlockSpec((2 * PEER_HEADS, PEER_KEYS, tm), lambda i: (0, 0, i)),
        ],
        out_shape=[
            jax.ShapeDtypeStruct((seq, D_MODEL), F32),
            jax.ShapeDtypeStruct((seq, D_MODEL), BF16),
            jax.ShapeDtypeStruct((2 * PEER_HEADS, PEER_KEYS, seq), F32),
        ],
        compiler_params=_cparams(("parallel",)),
        name="mid",
    )(oa, ob, x, w_out, g, w_query, sub_keys)


N_TOP = PEER_TOPK + 1
_CAND = [(a, b) for a in range(N_TOP) for b in range(N_TOP) if (a + 1) * (b + 1) <= N_TOP]


def _top_vals(s, n):
    vals = []
    for r in range(n):
        m = jnp.max(s, axis=0, keepdims=True)
        vals.append(m)
        if r + 1 < n:
            s = jnp.where(s >= m, -jnp.inf, s)
    return vals


def _topk_kernel(sc_ref, e1_ref, e2_ref, tau_ref):
    t = sc_ref.shape[-1]
    taus = []
    for h in range(PEER_HEADS):
        s1 = sc_ref[2 * h]
        s2 = sc_ref[2 * h + 1]
        v1 = _top_vals(s1, N_TOP)
        v2 = _top_vals(s2, N_TOP)
        cands = [v1[a] + v2[b] for (a, b) in _CAND]
        pad = (-len(cands)) % 8
        cands += [jnp.full((1, t), -jnp.inf, F32)] * pad
        c = jnp.concatenate(cands, axis=0)
        top = _top_vals(c, N_TOP)
        t16, t17 = top[PEER_TOPK - 1], top[PEER_TOPK]
        best = v1[0] + v2[0]
        z = jnp.sum(jnp.where(c >= t16, jnp.exp(c - best), 0.0), axis=0, keepdims=True)
        rz = 1.0 / z
        e1_ref[h] = jnp.exp(s1 - v1[0]) * rz
        e2_ref[h] = jnp.exp(s2 - v2[0])
        taus.append(0.5 * (jnp.exp(t16 - best) + jnp.exp(t17 - best)) * rz)
    tau_ref[...] = jnp.concatenate(taus, axis=0)


def _topk_call(sc, tt):
    seq = sc.shape[-1]
    blk = lambda i: (0, 0, i)
    return pl.pallas_call(
        _topk_kernel,
        grid=(seq // tt,),
        in_specs=[pl.BlockSpec((2 * PEER_HEADS, PEER_KEYS, tt), blk)],
        out_specs=[
            pl.BlockSpec((PEER_HEADS, PEER_KEYS, tt), blk),
            pl.BlockSpec((PEER_HEADS, PEER_KEYS, tt), blk),
            pl.BlockSpec((PEER_HEADS, tt), lambda i: (0, i)),
        ],
        out_shape=[
            jax.ShapeDtypeStruct((PEER_HEADS, PEER_KEYS, seq), F32),
            jax.ShapeDtypeStruct((PEER_HEADS, PEER_KEYS, seq), F32),
            jax.ShapeDtypeStruct((PEER_HEADS, seq), F32),
        ],
        compiler_params=_cparams(("parallel",)),
        name="topk",
    )(sc)


def _peer_kernel(xn_ref, u_ref, vt_ref, e1_ref, e2_ref, tau_ref, h_ref, g_ref, o_ref,
                 acc_ref, a_ref, w_ref, *, eb):
    e = pl.program_id(1)
    tt = xn_ref.shape[0]
    inv_sqrt2 = 1.0 / math.sqrt(2.0)
    a_ref[...] = lax.dot_general(u_ref[...], xn_ref[...], NT_DIMS,
                                 preferred_element_type=F32)
    for ii in range(eb // PEER_KEYS):
        i = e * (eb // PEER_KEYS) + ii
        rows = slice(PEER_KEYS * ii, PEER_KEYS * (ii + 1))
        e1_rows = [e1_ref[h, pl.ds(i, 1), :] for h in range(PEER_HEADS)]
        for tc in range(tt // LANES):
            cols = slice(LANES * tc, LANES * (tc + 1))
            a = a_ref[rows, cols]
            act = 0.5 * a * (1.0 + lax.erf(a * inv_sqrt2))
            gate = None
            for h in range(PEER_HEADS):
                p = e2_ref[h, :, cols] * e1_rows[h][:, cols]
                term = jnp.where(p >= tau_ref[h:h + 1, cols], p, 0.0)
                gate = term if gate is None else gate + term
            w_ref[rows, cols] = (gate * act).astype(BF16)
    part = jnp.dot(vt_ref[...], w_ref[...], preferred_element_type=F32)

    @pl.when(e == 0)
    def _():
        acc_ref[...] = part

    @pl.when(e > 0)
    def _():
        acc_ref[...] += part

    @pl.when(e == pl.num_programs(1) - 1)
    def _():
        out = h_ref[...] + acc_ref[...].T
        o_ref[...] = _rms(out, g_ref[...], NORM_EPS)


def _peer_call(xn, u, vt, e1, e2, tau, h, g, tt, eb):
    seq = xn.shape[0]
    n_exp = u.shape[0]
    tok = lambda t, e: (t, 0)
    tok3 = lambda t, e: (0, 0, t)
    return pl.pallas_call(
        functools.partial(_peer_kernel, eb=eb),
        grid=(seq // tt, n_exp // eb),
        in_specs=[
            pl.BlockSpec((tt, D_MODEL), tok),
            pl.BlockSpec((eb, D_MODEL), lambda t, e: (e, 0)),
            pl.BlockSpec((D_MODEL, eb), lambda t, e: (0, e)),
            pl.BlockSpec((PEER_HEADS, PEER_KEYS, tt), tok3),
            pl.BlockSpec((PEER_HEADS, PEER_KEYS, tt), tok3),
            pl.BlockSpec((PEER_HEADS, tt), lambda t, e: (0, t)),
            pl.BlockSpec((tt, D_MODEL), tok),
            pl.BlockSpec((1, D_MODEL), lambda t, e: (0, 0)),
        ],
        out_specs=pl.BlockSpec((tt, D_MODEL), tok),
        out_shape=jax.ShapeDtypeStruct((seq, D_MODEL), F32),
        scratch_shapes=[pltpu.VMEM((D_MODEL, tt), F32), pltpu.VMEM((eb, tt), F32), pltpu.VMEM((eb, tt), BF16)],
        compiler_params=_cparams(("parallel", "arbitrary")),
        name="peer",
    )(xn, u, vt, e1, e2, tau, h, g)


def _tiles(seq):
    big = seq >= 4096
    return dict(
        tm=512 if big else 256,
        tq_a=64, tq_b=128,
        tk=1024 if big else 512,
        tt_topk=256, tt_peer=512 if big else 256, eb=2048,
    )


def kernel(x, norm_attn_g, w_in, q_norm_g, k_norm_g, lambda_q1, lambda_k1, lambda_q2, lambda_k2,
           subln_g, w_out, norm_ffn_g, w_query, sub_keys, expert_u, expert_v, norm_final_g):
    batch, seq, d = x.shape
    assert batch == 1 and d == D_MODEL and norm_attn_g.shape[0] == 1
    t = _tiles(seq)
    lambda_init = 0.8 - 0.6 * math.exp(-0.3 * 0)
    x2 = x.reshape(seq, d)

    tab = _rope_tables(seq)
    blk = np.arange(A_Q_W) // HEAD_DIM
    bd = jnp.asarray(np.where(blk[:, None] == blk[None, :], 1.0 / HEAD_DIM, 0.0), dtype=BF16)
    qg = (jnp.tile(q_norm_g[0], A_Q_HEADS) * Q_SCALE).reshape(1, A_Q_W)
    kg = jnp.tile(k_norm_g[0], A_KV_HEADS).reshape(1, A_KV_W)

    qa, ka, vat, qb, kb, vbt = _proj_call(x2, norm_attn_g, w_in[0].astype(BF16), qg, kg, tab, bd, t["tm"])
    oa = _gqa_call(qa, ka, vat, t["tq_a"], t["tk"])
    lam_vecs = jnp.concatenate([lambda_q1, lambda_k1, lambda_q2, lambda_k2], axis=0)
    ob = _diff_call(lam_vecs, qb, kb, vbt, subln_g, t["tq_b"], t["tk"], lambda_init)

    sk = sub_keys[0].reshape(2 * PEER_HEADS, PEER_KEYS, PEER_HALF).astype(BF16)
    h, xn, sc = _mid_call(oa, ob, x2, w_out[0].astype(BF16), norm_ffn_g, w_query[0].astype(BF16), sk, t["tm"])
    e1, e2, tau = _topk_call(sc, t["tt_topk"])
    out = _peer_call(xn, expert_u[0].astype(BF16), expert_v[0].T.astype(BF16), e1, e2, tau, h,
                     norm_final_g.reshape(1, d), t["tt_peer"], t["eb"])
    return out.reshape(batch, seq, d)
```

```python
import functools
import math

import jax
import jax.numpy as jnp
import numpy as np
from jax import lax
from jax.experimental import pallas as pl
from jax.experimental.pallas import tpu as pltpu

F32 = jnp.float32
BF16 = jnp.bfloat16

D_MODEL = 1024
HEAD_DIM = 64
A_Q_HEADS = 8
A_KV_HEADS = 2
A_GROUP = A_Q_HEADS // A_KV_HEADS
B_HEADS = 4
B_V_DIM = 2 * HEAD_DIM
GRID_W = 64
AXIAL_THETA = 10000.0
AXIAL_HALF = HEAD_DIM // 2
ROPE_THETA = 500000.0
ROPE_DIMS = HEAD_DIM // 4
NORM_EPS = 1e-6
SUBLN_EPS = 1e-5
A_Q_W = A_Q_HEADS * HEAD_DIM
A_KV_W = A_KV_HEADS * HEAD_DIM
B_QK_W = 2 * B_HEADS * HEAD_DIM
B_V_W = B_HEADS * B_V_DIM
IN_COLS = A_Q_W + 2 * A_KV_W + 2 * B_QK_W + B_V_W
PEER_HEADS = 8
PEER_KEYS = 128
PEER_HALF = 128
PEER_TOPK = 16
LANES = 128

NT_DIMS = (((1,), (1,)), ((), ()))
Q_SCALE = (HEAD_DIM ** -0.5) * math.log2(math.e)

VMEM_LIMIT = 56 * 1024 * 1024


def _cparams(sem):
    return pltpu.CompilerParams(dimension_semantics=sem, vmem_limit_bytes=VMEM_LIMIT)


def _rms(x, g, eps):
    return x * lax.rsqrt(jnp.mean(x * x, axis=-1, keepdims=True) + eps) * g


def _group_mean_sq(v, bd):
    v2 = v * v
    hi = v2.astype(BF16)
    lo = (v2 - hi.astype(F32)).astype(BF16)
    return (jnp.dot(hi, bd, preferred_element_type=F32)
            + jnp.dot(lo, bd, preferred_element_type=F32))


def _rot_half(v, half, group):
    width = v.shape[-1]
    lane = lax.broadcasted_iota(jnp.int32, v.shape, 1)
    fwd = pltpu.roll(v, width - half, 1)
    bwd = pltpu.roll(v, half, 1)
    return jnp.where((lane % group) < half, fwd, bwd)


def _tile4(t):
    return jnp.concatenate([t, t, t, t], axis=1)


def _proj_kernel(x_ref, g_ref, w_ref, qg_ref, kg_ref, tab_ref, bd_ref,
                 qa_ref, ka_ref, va_ref, qb_ref, kb_ref, vb_ref):
    xn = _rms(x_ref[...], g_ref[...], NORM_EPS)
    proj = jnp.dot(xn.astype(BF16), w_ref[...], preferred_element_type=F32)
    tab = tab_ref[...]
    cos_a, sin_a = tab[:, 0:128], tab[:, 128:256]
    cos_b, sin_b = tab[:, 256:384], tab[:, 384:512]
    bd = bd_ref[...]

    c0 = 0
    qa = proj[:, c0:c0 + A_Q_W]
    qa = qa * lax.rsqrt(_group_mean_sq(qa, bd) + NORM_EPS) * qg_ref[...]
    qa = qa * _tile4(cos_a) + _rot_half(qa, AXIAL_HALF // 2, AXIAL_HALF) * _tile4(sin_a)
    qa_ref[...] = qa.astype(BF16)
    c0 += A_Q_W

    ka = proj[:, c0:c0 + A_KV_W]
    ka = ka * lax.rsqrt(_group_mean_sq(ka, bd[:A_KV_W, :A_KV_W]) + NORM_EPS) * kg_ref[...]
    ka = ka * cos_a + _rot_half(ka, AXIAL_HALF // 2, AXIAL_HALF) * sin_a
    ka_ref[...] = ka.astype(BF16)
    c0 += A_KV_W

    va_ref[0] = proj[:, c0:c0 + A_KV_W].T.astype(BF16)
    c0 += A_KV_W

    qb = proj[:, c0:c0 + B_QK_W]
    qb = qb * _tile4(cos_b) + _rot_half(qb, ROPE_DIMS // 2, HEAD_DIM) * _tile4(sin_b)
    qb_ref[...] = (qb * Q_SCALE).astype(BF16)
    c0 += B_QK_W

    kb = proj[:, c0:c0 + B_QK_W]
    kb = kb * _tile4(cos_b) + _rot_half(kb, ROPE_DIMS // 2, HEAD_DIM) * _tile4(sin_b)
    kb_ref[...] = kb.astype(BF16)
    c0 += B_QK_W

    vb_ref[0] = proj[:, c0:c0 + B_V_W].T.astype(BF16)


def _rope_tables(seq):
    f32 = np.float32
    rows = seq // GRID_W
    row = np.repeat(np.arange(rows, dtype=f32), GRID_W)
    col = np.tile(np.arange(GRID_W, dtype=f32), rows)
    pos = np.arange(seq, dtype=f32)
    inv_ax = (f32(AXIAL_THETA) ** (-np.arange(0, AXIAL_HALF, 2, dtype=f32) / f32(AXIAL_HALF))).astype(f32)
    inv_p = (f32(ROPE_THETA) ** (-np.arange(0, ROPE_DIMS, 2, dtype=f32) / f32(ROPE_DIMS))).astype(f32)
    row_ang = row[:, None] * inv_ax[None, :]
    col_ang = col[:, None] * inv_ax[None, :]
    pos_ang = pos[:, None] * inv_p[None, :]
    cr, sr = np.cos(row_ang), np.sin(row_ang)
    cc, sc = np.cos(col_ang), np.sin(col_ang)
    cp, sp = np.cos(pos_ang), np.sin(pos_ang)
    rest = HEAD_DIM - ROPE_DIMS
    cos_a = np.concatenate([cr, cr, cc, cc], axis=1)
    sin_a = np.concatenate([-sr, sr, -sc, sc], axis=1)
    cos_b = np.concatenate([cp, cp, np.ones((seq, rest), f32)], axis=1)
    sin_b = np.concatenate([-sp, sp, np.zeros((seq, rest), f32)], axis=1)
    two = lambda t: np.concatenate([t, t], axis=1)
    tab = np.concatenate([two(cos_a), two(sin_a), two(cos_b), two(sin_b)], axis=1).astype(f32)
    return jnp.asarray(tab)


def _proj_call(x, g, w_in, qg, kg, tab, bd, tm):
    seq = x.shape[0]
    row = lambda i: (i, 0)
    fix = lambda i: (0, 0)
    n = seq // tm
    rows_out = lambda w: (pl.BlockSpec((tm, w), row), jax.ShapeDtypeStruct((seq, w), BF16))
    cols_out = lambda w: (pl.BlockSpec((1, w, tm), lambda i: (i, 0, 0)), jax.ShapeDtypeStruct((n, w, tm), BF16))
    outs = [rows_out(A_Q_W), rows_out(A_KV_W), cols_out(A_KV_W), rows_out(B_QK_W), rows_out(B_QK_W), cols_out(B_V_W)]
    return pl.pallas_call(
        _proj_kernel,
        grid=(seq // tm,),
        in_specs=[
            pl.BlockSpec((tm, D_MODEL), row),
            pl.BlockSpec((1, D_MODEL), fix),
            pl.BlockSpec((D_MODEL, IN_COLS), fix),
            pl.BlockSpec((1, A_Q_W), fix),
            pl.BlockSpec((1, A_KV_W), fix),
            pl.BlockSpec((tm, 512), row),
            pl.BlockSpec((A_Q_W, A_Q_W), fix),
        ],
        out_specs=[o[0] for o in outs],
        out_shape=[o[1] for o in outs],
        compiler_params=_cparams(("parallel",)),
        name="proj",
    )(x, g, w_in, qg, kg, tab, bd)


SOFTMAX_ROWS = 16
N_SLOTS = 3
STEPS_PER_TRIP = 3
ST_M, ST_ALPHA, ST_NEXT_MAX, ST_L = 0, 1, 2, 8


def _flash_scratch(nq, tk, dv):
    return ([pltpu.VMEM((tk, nq), F32)] * N_SLOTS + [pltpu.VMEM((tk, nq), BF16)] * N_SLOTS
            + [pltpu.VMEM((dv, nq), F32), pltpu.VMEM((16, nq), F32)])


def _flash_cols(qst, k_ref, vt_ref, scratch):
    s_ref, p_ref = scratch[:N_SLOTS], scratch[N_SLOTS:2 * N_SLOTS]
    acc_ref, st_ref = scratch[2 * N_SLOTS:]
    nq = qst.shape[1]
    tk = s_ref[0].shape[0]
    per = tk // vt_ref.shape[2]
    n_chunks = vt_ref.shape[0] // per
    assert n_chunks >= N_SLOTS and tk % SOFTMAX_ROWS == 0
    n_sub = tk // SOFTMAX_ROWS
    row = lambda r, n=1: slice(r, r + n)
    sub = lambda slot, b: s_ref[slot][b * SOFTMAX_ROWS:(b + 1) * SOFTMAX_ROWS, :]

    def scores(j, slot):
        start = j * tk
        start = start if isinstance(start, int) else pl.multiple_of(start, tk)
        s_ref[slot][...] = jnp.dot(k_ref[pl.ds(start, tk), :], qst, preferred_element_type=F32)

    def column_max(slot):
        mx = sub(slot, 0)
        for b in range(1, n_sub):
            mx = jnp.maximum(mx, sub(slot, b))
        st_ref[row(ST_NEXT_MAX), :] = jnp.max(mx, axis=0, keepdims=True)

    def weighted(j, slot, alpha):
        vt = jnp.concatenate([vt_ref[per * j + c] for c in range(per)], axis=1)
        acc_ref[...] = alpha * acc_ref[...] + jnp.dot(vt, p_ref[slot][...], preferred_element_type=F32)

    def step(j, slot, with_scores=True, with_max=True):
        a_prev = st_ref[row(ST_ALPHA), :]
        m_old = st_ref[row(ST_M), :]
        m_new = jnp.maximum(m_old, st_ref[row(ST_NEXT_MAX), :])
        alpha = jnp.exp2(m_old - m_new)
        if with_scores:
            scores(j + 2, (slot + 2) % N_SLOTS)
        psum = jnp.zeros((8, nq), F32)
        for b in range(n_sub):
            p = jnp.exp2(sub(slot, b) - m_new)
            psum = psum + jnp.sum(p.reshape(SOFTMAX_ROWS // 8, 8, nq), axis=0)
            p_ref[slot][b * SOFTMAX_ROWS:(b + 1) * SOFTMAX_ROWS, :] = p.astype(BF16)
        weighted(jnp.maximum(j - 1, 0), (slot + 2) % N_SLOTS, a_prev)
        st_ref[row(ST_L, 8), :] = alpha * st_ref[row(ST_L, 8), :] + psum
        st_ref[row(ST_M), :] = m_new
        st_ref[row(ST_ALPHA), :] = alpha
        if with_max:
            column_max((slot + 1) % N_SLOTS)

    scores(0, 0)
    scores(1, 1)
    st_ref[...] = jnp.zeros_like(st_ref)
    st_ref[row(ST_M), :] = jnp.full((1, nq), -jnp.inf, F32)
    acc_ref[...] = jnp.zeros_like(acc_ref)
    p_ref[N_SLOTS - 1][...] = jnp.zeros((tk, nq), BF16)
    column_max(0)
    n_full = n_chunks - 2
    n_trips = n_full // STEPS_PER_TRIP
    lead = n_full - n_trips * STEPS_PER_TRIP
    for j in range(lead):
        step(j, j % N_SLOTS)

    def body(t, carry):
        for u in range(STEPS_PER_TRIP):
            step(lead + t * STEPS_PER_TRIP + u, (lead + u) % N_SLOTS)
        return carry

    lax.fori_loop(0, n_trips, body, 0)
    step(n_chunks - 2, (n_chunks - 2) % N_SLOTS, with_scores=False)
    step(n_chunks - 1, (n_chunks - 1) % N_SLOTS, with_scores=False, with_max=False)
    weighted(n_chunks - 1, (n_chunks - 1) % N_SLOTS, st_ref[row(ST_ALPHA), :])
    return acc_ref[...], jnp.sum(st_ref[row(ST_L, 8), :], axis=0, keepdims=True)


def _gqa_kernel(q_ref, k_ref, vt_ref, o_ref, *scratch, tq):
    g = pl.program_id(0)
    lane = lax.broadcasted_iota(jnp.int32, (tq, LANES), 1)
    in_g = (lane // HEAD_DIM) == g
    rows = []
    for hh in range(A_GROUP):
        qp = q_ref[:, LANES * (hh // 2):LANES * (hh // 2 + 1)].astype(F32)
        aligned = jnp.where((hh % 2) == g, qp, pltpu.roll(qp, HEAD_DIM, 1))
        rows.append(jnp.where(in_g, aligned, 0.0))
    qs = jnp.concatenate(rows, axis=0)
    acc_t, l = _flash_cols(qs.T.astype(BF16), k_ref, vt_ref, scratch)
    o_t = acc_t / l
    o = jnp.concatenate([o_t, jnp.zeros_like(o_t)], axis=0).T
    left_half = lane < HEAD_DIM
    for p in range(A_GROUP // 2):
        a = o[(2 * p) * tq:(2 * p + 1) * tq]
        b = o[(2 * p + 1) * tq:(2 * p + 2) * tq]
        o_ref[:, LANES * p:LANES * (p + 1)] = jnp.where(left_half, a, pltpu.roll(b, HEAD_DIM, 1)).astype(BF16)


def _gqa_call(qa, ka, vat, tq, tk):
    seq = qa.shape[0]
    return pl.pallas_call(
        functools.partial(_gqa_kernel, tq=tq),
        grid=(A_KV_HEADS, seq // tq),
        in_specs=[
            pl.BlockSpec((tq, A_GROUP * HEAD_DIM), lambda g, i: (i, g)),
            pl.BlockSpec((seq, A_KV_W), lambda g, i: (0, 0)),
            pl.BlockSpec((vat.shape[0], HEAD_DIM, vat.shape[2]), lambda g, i: (0, g, 0)),
        ],
        out_specs=pl.BlockSpec((tq, A_GROUP * HEAD_DIM), lambda g, i: (i, g)),
        out_shape=jax.ShapeDtypeStruct((seq, A_Q_W), BF16),
        scratch_shapes=_flash_scratch(A_GROUP * tq, tk, HEAD_DIM),
        compiler_params=_cparams(("parallel", "parallel")),
        name="gqa",
    )(qa, ka, vat)


def _diff_kernel(lam_ref, q_ref, k_ref, vt_ref, sg_ref, o_ref, *scratch, tq, lambda_init):
    lane = lax.broadcasted_iota(jnp.int32, (tq, LANES), 1)
    q = q_ref[...].astype(F32)
    qs = jnp.concatenate([jnp.where(lane < HEAD_DIM, q, 0.0),
                          jnp.where(lane >= HEAD_DIM, q, 0.0)], axis=0)
    acc_t, l = _flash_cols(qs.T.astype(BF16), k_ref, vt_ref, scratch)
    o = (acc_t / l).T
    lv = lam_ref[...]
    lam = (jnp.exp(jnp.sum(lv[0:1] * lv[1:2], axis=-1, keepdims=True))
           - jnp.exp(jnp.sum(lv[2:3] * lv[3:4], axis=-1, keepdims=True)) + lambda_init)
    ob = o[:tq] - lam * o[tq:]
    ob = _rms(ob, sg_ref[...], SUBLN_EPS) * (1.0 - lambda_init)
    o_ref[...] = ob.astype(BF16)


def _diff_call(lam_vecs, qb, kb, vbt, subln_g, tq, tk, lambda_init):
    seq = qb.shape[0]
    n_slabs, _, slab = vbt.shape
    return pl.pallas_call(
        functools.partial(_diff_kernel, tq=tq, lambda_init=lambda_init),
        grid=(B_HEADS, seq // tq),
        in_specs=[
            pl.BlockSpec((4, HEAD_DIM), lambda h, i: (0, 0)),
            pl.BlockSpec((tq, LANES), lambda h, i: (i, h)),
            pl.BlockSpec((seq, LANES), lambda h, i: (0, h)),
            pl.BlockSpec((n_slabs, B_V_DIM, slab), lambda h, i: (0, h, 0)),
            pl.BlockSpec((1, B_V_DIM), lambda h, i: (0, 0)),
        ],
        out_specs=pl.BlockSpec((tq, LANES), lambda h, i: (i, h)),
        out_shape=jax.ShapeDtypeStruct((seq, B_V_W), BF16),
        scratch_shapes=_flash_scratch(2 * tq, tk, B_V_DIM),
        compiler_params=_cparams(("parallel", "parallel")),
        name="diff",
    )(lam_vecs, qb, kb, vbt, subln_g)


def _mid_kernel(oa_ref, ob_ref, x_ref, wo_ref, g_ref, wq_ref, sk_ref, h_ref, xn_ref, sc_ref):
    o = jnp.concatenate([oa_ref[...], ob_ref[...]], axis=1)
    h = x_ref[...] + jnp.dot(o, wo_ref[...], preferred_element_type=F32)
    h_ref[...] = h
    xn = _rms(h, g_ref[...], NORM_EPS).astype(BF16)
    xn_ref[...] = xn
    q = jnp.dot(xn, wq_ref[...], preferred_element_type=F32).astype(BF16)
    for hp in range(2 * PEER_HEADS):
        sc_ref[hp] = lax.dot_general(sk_ref[hp], q[:, PEER_HALF * hp:PEER_HALF * (hp + 1)],
                                     NT_DIMS, preferred_element_type=F32)


def _mid_call(oa, ob, x, w_out, g, w_query, sub_keys, tm):
    seq = x.shape[0]
    row = lambda i: (i, 0)
    fix = lambda i: (0, 0)
    nq = 2 * PEER_HEADS * PEER_HALF
    return pl.pallas_call(
        _mid_kernel,
        grid=(seq // tm,),
        in_specs=[
            pl.BlockSpec((tm, A_Q_W), row),
            pl.BlockSpec((tm, B_V_W), row),
            pl.BlockSpec((tm, D_MODEL), row),
            pl.BlockSpec((D_MODEL, D_MODEL), fix),
            pl.BlockSpec((1, D_MODEL), fix),
            pl.BlockSpec((D_MODEL, nq), fix),
            pl.BlockSpec((2 * PEER_HEADS, PEER_KEYS, PEER_HALF), lambda i: (0, 0, 0)),
        ],
        out_specs=[
            pl.BlockSpec((tm, D_MODEL), row),
            pl.BlockSpec((tm, D_MODEL), row),
            pl.BlockSpec((2 * PEER_HEADS, PEER_KEYS, tm), lambda i: (0, 0, i)),
        ],
        out_shape=[
            jax.ShapeDtypeStruct((seq, D_MODEL), F32),
            jax.ShapeDtypeStruct((seq, D_MODEL), BF16),
            jax.ShapeDtypeStruct((2 * PEER_HEADS, PEER_KEYS, seq), F32),
        ],
        compiler_params=_cparams(("parallel",)),
        name="mid",
    )(oa, ob, x, w_out, g, w_query, sub_keys)


N_TOP = PEER_TOPK + 1
_CAND = [(a, b) for a in range(N_TOP) for b in range(N_TOP) if (a + 1) * (b + 1) <= N_TOP]


def _top_vals(s, n):
    vals = []
    for r in range(n):
        m = jnp.max(s, axis=0, keepdims=True)
        vals.append(m)
        if r + 1 < n:
            s = jnp.where(s >= m, -jnp.inf, s)
    return vals


def _topk_kernel(sc_ref, e1_ref, e2_ref, tau_ref):
    t = sc_ref.shape[-1]
    taus = []
    for h in range(PEER_HEADS):
        s1 = sc_ref[2 * h]
        s2 = sc_ref[2 * h + 1]
        v1 = _top_vals(s1, N_TOP)
        v2 = _top_vals(s2, N_TOP)
        cands = [v1[a] + v2[b] for (a, b) in _CAND]
        pad = (-len(cands)) % 8
        cands += [jnp.full((1, t), -jnp.inf, F32)] * pad
        c = jnp.concatenate(cands, axis=0)
        top = _top_vals(c, N_TOP)
        t16, t17 = top[PEER_TOPK - 1], top[PEER_TOPK]
        best = v1[0] + v2[0]
        z = jnp.sum(jnp.where(c >= t16, jnp.exp(c - best), 0.0), axis=0, keepdims=True)
        rz = 1.0 / z
        e1_ref[h] = jnp.exp(s1 - v1[0]) * rz
        e2_ref[h] = jnp.exp(s2 - v2[0])
        taus.append(0.5 * (jnp.exp(t16 - best) + jnp.exp(t17 - best)) * rz)
    tau_ref[...] = jnp.concatenate(taus, axis=0)


def _topk_call(sc, tt):
    seq = sc.shape[-1]
    blk = lambda i: (0, 0, i)
    return pl.pallas_call(
        _topk_kernel,
        grid=(seq // tt,),
        in_specs=[pl.BlockSpec((2 * PEER_HEADS, PEER_KEYS, tt), blk)],
        out_specs=[
            pl.BlockSpec((PEER_HEADS, PEER_KEYS, tt), blk),
            pl.BlockSpec((PEER_HEADS, PEER_KEYS, tt), blk),
            pl.BlockSpec((PEER_HEADS, tt), lambda i: (0, i)),
        ],
        out_shape=[
            jax.ShapeDtypeStruct((PEER_HEADS, PEER_KEYS, seq), F32),
            jax.ShapeDtypeStruct((PEER_HEADS, PEER_KEYS, seq), F32),
            jax.ShapeDtypeStruct((PEER_HEADS, seq), F32),
        ],
        compiler_params=_cparams(("parallel",)),
        name="topk",
    )(sc)


def _peer_kernel(xn_ref, u_ref, vt_ref, e1_ref, e2_ref, tau_ref, h_ref, g_ref, o_ref,
                 acc_ref, a_ref, w_ref, *, eb):
    e = pl.program_id(1)
    tt = xn_ref.shape[0]
    inv_sqrt2 = 1.0 / math.sqrt(2.0)
    a_ref[...] = lax.dot_general(u_ref[...], xn_ref[...], NT_DIMS,
                                 preferred_element_type=F32)
    for ii in range(eb // PEER_KEYS):
        i = e * (eb // PEER_KEYS) + ii
        rows = slice(PEER_KEYS * ii, PEER_KEYS * (ii + 1))
        e1_rows = [e1_ref[h, pl.ds(i, 1), :] for h in range(PEER_HEADS)]
        for tc in range(tt // LANES):
            cols = slice(LANES * tc, LANES * (tc + 1))
            a = a_ref[rows, cols]
            act = 0.5 * a * (1.0 + lax.erf(a * inv_sqrt2))
            gate = None
            for h in range(PEER_HEADS):
                p = e2_ref[h, :, cols] * e1_rows[h][:, cols]
                term = jnp.where(p >= tau_ref[h:h + 1, cols], p, 0.0)
                gate = term if gate is None else gate + term
            w_ref[rows, cols] = (gate * act).astype(BF16)
    part = jnp.dot(vt_ref[...], w_ref[...], preferred_element_type=F32)

    @pl.when(e == 0)
    def _():
        acc_ref[...] = part

    @pl.when(e > 0)
    def _():
        acc_ref[...] += part

    @pl.when(e == pl.num_programs(1) - 1)
    def _():
        out = h_ref[...] + acc_ref[...].T
        o_ref[...] = _rms(out, g_ref[...], NORM_EPS)


def _peer_call(xn, u, vt, e1, e2, tau, h, g, tt, eb):
    seq = xn.shape[0]
    n_exp = u.shape[0]
    tok = lambda t, e: (t, 0)
    tok3 = lambda t, e: (0, 0, t)
    return pl.pallas_call(
        functools.partial(_peer_kernel, eb=eb),
        grid=(seq // tt, n_exp // eb),
        in_specs=[
            pl.BlockSpec((tt, D_MODEL), tok),
            pl.BlockSpec((eb, D_MODEL), lambda t, e: (e, 0)),
            pl.BlockSpec((D_MODEL, eb), lambda t, e: (0, e)),
            pl.BlockSpec((PEER_HEADS, PEER_KEYS, tt), tok3),
            pl.BlockSpec((PEER_HEADS, PEER_KEYS, tt), tok3),
            pl.BlockSpec((PEER_HEADS, tt), lambda t, e: (0, t)),
            pl.BlockSpec((tt, D_MODEL), tok),
            pl.BlockSpec((1, D_MODEL), lambda t, e: (0, 0)),
        ],
        out_specs=pl.BlockSpec((tt, D_MODEL), tok),
        out_shape=jax.ShapeDtypeStruct((seq, D_MODEL), F32),
        scratch_shapes=[pltpu.VMEM((D_MODEL, tt), F32), pltpu.VMEM((eb, tt), F32), pltpu.VMEM((eb, tt), BF16)],
        compiler_params=_cparams(("parallel", "arbitrary")),
        name="peer",
    )(xn, u, vt, e1, e2, tau, h, g)


def _tiles(seq):
    big = seq >= 4096
    return dict(
        tm=512 if big else 256,
        tq_a=128, tq_b=256,
        tk=2048 if big else 512,
        tt_topk=256, tt_peer=512 if big else 256, eb=2048,
    )


def kernel(x, norm_attn_g, w_in, q_norm_g, k_norm_g, lambda_q1, lambda_k1, lambda_q2, lambda_k2,
           subln_g, w_out, norm_ffn_g, w_query, sub_keys, expert_u, expert_v, norm_final_g):
    batch, seq, d = x.shape
    assert batch == 1 and d == D_MODEL and norm_attn_g.shape[0] == 1
    t = _tiles(seq)
    lambda_init = 0.8 - 0.6 * math.exp(-0.3 * 0)
    x2 = x.reshape(seq, d)

    tab = _rope_tables(seq)
    blk = np.arange(A_Q_W) // HEAD_DIM
    bd = jnp.asarray(np.where(blk[:, None] == blk[None, :], 1.0 / HEAD_DIM, 0.0), dtype=BF16)
    qg = (jnp.tile(q_norm_g[0], A_Q_HEADS) * Q_SCALE).reshape(1, A_Q_W)
    kg = jnp.tile(k_norm_g[0], A_KV_HEADS).reshape(1, A_KV_W)

    qa, ka, vat, qb, kb, vbt = _proj_call(x2, norm_attn_g, w_in[0].astype(BF16), qg, kg, tab, bd, t["tm"])
    oa = _gqa_call(qa, ka, vat, t["tq_a"], t["tk"])
    lam_vecs = jnp.concatenate([lambda_q1, lambda_k1, lambda_q2, lambda_k2], axis=0)
    ob = _diff_call(lam_vecs, qb, kb, vbt, subln_g, t["tq_b"], t["tk"], lambda_init)

    sk = sub_keys[0].reshape(2 * PEER_HEADS, PEER_KEYS, PEER_HALF).astype(BF16)
    h, xn, sc = _mid_call(oa, ob, x2, w_out[0].astype(BF16), norm_ffn_g, w_query[0].astype(BF16), sk, t["tm"])
    e1, e2, tau = _topk_call(sc, t["tt_topk"])
    out = _peer_call(xn, expert_u[0].astype(BF16), expert_v[0].T.astype(BF16), e1, e2, tau, h,
                     norm_final_g.reshape(1, d), t["tt_peer"], t["eb"])
    return out.reshape(batch, seq, d)
```

```python
import functools
import math

import jax
import jax.numpy as jnp
import numpy as np
from jax import lax
from jax.experimental import pallas as pl
from jax.experimental.pallas import tpu as pltpu

F32 = jnp.float32
BF16 = jnp.bfloat16

D_MODEL = 1024
HEAD_DIM = 64
A_Q_HEADS = 8
A_KV_HEADS = 2
A_GROUP = A_Q_HEADS // A_KV_HEADS
B_HEADS = 4
B_V_DIM = 2 * HEAD_DIM
GRID_W = 64
AXIAL_THETA = 10000.0
AXIAL_HALF = HEAD_DIM // 2
ROPE_THETA = 500000.0
ROPE_DIMS = HEAD_DIM // 4
NORM_EPS = 1e-6
SUBLN_EPS = 1e-5
A_Q_W = A_Q_HEADS * HEAD_DIM
A_KV_W = A_KV_HEADS * HEAD_DIM
B_QK_W = 2 * B_HEADS * HEAD_DIM
B_V_W = B_HEADS * B_V_DIM
IN_COLS = A_Q_W + 2 * A_KV_W + 2 * B_QK_W + B_V_W
PEER_HEADS = 8
PEER_KEYS = 128
PEER_HALF = 128
PEER_TOPK = 16
LANES = 128

NT_DIMS = (((1,), (1,)), ((), ()))
Q_SCALE = (HEAD_DIM ** -0.5) * math.log2(math.e)

VMEM_LIMIT = 56 * 1024 * 1024


def _cparams(sem):
    return pltpu.CompilerParams(dimension_semantics=sem, vmem_limit_bytes=VMEM_LIMIT)


def _rms(x, g, eps):
    return x * lax.rsqrt(jnp.mean(x * x, axis=-1, keepdims=True) + eps) * g


def _group_mean_sq(v, bd):
    v2 = v * v
    hi = v2.astype(BF16)
    lo = (v2 - hi.astype(F32)).astype(BF16)
    return (jnp.dot(hi, bd, preferred_element_type=F32)
            + jnp.dot(lo, bd, preferred_element_type=F32))


def _rot_half(v, half, group):
    width = v.shape[-1]
    lane = lax.broadcasted_iota(jnp.int32, v.shape, 1)
    fwd = pltpu.roll(v, width - half, 1)
    bwd = pltpu.roll(v, half, 1)
    return jnp.where((lane % group) < half, fwd, bwd)


def _tile4(t):
    return jnp.concatenate([t, t, t, t], axis=1)


def _proj_kernel(x_ref, g_ref, w_ref, qg_ref, kg_ref, tab_ref, bd_ref,
                 qa_ref, ka_ref, va_ref, qb_ref, kb_ref, vb_ref):
    xn = _rms(x_ref[...], g_ref[...], NORM_EPS)
    proj = jnp.dot(xn.astype(BF16), w_ref[...], preferred_element_type=F32)
    tab = tab_ref[...]
    cos_a, sin_a = tab[:, 0:128], tab[:, 128:256]
    cos_b, sin_b = tab[:, 256:384], tab[:, 384:512]
    bd = bd_ref[...]

    c0 = 0
    qa = proj[:, c0:c0 + A_Q_W]
    qa = qa * lax.rsqrt(_group_mean_sq(qa, bd) + NORM_EPS) * qg_ref[...]
    qa = qa * _tile4(cos_a) + _rot_half(qa, AXIAL_HALF // 2, AXIAL_HALF) * _tile4(sin_a)
    qa_ref[...] = qa.astype(BF16)
    c0 += A_Q_W

    ka = proj[:, c0:c0 + A_KV_W]
    ka = ka * lax.rsqrt(_group_mean_sq(ka, bd[:A_KV_W, :A_KV_W]) + NORM_EPS) * kg_ref[...]
    ka = ka * cos_a + _rot_half(ka, AXIAL_HALF // 2, AXIAL_HALF) * sin_a
    ka_ref[...] = ka.astype(BF16)
    c0 += A_KV_W

    va_ref[0] = proj[:, c0:c0 + A_KV_W].T.astype(BF16)
    c0 += A_KV_W

    qb = proj[:, c0:c0 + B_QK_W]
    qb = qb * _tile4(cos_b) + _rot_half(qb, ROPE_DIMS // 2, HEAD_DIM) * _tile4(sin_b)
    qb_ref[...] = (qb * Q_SCALE).astype(BF16)
    c0 += B_QK_W

    kb = proj[:, c0:c0 + B_QK_W]
    kb = kb * _tile4(cos_b) + _rot_half(kb, ROPE_DIMS // 2, HEAD_DIM) * _tile4(sin_b)
    kb_ref[...] = kb.astype(BF16)
    c0 += B_QK_W

    vb_ref[0] = proj[:, c0:c0 + B_V_W].T.astype(BF16)


def _rope_tables(seq):
    f32 = np.float32
    rows = seq // GRID_W
    row = np.repeat(np.arange(rows, dtype=f32), GRID_W)
    col = np.tile(np.arange(GRID_W, dtype=f32), rows)
    pos = np.arange(seq, dtype=f32)
    inv_ax = (f32(AXIAL_THETA) ** (-np.arange(0, AXIAL_HALF, 2, dtype=f32) / f32(AXIAL_HALF))).astype(f32)
    inv_p = (f32(ROPE_THETA) ** (-np.arange(0, ROPE_DIMS, 2, dtype=f32) / f32(ROPE_DIMS))).astype(f32)
    row_ang = row[:, None] * inv_ax[None, :]
    col_ang = col[:, None] * inv_ax[None, :]
    pos_ang = pos[:, None] * inv_p[None, :]
    cr, sr = np.cos(row_ang), np.sin(row_ang)
    cc, sc = np.cos(col_ang), np.sin(col_ang)
    cp, sp = np.cos(pos_ang), np.sin(pos_ang)
    rest = HEAD_DIM - ROPE_DIMS
    cos_a = np.concatenate([cr, cr, cc, cc], axis=1)
    sin_a = np.concatenate([-sr, sr, -sc, sc], axis=1)
    cos_b = np.concatenate([cp, cp, np.ones((seq, rest), f32)], axis=1)
    sin_b = np.concatenate([-sp, sp, np.zeros((seq, rest), f32)], axis=1)
    two = lambda t: np.concatenate([t, t], axis=1)
    tab = np.concatenate([two(cos_a), two(sin_a), two(cos_b), two(sin_b)], axis=1).astype(f32)
    return jnp.asarray(tab)


def _proj_call(x, g, w_in, qg, kg, tab, bd, tm):
    seq = x.shape[0]
    row = lambda i: (i, 0)
    fix = lambda i: (0, 0)
    n = seq // tm
    rows_out = lambda w: (pl.BlockSpec((tm, w), row), jax.ShapeDtypeStruct((seq, w), BF16))
    cols_out = lambda w: (pl.BlockSpec((1, w, tm), lambda i: (i, 0, 0)), jax.ShapeDtypeStruct((n, w, tm), BF16))
    outs = [rows_out(A_Q_W), rows_out(A_KV_W), cols_out(A_KV_W), rows_out(B_QK_W), rows_out(B_QK_W), cols_out(B_V_W)]
    return pl.pallas_call(
        _proj_kernel,
        grid=(seq // tm,),
        in_specs=[
            pl.BlockSpec((tm, D_MODEL), row),
            pl.BlockSpec((1, D_MODEL), fix),
            pl.BlockSpec((D_MODEL, IN_COLS), fix),
            pl.BlockSpec((1, A_Q_W), fix),
            pl.BlockSpec((1, A_KV_W), fix),
            pl.BlockSpec((tm, 512), row),
            pl.BlockSpec((A_Q_W, A_Q_W), fix),
        ],
        out_specs=[o[0] for o in outs],
        out_shape=[o[1] for o in outs],
        compiler_params=_cparams(("parallel",)),
        name="proj",
    )(x, g, w_in, qg, kg, tab, bd)


SOFTMAX_ROWS = 16
N_SLOTS = 3
STEPS_PER_TRIP = 3
ST_M, ST_ALPHA, ST_NEXT_MAX, ST_L = 0, 1, 2, 8


def _flash_scratch(nq, tk, dv):
    return ([pltpu.VMEM((tk, nq), F32)] * N_SLOTS + [pltpu.VMEM((tk, nq), BF16)] * N_SLOTS
            + [pltpu.VMEM((dv, nq), F32), pltpu.VMEM((16, nq), F32)])


def _flash_cols(qst, k_ref, vt_ref, scratch):
    s_ref, p_ref = scratch[:N_SLOTS], scratch[N_SLOTS:2 * N_SLOTS]
    acc_ref, st_ref = scratch[2 * N_SLOTS:]
    nq = qst.shape[1]
    tk = s_ref[0].shape[0]
    per = tk // vt_ref.shape[2]
    n_chunks = vt_ref.shape[0] // per
    assert n_chunks >= N_SLOTS and tk % SOFTMAX_ROWS == 0
    n_sub = tk // SOFTMAX_ROWS
    row = lambda r, n=1: slice(r, r + n)
    sub = lambda slot, b: s_ref[slot][b * SOFTMAX_ROWS:(b + 1) * SOFTMAX_ROWS, :]

    def scores(j, slot):
        start = j * tk
        start = start if isinstance(start, int) else pl.multiple_of(start, tk)
        s_ref[slot][...] = jnp.dot(k_ref[pl.ds(start, tk), :], qst, preferred_element_type=F32)

    def column_max(slot):
        mx = sub(slot, 0)
        for b in range(1, n_sub):
            mx = jnp.maximum(mx, sub(slot, b))
        st_ref[row(ST_NEXT_MAX), :] = jnp.max(mx, axis=0, keepdims=True)

    def weighted(j, slot, alpha):
        vt = jnp.concatenate([vt_ref[per * j + c] for c in range(per)], axis=1)
        acc_ref[...] = alpha * acc_ref[...] + jnp.dot(vt, p_ref[slot][...], preferred_element_type=F32)

    def step(j, slot, with_scores=True, with_max=True):
        a_prev = st_ref[row(ST_ALPHA), :]
        m_old = st_ref[row(ST_M), :]
        m_new = jnp.maximum(m_old, st_ref[row(ST_NEXT_MAX), :])
        alpha = jnp.exp2(m_old - m_new)
        if with_scores:
            scores(j + 2, (slot + 2) % N_SLOTS)
        psum = jnp.zeros((8, nq), F32)
        for b in range(n_sub):
            p = jnp.exp2(sub(slot, b) - m_new)
            psum = psum + jnp.sum(p.reshape(SOFTMAX_ROWS // 8, 8, nq), axis=0)
            p_ref[slot][b * SOFTMAX_ROWS:(b + 1) * SOFTMAX_ROWS, :] = p.astype(BF16)
        weighted(jnp.maximum(j - 1, 0), (slot + 2) % N_SLOTS, a_prev)
        st_ref[row(ST_L, 8), :] = alpha * st_ref[row(ST_L, 8), :] + psum
        st_ref[row(ST_M), :] = m_new
        st_ref[row(ST_ALPHA), :] = alpha
        if with_max:
            column_max((slot + 1) % N_SLOTS)

    scores(0, 0)
    scores(1, 1)
    st_ref[...] = jnp.zeros_like(st_ref)
    st_ref[row(ST_M), :] = jnp.full((1, nq), -jnp.inf, F32)
    acc_ref[...] = jnp.zeros_like(acc_ref)
    p_ref[N_SLOTS - 1][...] = jnp.zeros((tk, nq), BF16)
    column_max(0)
    n_full = n_chunks - 2
    n_trips = n_full // STEPS_PER_TRIP
    lead = n_full - n_trips * STEPS_PER_TRIP
    for j in range(lead):
        step(j, j % N_SLOTS)

    def body(t, carry):
        for u in range(STEPS_PER_TRIP):
            step(lead + t * STEPS_PER_TRIP + u, (lead + u) % N_SLOTS)
        return carry

    lax.fori_loop(0, n_trips, body, 0)
    step(n_chunks - 2, (n_chunks - 2) % N_SLOTS, with_scores=False)
    step(n_chunks - 1, (n_chunks - 1) % N_SLOTS, with_scores=False, with_max=False)
    weighted(n_chunks - 1, (n_chunks - 1) % N_SLOTS, st_ref[row(ST_ALPHA), :])
    return acc_ref[...], jnp.sum(st_ref[row(ST_L, 8), :], axis=0, keepdims=True)


MXU_COLS = 256
P_LIMIT = 2.0 ** 60


def _flash_scratch_fast(nq, tk, dv):
    return [pltpu.VMEM((tk, nq), BF16)] * N_SLOTS + [pltpu.VMEM((dv, nq), F32), pltpu.VMEM((16, nq), F32)]


def _flash_cols_fast(qst, k_ref, vt_ref, scratch):
    p_ref = scratch[:N_SLOTS]
    acc_ref, st_ref = scratch[N_SLOTS:]
    nq = qst.shape[1]
    tk = p_ref[0].shape[0]
    per = tk // vt_ref.shape[2]
    n_chunks = vt_ref.shape[0] // per
    n_sub = tk // SOFTMAX_ROWS
    row = lambda r, n=1: slice(r, r + n)
    st_scale, st_flag = ST_ALPHA, ST_NEXT_MAX
    q_cols = [qst[:, c:c + MXU_COLS] for c in range(0, nq, MXU_COLS)]

    def keys(j):
        start = j * tk
        start = start if isinstance(start, int) else pl.multiple_of(start, tk)
        return k_ref[pl.ds(start, tk), :]

    def weighted(j, slot, scale):
        vt = jnp.concatenate([vt_ref[per * j + c] for c in range(per)], axis=1)
        acc_ref[...] = (acc_ref[...] + jnp.dot(vt, p_ref[slot][...], preferred_element_type=F32)) * scale

    def step(j, slot):
        scale_prev = st_ref[row(st_scale), :]
        m_ref = st_ref[row(ST_M), :]
        k = keys(j)
        pmaxs, psums = [], []
        for c, qc in enumerate(q_cols):
            cols = slice(c * MXU_COLS, (c + 1) * MXU_COLS)
            s = jnp.dot(k, qc, preferred_element_type=F32)
            mc = m_ref[:, cols]
            pm = None
            psum = jnp.zeros((8, MXU_COLS), F32)
            for b in range(n_sub):
                rows = slice(b * SOFTMAX_ROWS, (b + 1) * SOFTMAX_ROWS)
                p = jnp.exp2(s[rows] - mc)
                pm = p if pm is None else jnp.maximum(pm, p)
                psum = psum + jnp.sum(p.reshape(SOFTMAX_ROWS // 8, 8, MXU_COLS), axis=0)
                p_ref[slot][rows, cols] = p.astype(BF16)
            pmaxs.append(jnp.max(pm, axis=0, keepdims=True))
            psums.append(psum)
        pmax = jnp.concatenate(pmaxs, axis=1)
        weighted(jnp.maximum(j - 1, 0), (slot + 2) % N_SLOTS, scale_prev)
        big = jnp.maximum(pmax, 1.0)
        scale = 1.0 / big
        st_ref[row(ST_L, 8), :] = (st_ref[row(ST_L, 8), :] + jnp.concatenate(psums, axis=1)) * scale
        st_ref[row(ST_M), :] = m_ref + jnp.log2(big)
        st_ref[row(st_scale), :] = scale
        st_ref[row(st_flag), :] = jnp.maximum(st_ref[row(st_flag), :], pmax)

    k0 = keys(0)
    m0 = jnp.concatenate([jnp.max(jnp.dot(k0, qc, preferred_element_type=F32), axis=0, keepdims=True)
                          for qc in q_cols], axis=1)
    st_ref[...] = jnp.zeros_like(st_ref)
    st_ref[row(ST_M), :] = m0
    st_ref[row(st_scale), :] = jnp.ones((1, nq), F32)
    acc_ref[...] = jnp.zeros_like(acc_ref)
    p_ref[N_SLOTS - 1][...] = jnp.zeros((tk, nq), BF16)
    n_trips = n_chunks // STEPS_PER_TRIP
    lead = n_chunks - n_trips * STEPS_PER_TRIP
    for j in range(lead):
        step(j, j % N_SLOTS)

    def body(t, carry):
        for u in range(STEPS_PER_TRIP):
            step(lead + t * STEPS_PER_TRIP + u, (lead + u) % N_SLOTS)
        return carry

    lax.fori_loop(0, n_trips, body, 0)
    weighted(n_chunks - 1, (n_chunks - 1) % N_SLOTS, st_ref[row(st_scale), :])
    return acc_ref[...], jnp.sum(st_ref[row(ST_L, 8), :], axis=0, keepdims=True), st_ref[row(st_flag), :]


def _attend(qst, k_ref, vt_ref, rest, fast):
    if not fast:
        return _flash_cols(qst, k_ref, vt_ref, rest)
    flag_ref, scratch = rest[0], rest[1:]
    acc_t, l, flag = _flash_cols_fast(qst, k_ref, vt_ref, scratch)
    flag_ref[...] = jnp.broadcast_to(flag, flag_ref.shape)
    return acc_t, l


def _attn_outs(out_spec, out_shape, grid_axes, n_i, nq, fast):
    if not fast:
        return out_spec, out_shape
    flag_spec = pl.BlockSpec((8, nq), lambda a, i: (a * n_i + i, 0))
    return [out_spec, flag_spec], [out_shape, jax.ShapeDtypeStruct((grid_axes * n_i * 8, nq), F32)]


def _gqa_kernel(q_ref, k_ref, vt_ref, o_ref, *rest, tq, fast):
    g = pl.program_id(0)
    lane = lax.broadcasted_iota(jnp.int32, (tq, LANES), 1)
    in_g = (lane // HEAD_DIM) == g
    rows = []
    for hh in range(A_GROUP):
        qp = q_ref[:, LANES * (hh // 2):LANES * (hh // 2 + 1)].astype(F32)
        aligned = jnp.where((hh % 2) == g, qp, pltpu.roll(qp, HEAD_DIM, 1))
        rows.append(jnp.where(in_g, aligned, 0.0))
    qs = jnp.concatenate(rows, axis=0)
    acc_t, l = _attend(qs.T.astype(BF16), k_ref, vt_ref, rest, fast)
    o_t = acc_t / l
    o = jnp.concatenate([o_t, jnp.zeros_like(o_t)], axis=0).T
    left_half = lane < HEAD_DIM
    for p in range(A_GROUP // 2):
        a = o[(2 * p) * tq:(2 * p + 1) * tq]
        b = o[(2 * p + 1) * tq:(2 * p + 2) * tq]
        o_ref[:, LANES * p:LANES * (p + 1)] = jnp.where(left_half, a, pltpu.roll(b, HEAD_DIM, 1)).astype(BF16)


def _gqa_call(qa, ka, vat, tq, tk, fast):
    seq = qa.shape[0]
    nq = A_GROUP * tq
    out_specs, out_shape = _attn_outs(pl.BlockSpec((tq, A_GROUP * HEAD_DIM), lambda g, i: (i, g)),
                                      jax.ShapeDtypeStruct((seq, A_Q_W), BF16), A_KV_HEADS, seq // tq, nq, fast)
    return pl.pallas_call(
        functools.partial(_gqa_kernel, tq=tq, fast=fast),
        grid=(A_KV_HEADS, seq // tq),
        in_specs=[
            pl.BlockSpec((tq, A_GROUP * HEAD_DIM), lambda g, i: (i, g)),
            pl.BlockSpec((seq, A_KV_W), lambda g, i: (0, 0)),
            pl.BlockSpec((vat.shape[0], HEAD_DIM, vat.shape[2]), lambda g, i: (0, g, 0)),
        ],
        out_specs=out_specs,
        out_shape=out_shape,
        scratch_shapes=(_flash_scratch_fast if fast else _flash_scratch)(nq, tk, HEAD_DIM),
        compiler_params=_cparams(("parallel", "parallel")),
        name="gqa_fast" if fast else "gqa",
    )(qa, ka, vat)


def _diff_kernel(lam_ref, q_ref, k_ref, vt_ref, sg_ref, o_ref, *rest, tq, lambda_init, fast):
    lane = lax.broadcasted_iota(jnp.int32, (tq, LANES), 1)
    q = q_ref[...].astype(F32)
    qs = jnp.concatenate([jnp.where(lane < HEAD_DIM, q, 0.0),
                          jnp.where(lane >= HEAD_DIM, q, 0.0)], axis=0)
    acc_t, l = _attend(qs.T.astype(BF16), k_ref, vt_ref, rest, fast)
    o = (acc_t / l).T
    lv = lam_ref[...]
    lam = (jnp.exp(jnp.sum(lv[0:1] * lv[1:2], axis=-1, keepdims=True))
           - jnp.exp(jnp.sum(lv[2:3] * lv[3:4], axis=-1, keepdims=True)) + lambda_init)
    ob = o[:tq] - lam * o[tq:]
    ob = _rms(ob, sg_ref[...], SUBLN_EPS) * (1.0 - lambda_init)
    o_ref[...] = ob.astype(BF16)


def _diff_call(lam_vecs, qb, kb, vbt, subln_g, tq, tk, lambda_init, fast):
    seq = qb.shape[0]
    n_slabs, _, slab = vbt.shape
    out_specs, out_shape = _attn_outs(pl.BlockSpec((tq, LANES), lambda h, i: (i, h)),
                                      jax.ShapeDtypeStruct((seq, B_V_W), BF16), B_HEADS, seq // tq, 2 * tq, fast)
    return pl.pallas_call(
        functools.partial(_diff_kernel, tq=tq, lambda_init=lambda_init, fast=fast),
        grid=(B_HEADS, seq // tq),
        in_specs=[
            pl.BlockSpec((4, HEAD_DIM), lambda h, i: (0, 0)),
            pl.BlockSpec((tq, LANES), lambda h, i: (i, h)),
            pl.BlockSpec((seq, LANES), lambda h, i: (0, h)),
            pl.BlockSpec((n_slabs, B_V_DIM, slab), lambda h, i: (0, h, 0)),
            pl.BlockSpec((1, B_V_DIM), lambda h, i: (0, 0)),
        ],
        out_specs=out_specs,
        out_shape=out_shape,
        scratch_shapes=(_flash_scratch_fast if fast else _flash_scratch)(2 * tq, tk, B_V_DIM),
        compiler_params=_cparams(("parallel", "parallel")),
        name="diff_fast" if fast else "diff",
    )(lam_vecs, qb, kb, vbt, subln_g)


def _mid_kernel(oa_ref, ob_ref, x_ref, wo_ref, g_ref, wq_ref, sk_ref, h_ref, xn_ref, sc_ref):
    o = jnp.concatenate([oa_ref[...], ob_ref[...]], axis=1)
    h = x_ref[...] + jnp.dot(o, wo_ref[...], preferred_element_type=F32)
    h_ref[...] = h
    xn = _rms(h, g_ref[...], NORM_EPS).astype(BF16)
    xn_ref[...] = xn
    q = jnp.dot(xn, wq_ref[...], preferred_element_type=F32).astype(BF16)
    for hp in range(2 * PEER_HEADS):
        sc_ref[hp] = lax.dot_general(sk_ref[hp], q[:, PEER_HALF * hp:PEER_HALF * (hp + 1)],
                                     NT_DIMS, preferred_element_type=F32)


def _mid_call(oa, ob, x, w_out, g, w_query, sub_keys, tm):
    seq = x.shape[0]
    row = lambda i: (i, 0)
    fix = lambda i: (0, 0)
    nq = 2 * PEER_HEADS * PEER_HALF
    return pl.pallas_call(
        _mid_kernel,
        grid=(seq // tm,),
        in_specs=[
            pl.BlockSpec((tm, A_Q_W), row),
            pl.BlockSpec((tm, B_V_W), row),
            pl.BlockSpec((tm, D_MODEL), row),
            pl.BlockSpec((D_MODEL, D_MODEL), fix),
            pl.BlockSpec((1, D_MODEL), fix),
            pl.BlockSpec((D_MODEL, nq), fix),
            pl.BlockSpec((2 * PEER_HEADS, PEER_KEYS, PEER_HALF), lambda i: (0, 0, 0)),
        ],
        out_specs=[
            pl.BlockSpec((tm, D_MODEL), row),
            pl.BlockSpec((tm, D_MODEL), row),
            pl.BlockSpec((2 * PEER_HEADS, PEER_KEYS, tm), lambda i: (0, 0, i)),
        ],
        out_shape=[
            jax.ShapeDtypeStruct((seq, D_MODEL), F32),
            jax.ShapeDtypeStruct((seq, D_MODEL), BF16),
            jax.ShapeDtypeStruct((2 * PEER_HEADS, PEER_KEYS, seq), F32),
        ],
        compiler_params=_cparams(("parallel",)),
        name="mid",
    )(oa, ob, x, w_out, g, w_query, sub_keys)


N_TOP = PEER_TOPK + 1
_CAND = [(a, b) for a in range(N_TOP) for b in range(N_TOP) if (a + 1) * (b + 1) <= N_TOP]


def _top_vals(s, n):
    vals = []
    for r in range(n):
        m = jnp.max(s, axis=0, keepdims=True)
        vals.append(m)
        if r + 1 < n:
            s = jnp.where(s >= m, -jnp.inf, s)
    return vals


def _topk_kernel(sc_ref, e1_ref, e2_ref, tau_ref):
    t = sc_ref.shape[-1]
    taus = []
    for h in range(PEER_HEADS):
        s1 = sc_ref[2 * h]
        s2 = sc_ref[2 * h + 1]
        v1 = _top_vals(s1, N_TOP)
        v2 = _top_vals(s2, N_TOP)
        cands = [v1[a] + v2[b] for (a, b) in _CAND]
        pad = (-len(cands)) % 8
        cands += [jnp.full((1, t), -jnp.inf, F32)] * pad
        c = jnp.concatenate(cands, axis=0)
        top = _top_vals(c, N_TOP)
        t16, t17 = top[PEER_TOPK - 1], top[PEER_TOPK]
        best = v1[0] + v2[0]
        z = jnp.sum(jnp.where(c >= t16, jnp.exp(c - best), 0.0), axis=0, keepdims=True)
        rz = 1.0 / z
        e1_ref[h] = jnp.exp(s1 - v1[0]) * rz
        e2_ref[h] = jnp.exp(s2 - v2[0])
        taus.append(0.5 * (jnp.exp(t16 - best) + jnp.exp(t17 - best)) * rz)
    tau_ref[...] = jnp.concatenate(taus, axis=0)


def _topk_call(sc, tt):
    seq = sc.shape[-1]
    blk = lambda i: (0, 0, i)
    return pl.pallas_call(
        _topk_kernel,
        grid=(seq // tt,),
        in_specs=[pl.BlockSpec((2 * PEER_HEADS, PEER_KEYS, tt), blk)],
        out_specs=[
            pl.BlockSpec((PEER_HEADS, PEER_KEYS, tt), blk),
            pl.BlockSpec((PEER_HEADS, PEER_KEYS, tt), blk),
            pl.BlockSpec((PEER_HEADS, tt), lambda i: (0, i)),
        ],
        out_shape=[
            jax.ShapeDtypeStruct((PEER_HEADS, PEER_KEYS, seq), F32),
            jax.ShapeDtypeStruct((PEER_HEADS, PEER_KEYS, seq), F32),
            jax.ShapeDtypeStruct((PEER_HEADS, seq), F32),
        ],
        compiler_params=_cparams(("parallel",)),
        name="topk",
    )(sc)


def _peer_kernel(xn_ref, u_ref, vt_ref, e1_ref, e2_ref, tau_ref, h_ref, g_ref, o_ref,
                 acc_ref, a_ref, w_ref, *, eb):
    e = pl.program_id(1)
    tt = xn_ref.shape[0]
    inv_sqrt2 = 1.0 / math.sqrt(2.0)
    a_ref[...] = lax.dot_general(u_ref[...], xn_ref[...], NT_DIMS,
                                 preferred_element_type=F32)
    for ii in range(eb // PEER_KEYS):
        i = e * (eb // PEER_KEYS) + ii
        rows = slice(PEER_KEYS * ii, PEER_KEYS * (ii + 1))
        e1_rows = [e1_ref[h, pl.ds(i, 1), :] for h in range(PEER_HEADS)]
        for tc in range(tt // LANES):
            cols = slice(LANES * tc, LANES * (tc + 1))
            a = a_ref[rows, cols]
            act = 0.5 * a * (1.0 + lax.erf(a * inv_sqrt2))
            gate = None
            for h in range(PEER_HEADS):
                p = e2_ref[h, :, cols] * e1_rows[h][:, cols]
                term = jnp.where(p >= tau_ref[h:h + 1, cols], p, 0.0)
                gate = term if gate is None else gate + term
            w_ref[rows, cols] = (gate * act).astype(BF16)
    part = jnp.dot(vt_ref[...], w_ref[...], preferred_element_type=F32)

    @pl.when(e == 0)
    def _():
        acc_ref[...] = part

    @pl.when(e > 0)
    def _():
        acc_ref[...] += part

    @pl.when(e == pl.num_programs(1) - 1)
    def _():
        out = h_ref[...] + acc_ref[...].T
        o_ref[...] = _rms(out, g_ref[...], NORM_EPS)


def _peer_call(xn, u, vt, e1, e2, tau, h, g, tt, eb):
    seq = xn.shape[0]
    n_exp = u.shape[0]
    tok = lambda t, e: (t, 0)
    tok3 = lambda t, e: (0, 0, t)
    return pl.pallas_call(
        functools.partial(_peer_kernel, eb=eb),
        grid=(seq // tt, n_exp // eb),
        in_specs=[
            pl.BlockSpec((tt, D_MODEL), tok),
            pl.BlockSpec((eb, D_MODEL), lambda t, e: (e, 0)),
            pl.BlockSpec((D_MODEL, eb), lambda t, e: (0, e)),
            pl.BlockSpec((PEER_HEADS, PEER_KEYS, tt), tok3),
            pl.BlockSpec((PEER_HEADS, PEER_KEYS, tt), tok3),
            pl.BlockSpec((PEER_HEADS, tt), lambda t, e: (0, t)),
            pl.BlockSpec((tt, D_MODEL), tok),
            pl.BlockSpec((1, D_MODEL), lambda t, e: (0, 0)),
        ],
        out_specs=pl.BlockSpec((tt, D_MODEL), tok),
        out_shape=jax.ShapeDtypeStruct((seq, D_MODEL), F32),
        scratch_shapes=[pltpu.VMEM((D_MODEL, tt), F32), pltpu.VMEM((eb, tt), F32), pltpu.VMEM((eb, tt), BF16)],
        compiler_params=_cparams(("parallel", "arbitrary")),
        name="peer",
    )(xn, u, vt, e1, e2, tau, h, g)


def _tiles(seq):
    big = seq >= 4096
    return dict(
        tm=512 if big else 256,
        tq_a=128, tq_b=256,
        tk=1024 if big else 512,
        tt_topk=256, tt_peer=512 if big else 256, eb=2048,
    )


def kernel(x, norm_attn_g, w_in, q_norm_g, k_norm_g, lambda_q1, lambda_k1, lambda_q2, lambda_k2,
           subln_g, w_out, norm_ffn_g, w_query, sub_keys, expert_u, expert_v, norm_final_g):
    batch, seq, d = x.shape
    assert batch == 1 and d == D_MODEL and norm_attn_g.shape[0] == 1
    t = _tiles(seq)
    lambda_init = 0.8 - 0.6 * math.exp(-0.3 * 0)
    x2 = x.reshape(seq, d)

    tab = _rope_tables(seq)
    blk = np.arange(A_Q_W) // HEAD_DIM
    bd = jnp.asarray(np.where(blk[:, None] == blk[None, :], 1.0 / HEAD_DIM, 0.0), dtype=BF16)
    qg = (jnp.tile(q_norm_g[0], A_Q_HEADS) * Q_SCALE).reshape(1, A_Q_W)
    kg = jnp.tile(k_norm_g[0], A_KV_HEADS).reshape(1, A_KV_W)

    qa, ka, vat, qb, kb, vbt = _proj_call(x2, norm_attn_g, w_in[0].astype(BF16), qg, kg, tab, bd, t["tm"])
    lam_vecs = jnp.concatenate([lambda_q1, lambda_k1, lambda_q2, lambda_k2], axis=0)
    gqa = functools.partial(_gqa_call, qa, ka, vat, t["tq_a"], t["tk"])
    diff = functools.partial(_diff_call, lam_vecs, qb, kb, vbt, subln_g, t["tq_b"], t["tk"], lambda_init)
    oa, flag_a = gqa(True)
    ob, flag_b = diff(True)
    trusted = jnp.all(flag_a < P_LIMIT) & jnp.all(flag_b < P_LIMIT)
    oa, ob = lax.cond(trusted, lambda: (oa, ob), lambda: (gqa(False), diff(False)))

    sk = sub_keys[0].reshape(2 * PEER_HEADS, PEER_KEYS, PEER_HALF).astype(BF16)
    h, xn, sc = _mid_call(oa, ob, x2, w_out[0].astype(BF16), norm_ffn_g, w_query[0].astype(BF16), sk, t["tm"])
    e1, e2, tau = _topk_call(sc, t["tt_topk"])
    out = _peer_call(xn, expert_u[0].astype(BF16), expert_v[0].T.astype(BF16), e1, e2, tau, h,
                     norm_final_g.reshape(1, d), t["tt_peer"], t["eb"])
    return out.reshape(batch, seq, d)
```

```python
import functools
import math

import jax
import jax.numpy as jnp
import numpy as np
from jax import lax
from jax.experimental import pallas as pl
from jax.experimental.pallas import tpu as pltpu

F32 = jnp.float32
BF16 = jnp.bfloat16

D_MODEL = 1024
HEAD_DIM = 64
A_Q_HEADS = 8
A_KV_HEADS = 2
A_GROUP = A_Q_HEADS // A_KV_HEADS
B_HEADS = 4
B_V_DIM = 2 * HEAD_DIM
GRID_W = 64
AXIAL_THETA = 10000.0
AXIAL_HALF = HEAD_DIM // 2
ROPE_THETA = 500000.0
ROPE_DIMS = HEAD_DIM // 4
NORM_EPS = 1e-6
SUBLN_EPS = 1e-5
A_Q_W = A_Q_HEADS * HEAD_DIM
A_KV_W = A_KV_HEADS * HEAD_DIM
B_QK_W = 2 * B_HEADS * HEAD_DIM
B_V_W = B_HEADS * B_V_DIM
IN_COLS = A_Q_W + 2 * A_KV_W + 2 * B_QK_W + B_V_W
PEER_HEADS = 8
PEER_KEYS = 128
PEER_HALF = 128
PEER_TOPK = 16
LANES = 128

NT_DIMS = (((1,), (1,)), ((), ()))
Q_SCALE = (HEAD_DIM ** -0.5) * math.log2(math.e)

VMEM_LIMIT = 56 * 1024 * 1024


def _cparams(sem):
    return pltpu.CompilerParams(dimension_semantics=sem, vmem_limit_bytes=VMEM_LIMIT)


def _rms(x, g, eps):
    return x * lax.rsqrt(jnp.mean(x * x, axis=-1, keepdims=True) + eps) * g


def _group_mean_sq(v, bd):
    v2 = v * v
    hi = v2.astype(BF16)
    lo = (v2 - hi.astype(F32)).astype(BF16)
    return (jnp.dot(hi, bd, preferred_element_type=F32)
            + jnp.dot(lo, bd, preferred_element_type=F32))


def _rot_half(v, half, group):
    width = v.shape[-1]
    lane = lax.broadcasted_iota(jnp.int32, v.shape, 1)
    fwd = pltpu.roll(v, width - half, 1)
    bwd = pltpu.roll(v, half, 1)
    return jnp.where((lane % group) < half, fwd, bwd)


def _tile4(t):
    return jnp.concatenate([t, t, t, t], axis=1)


def _proj_kernel(x_ref, g_ref, w_ref, qg_ref, kg_ref, tab_ref, bd_ref,
                 qa_ref, ka_ref, va_ref, qb_ref, kb_ref, vb_ref):
    xn = _rms(x_ref[...], g_ref[...], NORM_EPS)
    proj = jnp.dot(xn.astype(BF16), w_ref[...], preferred_element_type=F32)
    tab = tab_ref[...]
    cos_a, sin_a = tab[:, 0:128], tab[:, 128:256]
    cos_b, sin_b = tab[:, 256:384], tab[:, 384:512]
    bd = bd_ref[...]

    c0 = 0
    qa = proj[:, c0:c0 + A_Q_W]
    qa = qa * lax.rsqrt(_group_mean_sq(qa, bd) + NORM_EPS) * qg_ref[...]
    qa = qa * _tile4(cos_a) + _rot_half(qa, AXIAL_HALF // 2, AXIAL_HALF) * _tile4(sin_a)
    qa_ref[...] = qa.astype(BF16)
    c0 += A_Q_W

    ka = proj[:, c0:c0 + A_KV_W]
    ka = ka * lax.rsqrt(_group_mean_sq(ka, bd[:A_KV_W, :A_KV_W]) + NORM_EPS) * kg_ref[...]
    ka = ka * cos_a + _rot_half(ka, AXIAL_HALF // 2, AXIAL_HALF) * sin_a
    ka_ref[...] = ka.astype(BF16)
    c0 += A_KV_W

    va_ref[0] = proj[:, c0:c0 + A_KV_W].T.astype(BF16)
    c0 += A_KV_W

    qb = proj[:, c0:c0 + B_QK_W]
    qb = qb * _tile4(cos_b) + _rot_half(qb, ROPE_DIMS // 2, HEAD_DIM) * _tile4(sin_b)
    qb_ref[...] = (qb * Q_SCALE).astype(BF16)
    c0 += B_QK_W

    kb = proj[:, c0:c0 + B_QK_W]
    kb = kb * _tile4(cos_b) + _rot_half(kb, ROPE_DIMS // 2, HEAD_DIM) * _tile4(sin_b)
    kb_ref[...] = kb.astype(BF16)
    c0 += B_QK_W

    vb_ref[0] = proj[:, c0:c0 + B_V_W].T.astype(BF16)


def _rope_tables(seq):
    f32 = np.float32
    rows = seq // GRID_W
    row = np.repeat(np.arange(rows, dtype=f32), GRID_W)
    col = np.tile(np.arange(GRID_W, dtype=f32), rows)
    pos = np.arange(seq, dtype=f32)
    inv_ax = (f32(AXIAL_THETA) ** (-np.arange(0, AXIAL_HALF, 2, dtype=f32) / f32(AXIAL_HALF))).astype(f32)
    inv_p = (f32(ROPE_THETA) ** (-np.arange(0, ROPE_DIMS, 2, dtype=f32) / f32(ROPE_DIMS))).astype(f32)
    row_ang = row[:, None] * inv_ax[None, :]
    col_ang = col[:, None] * inv_ax[None, :]
    pos_ang = pos[:, None] * inv_p[None, :]
    cr, sr = np.cos(row_ang), np.sin(row_ang)
    cc, sc = np.cos(col_ang), np.sin(col_ang)
    cp, sp = np.cos(pos_ang), np.sin(pos_ang)
    rest = HEAD_DIM - ROPE_DIMS
    cos_a = np.concatenate([cr, cr, cc, cc], axis=1)
    sin_a = np.concatenate([-sr, sr, -sc, sc], axis=1)
    cos_b = np.concatenate([cp, cp, np.ones((seq, rest), f32)], axis=1)
    sin_b = np.concatenate([-sp, sp, np.zeros((seq, rest), f32)], axis=1)
    two = lambda t: np.concatenate([t, t], axis=1)
    tab = np.concatenate([two(cos_a), two(sin_a), two(cos_b), two(sin_b)], axis=1).astype(f32)
    return jnp.asarray(tab)


def _proj_call(x, g, w_in, qg, kg, tab, bd, tm):
    seq = x.shape[0]
    row = lambda i: (i, 0)
    fix = lambda i: (0, 0)
    n = seq // tm
    rows_out = lambda w: (pl.BlockSpec((tm, w), row), jax.ShapeDtypeStruct((seq, w), BF16))
    cols_out = lambda w: (pl.BlockSpec((1, w, tm), lambda i: (i, 0, 0)), jax.ShapeDtypeStruct((n, w, tm), BF16))
    outs = [rows_out(A_Q_W), rows_out(A_KV_W), cols_out(A_KV_W), rows_out(B_QK_W), rows_out(B_QK_W), cols_out(B_V_W)]
    return pl.pallas_call(
        _proj_kernel,
        grid=(seq // tm,),
        in_specs=[
            pl.BlockSpec((tm, D_MODEL), row),
            pl.BlockSpec((1, D_MODEL), fix),
            pl.BlockSpec((D_MODEL, IN_COLS), fix),
            pl.BlockSpec((1, A_Q_W), fix),
            pl.BlockSpec((1, A_KV_W), fix),
            pl.BlockSpec((tm, 512), row),
            pl.BlockSpec((A_Q_W, A_Q_W), fix),
        ],
        out_specs=[o[0] for o in outs],
        out_shape=[o[1] for o in outs],
        compiler_params=_cparams(("parallel",)),
        name="proj",
    )(x, g, w_in, qg, kg, tab, bd)


SOFTMAX_ROWS = 16
N_SLOTS = 3
STEPS_PER_TRIP = 3
ST_M, ST_ALPHA, ST_NEXT_MAX, ST_L = 0, 1, 2, 8


def _flash_scratch(nq, tk, dv):
    return ([pltpu.VMEM((tk, nq), F32)] * N_SLOTS + [pltpu.VMEM((tk, nq), BF16)] * N_SLOTS
            + [pltpu.VMEM((dv, nq), F32), pltpu.VMEM((16, nq), F32)])


def _flash_cols(qst, k_ref, vt_ref, scratch):
    s_ref, p_ref = scratch[:N_SLOTS], scratch[N_SLOTS:2 * N_SLOTS]
    acc_ref, st_ref = scratch[2 * N_SLOTS:]
    nq = qst.shape[1]
    tk = s_ref[0].shape[0]
    per = tk // vt_ref.shape[2]
    n_chunks = vt_ref.shape[0] // per
    assert n_chunks >= N_SLOTS and tk % SOFTMAX_ROWS == 0
    n_sub = tk // SOFTMAX_ROWS
    row = lambda r, n=1: slice(r, r + n)
    sub = lambda slot, b: s_ref[slot][b * SOFTMAX_ROWS:(b + 1) * SOFTMAX_ROWS, :]

    def scores(j, slot):
        start = j * tk
        start = start if isinstance(start, int) else pl.multiple_of(start, tk)
        s_ref[slot][...] = jnp.dot(k_ref[pl.ds(start, tk), :], qst, preferred_element_type=F32)

    def column_max(slot):
        mx = sub(slot, 0)
        for b in range(1, n_sub):
            mx = jnp.maximum(mx, sub(slot, b))
        st_ref[row(ST_NEXT_MAX), :] = jnp.max(mx, axis=0, keepdims=True)

    def weighted(j, slot, alpha):
        vt = jnp.concatenate([vt_ref[per * j + c] for c in range(per)], axis=1)
        acc_ref[...] = alpha * acc_ref[...] + jnp.dot(vt, p_ref[slot][...], preferred_element_type=F32)

    def step(j, slot, with_scores=True, with_max=True):
        a_prev = st_ref[row(ST_ALPHA), :]
        m_old = st_ref[row(ST_M), :]
        m_new = jnp.maximum(m_old, st_ref[row(ST_NEXT_MAX), :])
        alpha = jnp.exp2(m_old - m_new)
        if with_scores:
            scores(j + 2, (slot + 2) % N_SLOTS)
        psum = jnp.zeros((8, nq), F32)
        for b in range(n_sub):
            p = jnp.exp2(sub(slot, b) - m_new)
            psum = psum + jnp.sum(p.reshape(SOFTMAX_ROWS // 8, 8, nq), axis=0)
            p_ref[slot][b * SOFTMAX_ROWS:(b + 1) * SOFTMAX_ROWS, :] = p.astype(BF16)
        weighted(jnp.maximum(j - 1, 0), (slot + 2) % N_SLOTS, a_prev)
        st_ref[row(ST_L, 8), :] = alpha * st_ref[row(ST_L, 8), :] + psum
        st_ref[row(ST_M), :] = m_new
        st_ref[row(ST_ALPHA), :] = alpha
        if with_max:
            column_max((slot + 1) % N_SLOTS)

    scores(0, 0)
    scores(1, 1)
    st_ref[...] = jnp.zeros_like(st_ref)
    st_ref[row(ST_M), :] = jnp.full((1, nq), -jnp.inf, F32)
    acc_ref[...] = jnp.zeros_like(acc_ref)
    p_ref[N_SLOTS - 1][...] = jnp.zeros((tk, nq), BF16)
    column_max(0)
    n_full = n_chunks - 2
    n_trips = n_full // STEPS_PER_TRIP
    lead = n_full - n_trips * STEPS_PER_TRIP
    for j in range(lead):
        step(j, j % N_SLOTS)

    def body(t, carry):
        for u in range(STEPS_PER_TRIP):
            step(lead + t * STEPS_PER_TRIP + u, (lead + u) % N_SLOTS)
        return carry

    lax.fori_loop(0, n_trips, body, 0)
    step(n_chunks - 2, (n_chunks - 2) % N_SLOTS, with_scores=False)
    step(n_chunks - 1, (n_chunks - 1) % N_SLOTS, with_scores=False, with_max=False)
    weighted(n_chunks - 1, (n_chunks - 1) % N_SLOTS, st_ref[row(ST_ALPHA), :])
    return acc_ref[...], jnp.sum(st_ref[row(ST_L, 8), :], axis=0, keepdims=True)


MXU_COLS = 256
KEY_PARTS = 2
P_LIMIT = 2.0 ** 60


def _flash_scratch_fast(nq, tk, dv):
    return [pltpu.VMEM((tk, nq), BF16)] * N_SLOTS + [pltpu.VMEM((dv, nq), F32), pltpu.VMEM((16, nq), F32)]


def _flash_cols_fast(qst, k_ref, vt_ref, scratch):
    p_ref = scratch[:N_SLOTS]
    acc_ref, st_ref = scratch[N_SLOTS:]
    nq = qst.shape[1]
    tk = p_ref[0].shape[0]
    per = tk // vt_ref.shape[2]
    n_chunks = vt_ref.shape[0] // per
    n_sub = tk // SOFTMAX_ROWS
    row = lambda r, n=1: slice(r, r + n)
    st_scale, st_flag = ST_ALPHA, ST_NEXT_MAX
    q_cols = [qst[:, c:c + MXU_COLS] for c in range(0, nq, MXU_COLS)]

    def keys(j):
        start = j * tk
        start = start if isinstance(start, int) else pl.multiple_of(start, tk)
        return k_ref[pl.ds(start, tk), :]

    def weighted(j, slot, scale):
        vt = jnp.concatenate([vt_ref[per * j + c] for c in range(per)], axis=1)
        acc_ref[...] = (acc_ref[...] + jnp.dot(vt, p_ref[slot][...], preferred_element_type=F32)) * scale

    def step(j, slot):
        scale_prev = st_ref[row(st_scale), :]
        m_ref = st_ref[row(ST_M), :]
        k = keys(j)
        pmaxs, psums = [], []
        for c, qc in enumerate(q_cols):
            cols = slice(c * MXU_COLS, (c + 1) * MXU_COLS)
            mc = m_ref[:, cols]
            pm = None
            psum = jnp.zeros((8, MXU_COLS), F32)
            for part in range(KEY_PARTS):
                r0 = part * (tk // KEY_PARTS)
                s = jnp.dot(k[r0:r0 + tk // KEY_PARTS], qc, preferred_element_type=F32)
                for b in range(n_sub // KEY_PARTS):
                    rows = slice(b * SOFTMAX_ROWS, (b + 1) * SOFTMAX_ROWS)
                    p = jnp.exp2(s[rows] - mc)
                    pm = p if pm is None else jnp.maximum(pm, p)
                    psum = psum + jnp.sum(p.reshape(SOFTMAX_ROWS // 8, 8, MXU_COLS), axis=0)
                    p_ref[slot][r0 + b * SOFTMAX_ROWS:r0 + (b + 1) * SOFTMAX_ROWS, cols] = p.astype(BF16)
            pmaxs.append(jnp.max(pm, axis=0, keepdims=True))
            psums.append(psum)
        pmax = jnp.concatenate(pmaxs, axis=1)
        weighted(jnp.maximum(j - 1, 0), (slot + 2) % N_SLOTS, scale_prev)
        big = jnp.maximum(pmax, 1.0)
        scale = 1.0 / big
        st_ref[row(ST_L, 8), :] = (st_ref[row(ST_L, 8), :] + jnp.concatenate(psums, axis=1)) * scale
        st_ref[row(ST_M), :] = m_ref + jnp.log2(big)
        st_ref[row(st_scale), :] = scale
        st_ref[row(st_flag), :] = jnp.maximum(st_ref[row(st_flag), :], pmax)

    k0 = keys(0)
    m0 = jnp.concatenate([jnp.max(jnp.dot(k0, qc, preferred_element_type=F32), axis=0, keepdims=True)
                          for qc in q_cols], axis=1)
    st_ref[...] = jnp.zeros_like(st_ref)
    st_ref[row(ST_M), :] = m0
    st_ref[row(st_scale), :] = jnp.ones((1, nq), F32)
    acc_ref[...] = jnp.zeros_like(acc_ref)
    p_ref[N_SLOTS - 1][...] = jnp.zeros((tk, nq), BF16)
    n_trips = n_chunks // STEPS_PER_TRIP
    lead = n_chunks - n_trips * STEPS_PER_TRIP
    for j in range(lead):
        step(j, j % N_SLOTS)

    def body(t, carry):
        for u in range(STEPS_PER_TRIP):
            step(lead + t * STEPS_PER_TRIP + u, (lead + u) % N_SLOTS)
        return carry

    lax.fori_loop(0, n_trips, body, 0)
    weighted(n_chunks - 1, (n_chunks - 1) % N_SLOTS, st_ref[row(st_scale), :])
    return acc_ref[...], jnp.sum(st_ref[row(ST_L, 8), :], axis=0, keepdims=True), st_ref[row(st_flag), :]


def _attend(qst, k_ref, vt_ref, rest, fast):
    if not fast:
        return _flash_cols(qst, k_ref, vt_ref, rest)
    flag_ref, scratch = rest[0], rest[1:]
    acc_t, l, flag = _flash_cols_fast(qst, k_ref, vt_ref, scratch)
    flag_ref[...] = jnp.broadcast_to(flag, flag_ref.shape)
    return acc_t, l


def _attn_outs(out_spec, out_shape, grid_axes, n_i, nq, fast):
    if not fast:
        return out_spec, out_shape
    flag_spec = pl.BlockSpec((8, nq), lambda a, i: (a * n_i + i, 0))
    return [out_spec, flag_spec], [out_shape, jax.ShapeDtypeStruct((grid_axes * n_i * 8, nq), F32)]


def _gqa_kernel(q_ref, k_ref, vt_ref, o_ref, *rest, tq, fast):
    g = pl.program_id(0)
    lane = lax.broadcasted_iota(jnp.int32, (tq, LANES), 1)
    in_g = (lane // HEAD_DIM) == g
    rows = []
    for hh in range(A_GROUP):
        qp = q_ref[:, LANES * (hh // 2):LANES * (hh // 2 + 1)].astype(F32)
        aligned = jnp.where((hh % 2) == g, qp, pltpu.roll(qp, HEAD_DIM, 1))
        rows.append(jnp.where(in_g, aligned, 0.0))
    qs = jnp.concatenate(rows, axis=0)
    acc_t, l = _attend(qs.T.astype(BF16), k_ref, vt_ref, rest, fast)
    o_t = acc_t / l
    o = jnp.concatenate([o_t, jnp.zeros_like(o_t)], axis=0).T
    left_half = lane < HEAD_DIM
    for p in range(A_GROUP // 2):
        a = o[(2 * p) * tq:(2 * p + 1) * tq]
        b = o[(2 * p + 1) * tq:(2 * p + 2) * tq]
        o_ref[:, LANES * p:LANES * (p + 1)] = jnp.where(left_half, a, pltpu.roll(b, HEAD_DIM, 1)).astype(BF16)


def _gqa_call(qa, ka, vat, tq, tk, fast):
    seq = qa.shape[0]
    nq = A_GROUP * tq
    out_specs, out_shape = _attn_outs(pl.BlockSpec((tq, A_GROUP * HEAD_DIM), lambda g, i: (i, g)),
                                      jax.ShapeDtypeStruct((seq, A_Q_W), BF16), A_KV_HEADS, seq // tq, nq, fast)
    return pl.pallas_call(
        functools.partial(_gqa_kernel, tq=tq, fast=fast),
        grid=(A_KV_HEADS, seq // tq),
        in_specs=[
            pl.BlockSpec((tq, A_GROUP * HEAD_DIM), lambda g, i: (i, g)),
            pl.BlockSpec((seq, A_KV_W), lambda g, i: (0, 0)),
            pl.BlockSpec((vat.shape[0], HEAD_DIM, vat.shape[2]), lambda g, i: (0, g, 0)),
        ],
        out_specs=out_specs,
        out_shape=out_shape,
        scratch_shapes=(_flash_scratch_fast if fast else _flash_scratch)(nq, tk, HEAD_DIM),
        compiler_params=_cparams(("parallel", "parallel")),
        name="gqa_fast" if fast else "gqa",
    )(qa, ka, vat)


def _diff_kernel(lam_ref, q_ref, k_ref, vt_ref, sg_ref, o_ref, *rest, tq, lambda_init, fast):
    lane = lax.broadcasted_iota(jnp.int32, (tq, LANES), 1)
    q = q_ref[...].astype(F32)
    qs = jnp.concatenate([jnp.where(lane < HEAD_DIM, q, 0.0),
                          jnp.where(lane >= HEAD_DIM, q, 0.0)], axis=0)
    acc_t, l = _attend(qs.T.astype(BF16), k_ref, vt_ref, rest, fast)
    o = (acc_t / l).T
    lv = lam_ref[...]
    lam = (jnp.exp(jnp.sum(lv[0:1] * lv[1:2], axis=-1, keepdims=True))
           - jnp.exp(jnp.sum(lv[2:3] * lv[3:4], axis=-1, keepdims=True)) + lambda_init)
    ob = o[:tq] - lam * o[tq:]
    ob = _rms(ob, sg_ref[...], SUBLN_EPS) * (1.0 - lambda_init)
    o_ref[...] = ob.astype(BF16)


def _diff_call(lam_vecs, qb, kb, vbt, subln_g, tq, tk, lambda_init, fast):
    seq = qb.shape[0]
    n_slabs, _, slab = vbt.shape
    out_specs, out_shape = _attn_outs(pl.BlockSpec((tq, LANES), lambda h, i: (i, h)),
                                      jax.ShapeDtypeStruct((seq, B_V_W), BF16), B_HEADS, seq // tq, 2 * tq, fast)
    return pl.pallas_call(
        functools.partial(_diff_kernel, tq=tq, lambda_init=lambda_init, fast=fast),
        grid=(B_HEADS, seq // tq),
        in_specs=[
            pl.BlockSpec((4, HEAD_DIM), lambda h, i: (0, 0)),
            pl.BlockSpec((tq, LANES), lambda h, i: (i, h)),
            pl.BlockSpec((seq, LANES), lambda h, i: (0, h)),
            pl.BlockSpec((n_slabs, B_V_DIM, slab), lambda h, i: (0, h, 0)),
            pl.BlockSpec((1, B_V_DIM), lambda h, i: (0, 0)),
        ],
        out_specs=out_specs,
        out_shape=out_shape,
        scratch_shapes=(_flash_scratch_fast if fast else _flash_scratch)(2 * tq, tk, B_V_DIM),
        compiler_params=_cparams(("parallel", "parallel")),
        name="diff_fast" if fast else "diff",
    )(lam_vecs, qb, kb, vbt, subln_g)


def _mid_kernel(oa_ref, ob_ref, x_ref, wo_ref, g_ref, wq_ref, sk_ref, h_ref, xn_ref, sc_ref):
    o = jnp.concatenate([oa_ref[...], ob_ref[...]], axis=1)
    h = x_ref[...] + jnp.dot(o, wo_ref[...], preferred_element_type=F32)
    h_ref[...] = h
    xn = _rms(h, g_ref[...], NORM_EPS).astype(BF16)
    xn_ref[...] = xn
    q = jnp.dot(xn, wq_ref[...], preferred_element_type=F32).astype(BF16)
    for hp in range(2 * PEER_HEADS):
        sc_ref[hp] = lax.dot_general(sk_ref[hp], q[:, PEER_HALF * hp:PEER_HALF * (hp + 1)],
                                     NT_DIMS, preferred_element_type=F32)


def _mid_call(oa, ob, x, w_out, g, w_query, sub_keys, tm):
    seq = x.shape[0]
    row = lambda i: (i, 0)
    fix = lambda i: (0, 0)
    nq = 2 * PEER_HEADS * PEER_HALF
    return pl.pallas_call(
        _mid_kernel,
        grid=(seq // tm,),
        in_specs=[
            pl.BlockSpec((tm, A_Q_W), row),
            pl.BlockSpec((tm, B_V_W), row),
            pl.BlockSpec((tm, D_MODEL), row),
            pl.BlockSpec((D_MODEL, D_MODEL), fix),
            pl.BlockSpec((1, D_MODEL), fix),
            pl.BlockSpec((D_MODEL, nq), fix),
            pl.BlockSpec((2 * PEER_HEADS, PEER_KEYS, PEER_HALF), lambda i: (0, 0, 0)),
        ],
        out_specs=[
            pl.BlockSpec((tm, D_MODEL), row),
            pl.BlockSpec((tm, D_MODEL), row),
            pl.BlockSpec((2 * PEER_HEADS, PEER_KEYS, tm), lambda i: (0, 0, i)),
        ],
        out_shape=[
            jax.ShapeDtypeStruct((seq, D_MODEL), F32),
            jax.ShapeDtypeStruct((seq, D_MODEL), BF16),
            jax.ShapeDtypeStruct((2 * PEER_HEADS, PEER_KEYS, seq), F32),
        ],
        compiler_params=_cparams(("parallel",)),
        name="mid",
    )(oa, ob, x, w_out, g, w_query, sub_keys)


N_TOP = PEER_TOPK + 1
_CAND = [(a, b) for a in range(N_TOP) for b in range(N_TOP) if (a + 1) * (b + 1) <= N_TOP]


def _top_vals(s, n):
    vals = []
    for r in range(n):
        m = jnp.max(s, axis=0, keepdims=True)
        vals.append(m)
        if r + 1 < n:
            s = jnp.where(s >= m, -jnp.inf, s)
    return vals


def _topk_kernel(sc_ref, e1_ref, e2_ref, tau_ref):
    t = sc_ref.shape[-1]
    taus = []
    for h in range(PEER_HEADS):
        s1 = sc_ref[2 * h]
        s2 = sc_ref[2 * h + 1]
        v1 = _top_vals(s1, N_TOP)
        v2 = _top_vals(s2, N_TOP)
        cands = [v1[a] + v2[b] for (a, b) in _CAND]
        pad = (-len(cands)) % 8
        cands += [jnp.full((1, t), -jnp.inf, F32)] * pad
        c = jnp.concatenate(cands, axis=0)
        top = _top_vals(c, N_TOP)
        t16, t17 = top[PEER_TOPK - 1], top[PEER_TOPK]
        best = v1[0] + v2[0]
        z = jnp.sum(jnp.where(c >= t16, jnp.exp(c - best), 0.0), axis=0, keepdims=True)
        rz = 1.0 / z
        e1_ref[h] = jnp.exp(s1 - v1[0]) * rz
        e2_ref[h] = jnp.exp(s2 - v2[0])
        taus.append(0.5 * (jnp.exp(t16 - best) + jnp.exp(t17 - best)) * rz)
    tau_ref[...] = jnp.concatenate(taus, axis=0)


def _topk_call(sc, tt):
    seq = sc.shape[-1]
    blk = lambda i: (0, 0, i)
    return pl.pallas_call(
        _topk_kernel,
        grid=(seq // tt,),
        in_specs=[pl.BlockSpec((2 * PEER_HEADS, PEER_KEYS, tt), blk)],
        out_specs=[
            pl.BlockSpec((PEER_HEADS, PEER_KEYS, tt), blk),
            pl.BlockSpec((PEER_HEADS, PEER_KEYS, tt), blk),
            pl.BlockSpec((PEER_HEADS, tt), lambda i: (0, i)),
        ],
        out_shape=[
            jax.ShapeDtypeStruct((PEER_HEADS, PEER_KEYS, seq), F32),
            jax.ShapeDtypeStruct((PEER_HEADS, PEER_KEYS, seq), F32),
            jax.ShapeDtypeStruct((PEER_HEADS, seq), F32),
        ],
        compiler_params=_cparams(("parallel",)),
        name="topk",
    )(sc)


def _peer_kernel(xn_ref, u_ref, vt_ref, e1_ref, e2_ref, tau_ref, h_ref, g_ref, o_ref,
                 acc_ref, a_ref, w_ref, *, eb):
    e = pl.program_id(1)
    tt = xn_ref.shape[0]
    inv_sqrt2 = 1.0 / math.sqrt(2.0)
    a_ref[...] = lax.dot_general(u_ref[...], xn_ref[...], NT_DIMS,
                                 preferred_element_type=F32)
    for ii in range(eb // PEER_KEYS):
        i = e * (eb // PEER_KEYS) + ii
        rows = slice(PEER_KEYS * ii, PEER_KEYS * (ii + 1))
        e1_rows = [e1_ref[h, pl.ds(i, 1), :] for h in range(PEER_HEADS)]
        for tc in range(tt // LANES):
            cols = slice(LANES * tc, LANES * (tc + 1))
            a = a_ref[rows, cols]
            act = 0.5 * a * (1.0 + lax.erf(a * inv_sqrt2))
            gate = None
            for h in range(PEER_HEADS):
                p = e2_ref[h, :, cols] * e1_rows[h][:, cols]
                term = jnp.where(p >= tau_ref[h:h + 1, cols], p, 0.0)
                gate = term if gate is None else gate + term
            w_ref[rows, cols] = (gate * act).astype(BF16)
    part = jnp.dot(vt_ref[...], w_ref[...], preferred_element_type=F32)

    @pl.when(e == 0)
    def _():
        acc_ref[...] = part

    @pl.when(e > 0)
    def _():
        acc_ref[...] += part

    @pl.when(e == pl.num_programs(1) - 1)
    def _():
        out = h_ref[...] + acc_ref[...].T
        o_ref[...] = _rms(out, g_ref[...], NORM_EPS)


def _peer_call(xn, u, vt, e1, e2, tau, h, g, tt, eb):
    seq = xn.shape[0]
    n_exp = u.shape[0]
    tok = lambda t, e: (t, 0)
    tok3 = lambda t, e: (0, 0, t)
    return pl.pallas_call(
        functools.partial(_peer_kernel, eb=eb),
        grid=(seq // tt, n_exp // eb),
        in_specs=[
            pl.BlockSpec((tt, D_MODEL), tok),
            pl.BlockSpec((eb, D_MODEL), lambda t, e: (e, 0)),
            pl.BlockSpec((D_MODEL, eb), lambda t, e: (0, e)),
            pl.BlockSpec((PEER_HEADS, PEER_KEYS, tt), tok3),
            pl.BlockSpec((PEER_HEADS, PEER_KEYS, tt), tok3),
            pl.BlockSpec((PEER_HEADS, tt), lambda t, e: (0, t)),
            pl.BlockSpec((tt, D_MODEL), tok),
            pl.BlockSpec((1, D_MODEL), lambda t, e: (0, 0)),
        ],
        out_specs=pl.BlockSpec((tt, D_MODEL), tok),
        out_shape=jax.ShapeDtypeStruct((seq, D_MODEL), F32),
        scratch_shapes=[pltpu.VMEM((D_MODEL, tt), F32), pltpu.VMEM((eb, tt), F32), pltpu.VMEM((eb, tt), BF16)],
        compiler_params=_cparams(("parallel", "arbitrary")),
        name="peer",
    )(xn, u, vt, e1, e2, tau, h, g)


def _tiles(seq):
    big = seq >= 4096
    return dict(
        tm=512 if big else 256,
        tq_a=128, tq_b=256,
        tk=1024 if big else 512,
        tt_topk=256, tt_peer=512 if big else 256, eb=2048,
    )


def kernel(x, norm_attn_g, w_in, q_norm_g, k_norm_g, lambda_q1, lambda_k1, lambda_q2, lambda_k2,
           subln_g, w_out, norm_ffn_g, w_query, sub_keys, expert_u, expert_v, norm_final_g):
    batch, seq, d = x.shape
    assert batch == 1 and d == D_MODEL and norm_attn_g.shape[0] == 1
    t = _tiles(seq)
    lambda_init = 0.8 - 0.6 * math.exp(-0.3 * 0)
    x2 = x.reshape(seq, d)

    tab = _rope_tables(seq)
    blk = np.arange(A_Q_W) // HEAD_DIM
    bd = jnp.asarray(np.where(blk[:, None] == blk[None, :], 1.0 / HEAD_DIM, 0.0), dtype=BF16)
    qg = (jnp.tile(q_norm_g[0], A_Q_HEADS) * Q_SCALE).reshape(1, A_Q_W)
    kg = jnp.tile(k_norm_g[0], A_KV_HEADS).reshape(1, A_KV_W)

    qa, ka, vat, qb, kb, vbt = _proj_call(x2, norm_attn_g, w_in[0].astype(BF16), qg, kg, tab, bd, t["tm"])
    lam_vecs = jnp.concatenate([lambda_q1, lambda_k1, lambda_q2, lambda_k2], axis=0)
    gqa = functools.partial(_gqa_call, qa, ka, vat, t["tq_a"], t["tk"])
    diff = functools.partial(_diff_call, lam_vecs, qb, kb, vbt, subln_g, t["tq_b"], t["tk"], lambda_init)
    oa, flag_a = gqa(True)
    ob, flag_b = diff(True)
    trusted = jnp.all(flag_a < P_LIMIT) & jnp.all(flag_b < P_LIMIT)
    oa, ob = lax.cond(trusted, lambda: (oa, ob), lambda: (gqa(False), diff(False)))

    sk = sub_keys[0].reshape(2 * PEER_HEADS, PEER_KEYS, PEER_HALF).astype(BF16)
    h, xn, sc = _mid_call(oa, ob, x2, w_out[0].astype(BF16), norm_ffn_g, w_query[0].astype(BF16), sk, t["tm"])
    e1, e2, tau = _topk_call(sc, t["tt_topk"])
    out = _peer_call(xn, expert_u[0].astype(BF16), expert_v[0].T.astype(BF16), e1, e2, tau, h,
                     norm_final_g.reshape(1, d), t["tt_peer"], t["eb"])
    return out.reshape(batch, seq, d)
```

```python
import functools
import math

import jax
import jax.numpy as jnp
import numpy as np
from jax import lax
from jax.experimental import pallas as pl
from jax.experimental.pallas import tpu as pltpu

F32 = jnp.float32
BF16 = jnp.bfloat16

D_MODEL = 1024
HEAD_DIM = 64
A_Q_HEADS = 8
A_KV_HEADS = 2
A_GROUP = A_Q_HEADS // A_KV_HEADS
B_HEADS = 4
B_V_DIM = 2 * HEAD_DIM
GRID_W = 64
AXIAL_THETA = 10000.0
AXIAL_HALF = HEAD_DIM // 2
ROPE_THETA = 500000.0
ROPE_DIMS = HEAD_DIM // 4
NORM_EPS = 1e-6
SUBLN_EPS = 1e-5
A_Q_W = A_Q_HEADS * HEAD_DIM
A_KV_W = A_KV_HEADS * HEAD_DIM
B_QK_W = 2 * B_HEADS * HEAD_DIM
B_V_W = B_HEADS * B_V_DIM
IN_COLS = A_Q_W + 2 * A_KV_W + 2 * B_QK_W + B_V_W
PEER_HEADS = 8
PEER_KEYS = 128
PEER_HALF = 128
PEER_TOPK = 16
LANES = 128

NT_DIMS = (((1,), (1,)), ((), ()))
Q_SCALE = (HEAD_DIM ** -0.5) * math.log2(math.e)

VMEM_LIMIT = 56 * 1024 * 1024


def _cparams(sem):
    return pltpu.CompilerParams(dimension_semantics=sem, vmem_limit_bytes=VMEM_LIMIT)


def _rms(x, g, eps):
    return x * lax.rsqrt(jnp.mean(x * x, axis=-1, keepdims=True) + eps) * g


def _group_mean_sq(v, bd):
    v2 = v * v
    hi = v2.astype(BF16)
    lo = (v2 - hi.astype(F32)).astype(BF16)
    return (jnp.dot(hi, bd, preferred_element_type=F32)
            + jnp.dot(lo, bd, preferred_element_type=F32))


def _rot_half(v, half, group):
    width = v.shape[-1]
    lane = lax.broadcasted_iota(jnp.int32, v.shape, 1)
    fwd = pltpu.roll(v, width - half, 1)
    bwd = pltpu.roll(v, half, 1)
    return jnp.where((lane % group) < half, fwd, bwd)


def _tile4(t):
    return jnp.concatenate([t, t, t, t], axis=1)


def _proj_kernel(x_ref, g_ref, w_ref, qg_ref, kg_ref, tab_ref, bd_ref,
                 qa_ref, ka_ref, va_ref, qb_ref, kb_ref, vb_ref):
    xn = _rms(x_ref[...], g_ref[...], NORM_EPS)
    proj = jnp.dot(xn.astype(BF16), w_ref[...], preferred_element_type=F32)
    tab = tab_ref[...]
    cos_a, sin_a = tab[:, 0:128], tab[:, 128:256]
    cos_b, sin_b = tab[:, 256:384], tab[:, 384:512]
    bd = bd_ref[...]

    c0 = 0
    qa = proj[:, c0:c0 + A_Q_W]
    qa = qa * lax.rsqrt(_group_mean_sq(qa, bd) + NORM_EPS) * qg_ref[...]
    qa = qa * _tile4(cos_a) + _rot_half(qa, AXIAL_HALF // 2, AXIAL_HALF) * _tile4(sin_a)
    qa_ref[...] = qa.astype(BF16)
    c0 += A_Q_W

    ka = proj[:, c0:c0 + A_KV_W]
    ka = ka * lax.rsqrt(_group_mean_sq(ka, bd[:A_KV_W, :A_KV_W]) + NORM_EPS) * kg_ref[...]
    ka = ka * cos_a + _rot_half(ka, AXIAL_HALF // 2, AXIAL_HALF) * sin_a
    ka_ref[...] = ka.astype(BF16)
    c0 += A_KV_W

    va_ref[0] = proj[:, c0:c0 + A_KV_W].T.astype(BF16)
    c0 += A_KV_W

    qb = proj[:, c0:c0 + B_QK_W]
    qb = qb * _tile4(cos_b) + _rot_half(qb, ROPE_DIMS // 2, HEAD_DIM) * _tile4(sin_b)
    qb_ref[...] = (qb * Q_SCALE).astype(BF16)
    c0 += B_QK_W

    kb = proj[:, c0:c0 + B_QK_W]
    kb = kb * _tile4(cos_b) + _rot_half(kb, ROPE_DIMS // 2, HEAD_DIM) * _tile4(sin_b)
    kb_ref[...] = kb.astype(BF16)
    c0 += B_QK_W

    vb_ref[0] = proj[:, c0:c0 + B_V_W].T.astype(BF16)


def _rope_tables(seq):
    f32 = np.float32
    rows = seq // GRID_W
    row = np.repeat(np.arange(rows, dtype=f32), GRID_W)
    col = np.tile(np.arange(GRID_W, dtype=f32), rows)
    pos = np.arange(seq, dtype=f32)
    inv_ax = (f32(AXIAL_THETA) ** (-np.arange(0, AXIAL_HALF, 2, dtype=f32) / f32(AXIAL_HALF))).astype(f32)
    inv_p = (f32(ROPE_THETA) ** (-np.arange(0, ROPE_DIMS, 2, dtype=f32) / f32(ROPE_DIMS))).astype(f32)
    row_ang = row[:, None] * inv_ax[None, :]
    col_ang = col[:, None] * inv_ax[None, :]
    pos_ang = pos[:, None] * inv_p[None, :]
    cr, sr = np.cos(row_ang), np.sin(row_ang)
    cc, sc = np.cos(col_ang), np.sin(col_ang)
    cp, sp = np.cos(pos_ang), np.sin(pos_ang)
    rest = HEAD_DIM - ROPE_DIMS
    cos_a = np.concatenate([cr, cr, cc, cc], axis=1)
    sin_a = np.concatenate([-sr, sr, -sc, sc], axis=1)
    cos_b = np.concatenate([cp, cp, np.ones((seq, rest), f32)], axis=1)
    sin_b = np.concatenate([-sp, sp, np.zeros((seq, rest), f32)], axis=1)
    two = lambda t: np.concatenate([t, t], axis=1)
    tab = np.concatenate([two(cos_a), two(sin_a), two(cos_b), two(sin_b)], axis=1).astype(f32)
    return jnp.asarray(tab)


def _proj_call(x, g, w_in, qg, kg, tab, bd, tm):
    seq = x.shape[0]
    row = lambda i: (i, 0)
    fix = lambda i: (0, 0)
    n = seq // tm
    rows_out = lambda w: (pl.BlockSpec((tm, w), row), jax.ShapeDtypeStruct((seq, w), BF16))
    cols_out = lambda w: (pl.BlockSpec((1, w, tm), lambda i: (i, 0, 0)), jax.ShapeDtypeStruct((n, w, tm), BF16))
    outs = [rows_out(A_Q_W), rows_out(A_KV_W), cols_out(A_KV_W), rows_out(B_QK_W), rows_out(B_QK_W), cols_out(B_V_W)]
    return pl.pallas_call(
        _proj_kernel,
        grid=(seq // tm,),
        in_specs=[
            pl.BlockSpec((tm, D_MODEL), row),
            pl.BlockSpec((1, D_MODEL), fix),
            pl.BlockSpec((D_MODEL, IN_COLS), fix),
            pl.BlockSpec((1, A_Q_W), fix),
            pl.BlockSpec((1, A_KV_W), fix),
            pl.BlockSpec((tm, 512), row),
            pl.BlockSpec((A_Q_W, A_Q_W), fix),
        ],
        out_specs=[o[0] for o in outs],
        out_shape=[o[1] for o in outs],
        compiler_params=_cparams(("parallel",)),
        name="proj",
    )(x, g, w_in, qg, kg, tab, bd)


SOFTMAX_ROWS = 16
N_SLOTS = 3
STEPS_PER_TRIP = 3
ST_M, ST_ALPHA, ST_NEXT_MAX, ST_L = 0, 1, 2, 8


def _flash_scratch(nq, tk, dv):
    return ([pltpu.VMEM((tk, nq), F32)] * N_SLOTS + [pltpu.VMEM((tk, nq), BF16)] * N_SLOTS
            + [pltpu.VMEM((dv, nq), F32), pltpu.VMEM((16, nq), F32)])


def _flash_cols(qst, k_ref, vt_ref, scratch):
    s_ref, p_ref = scratch[:N_SLOTS], scratch[N_SLOTS:2 * N_SLOTS]
    acc_ref, st_ref = scratch[2 * N_SLOTS:]
    nq = qst.shape[1]
    tk = s_ref[0].shape[0]
    per = tk // vt_ref.shape[2]
    n_chunks = vt_ref.shape[0] // per
    assert n_chunks >= N_SLOTS and tk % SOFTMAX_ROWS == 0
    n_sub = tk // SOFTMAX_ROWS
    row = lambda r, n=1: slice(r, r + n)
    sub = lambda slot, b: s_ref[slot][b * SOFTMAX_ROWS:(b + 1) * SOFTMAX_ROWS, :]

    def scores(j, slot):
        start = j * tk
        start = start if isinstance(start, int) else pl.multiple_of(start, tk)
        s_ref[slot][...] = jnp.dot(k_ref[pl.ds(start, tk), :], qst, preferred_element_type=F32)

    def column_max(slot):
        mx = sub(slot, 0)
        for b in range(1, n_sub):
            mx = jnp.maximum(mx, sub(slot, b))
        st_ref[row(ST_NEXT_MAX), :] = jnp.max(mx, axis=0, keepdims=True)

    def weighted(j, slot, alpha):
        vt = jnp.concatenate([vt_ref[per * j + c] for c in range(per)], axis=1)
        acc_ref[...] = alpha * acc_ref[...] + jnp.dot(vt, p_ref[slot][...], preferred_element_type=F32)

    def step(j, slot, with_scores=True, with_max=True):
        a_prev = st_ref[row(ST_ALPHA), :]
        m_old = st_ref[row(ST_M), :]
        m_new = jnp.maximum(m_old, st_ref[row(ST_NEXT_MAX), :])
        alpha = jnp.exp2(m_old - m_new)
        if with_scores:
            scores(j + 2, (slot + 2) % N_SLOTS)
        psum = jnp.zeros((8, nq), F32)
        for b in range(n_sub):
            p = jnp.exp2(sub(slot, b) - m_new)
            psum = psum + jnp.sum(p.reshape(SOFTMAX_ROWS // 8, 8, nq), axis=0)
            p_ref[slot][b * SOFTMAX_ROWS:(b + 1) * SOFTMAX_ROWS, :] = p.astype(BF16)
        weighted(jnp.maximum(j - 1, 0), (slot + 2) % N_SLOTS, a_prev)
        st_ref[row(ST_L, 8), :] = alpha * st_ref[row(ST_L, 8), :] + psum
        st_ref[row(ST_M), :] = m_new
        st_ref[row(ST_ALPHA), :] = alpha
        if with_max:
            column_max((slot + 1) % N_SLOTS)

    scores(0, 0)
    scores(1, 1)
    st_ref[...] = jnp.zeros_like(st_ref)
    st_ref[row(ST_M), :] = jnp.full((1, nq), -jnp.inf, F32)
    acc_ref[...] = jnp.zeros_like(acc_ref)
    p_ref[N_SLOTS - 1][...] = jnp.zeros((tk, nq), BF16)
    column_max(0)
    n_full = n_chunks - 2
    n_trips = n_full // STEPS_PER_TRIP
    lead = n_full - n_trips * STEPS_PER_TRIP
    for j in range(lead):
        step(j, j % N_SLOTS)

    def body(t, carry):
        for u in range(STEPS_PER_TRIP):
            step(lead + t * STEPS_PER_TRIP + u, (lead + u) % N_SLOTS)
        return carry

    lax.fori_loop(0, n_trips, body, 0)
    step(n_chunks - 2, (n_chunks - 2) % N_SLOTS, with_scores=False)
    step(n_chunks - 1, (n_chunks - 1) % N_SLOTS, with_scores=False, with_max=False)
    weighted(n_chunks - 1, (n_chunks - 1) % N_SLOTS, st_ref[row(ST_ALPHA), :])
    return acc_ref[...], jnp.sum(st_ref[row(ST_L, 8), :], axis=0, keepdims=True)


MXU_COLS = 256
FAST_STEPS_PER_TRIP = 6
P_LIMIT = 2.0 ** 60


def _flash_scratch_fast(nq, tk, dv):
    return [pltpu.VMEM((tk, nq), BF16)] * N_SLOTS + [pltpu.VMEM((dv, nq), F32), pltpu.VMEM((16, nq), F32)]


def _flash_cols_fast(qst, k_ref, vt_ref, scratch):
    p_ref = scratch[:N_SLOTS]
    acc_ref, st_ref = scratch[N_SLOTS:]
    nq = qst.shape[1]
    tk = p_ref[0].shape[0]
    per = tk // vt_ref.shape[2]
    n_chunks = vt_ref.shape[0] // per
    n_sub = tk // SOFTMAX_ROWS
    row = lambda r, n=1: slice(r, r + n)
    st_scale, st_flag = ST_ALPHA, ST_NEXT_MAX
    q_cols = [qst[:, c:c + MXU_COLS] for c in range(0, nq, MXU_COLS)]

    def keys(j):
        start = j * tk
        start = start if isinstance(start, int) else pl.multiple_of(start, tk)
        return k_ref[pl.ds(start, tk), :]

    def weighted(j, slot, scale):
        vt = jnp.concatenate([vt_ref[per * j + c] for c in range(per)], axis=1)
        acc_ref[...] = (acc_ref[...] + jnp.dot(vt, p_ref[slot][...], preferred_element_type=F32)) * scale

    def step(j, slot):
        scale_prev = st_ref[row(st_scale), :]
        m_ref = st_ref[row(ST_M), :]
        k = keys(j)
        pmaxs, psums = [], []
        for c, qc in enumerate(q_cols):
            cols = slice(c * MXU_COLS, (c + 1) * MXU_COLS)
            s = jnp.dot(k, qc, preferred_element_type=F32)
            mc = m_ref[:, cols]
            pm = None
            psum = jnp.zeros((8, MXU_COLS), F32)
            for b in range(n_sub):
                rows = slice(b * SOFTMAX_ROWS, (b + 1) * SOFTMAX_ROWS)
                p = jnp.exp2(s[rows] - mc)
                pm = p if pm is None else jnp.maximum(pm, p)
                psum = psum + jnp.sum(p.reshape(SOFTMAX_ROWS // 8, 8, MXU_COLS), axis=0)
                p_ref[slot][rows, cols] = p.astype(BF16)
            pmaxs.append(jnp.max(pm, axis=0, keepdims=True))
            psums.append(psum)
        pmax = jnp.concatenate(pmaxs, axis=1)
        weighted(jnp.maximum(j - 1, 0), (slot + 2) % N_SLOTS, scale_prev)
        big = jnp.maximum(pmax, 1.0)
        scale = 1.0 / big
        st_ref[row(ST_L, 8), :] = (st_ref[row(ST_L, 8), :] + jnp.concatenate(psums, axis=1)) * scale
        st_ref[row(ST_M), :] = m_ref + jnp.log2(big)
        st_ref[row(st_scale), :] = scale
        st_ref[row(st_flag), :] = jnp.maximum(st_ref[row(st_flag), :], pmax)

    k0 = keys(0)
    m0 = jnp.concatenate([jnp.max(jnp.dot(k0, qc, preferred_element_type=F32), axis=0, keepdims=True)
                          for qc in q_cols], axis=1)
    st_ref[...] = jnp.zeros_like(st_ref)
    st_ref[row(ST_M), :] = m0
    st_ref[row(st_scale), :] = jnp.ones((1, nq), F32)
    acc_ref[...] = jnp.zeros_like(acc_ref)
    p_ref[N_SLOTS - 1][...] = jnp.zeros((tk, nq), BF16)
    n_trips = n_chunks // FAST_STEPS_PER_TRIP
    lead = n_chunks - n_trips * FAST_STEPS_PER_TRIP
    for j in range(lead):
        step(j, j % N_SLOTS)

    def body(t, carry):
        for u in range(FAST_STEPS_PER_TRIP):
            step(lead + t * FAST_STEPS_PER_TRIP + u, (lead + u) % N_SLOTS)
        return carry

    lax.fori_loop(0, n_trips, body, 0)
    weighted(n_chunks - 1, (n_chunks - 1) % N_SLOTS, st_ref[row(st_scale), :])
    return acc_ref[...], jnp.sum(st_ref[row(ST_L, 8), :], axis=0, keepdims=True), st_ref[row(st_flag), :]


def _attend(qst, k_ref, vt_ref, rest, fast):
    if not fast:
        return _flash_cols(qst, k_ref, vt_ref, rest)
    flag_ref, scratch = rest[0], rest[1:]
    acc_t, l, flag = _flash_cols_fast(qst, k_ref, vt_ref, scratch)
    flag_ref[...] = jnp.broadcast_to(flag, flag_ref.shape)
    return acc_t, l


def _attn_outs(out_spec, out_shape, grid_axes, n_i, nq, fast):
    if not fast:
        return out_spec, out_shape
    flag_spec = pl.BlockSpec((8, nq), lambda a, i: (a * n_i + i, 0))
    return [out_spec, flag_spec], [out_shape, jax.ShapeDtypeStruct((grid_axes * n_i * 8, nq), F32)]


def _gqa_kernel(q_ref, k_ref, vt_ref, o_ref, *rest, tq, fast):
    g = pl.program_id(0)
    lane = lax.broadcasted_iota(jnp.int32, (tq, LANES), 1)
    in_g = (lane // HEAD_DIM) == g
    rows = []
    for hh in range(A_GROUP):
        qp = q_ref[:, LANES * (hh // 2):LANES * (hh // 2 + 1)].astype(F32)
        aligned = jnp.where((hh % 2) == g, qp, pltpu.roll(qp, HEAD_DIM, 1))
        rows.append(jnp.where(in_g, aligned, 0.0))
    qs = jnp.concatenate(rows, axis=0)
    acc_t, l = _attend(qs.T.astype(BF16), k_ref, vt_ref, rest, fast)
    o_t = acc_t / l
    o = jnp.concatenate([o_t, jnp.zeros_like(o_t)], axis=0).T
    left_half = lane < HEAD_DIM
    for p in range(A_GROUP // 2):
        a = o[(2 * p) * tq:(2 * p + 1) * tq]
        b = o[(2 * p + 1) * tq:(2 * p + 2) * tq]
        o_ref[:, LANES * p:LANES * (p + 1)] = jnp.where(left_half, a, pltpu.roll(b, HEAD_DIM, 1)).astype(BF16)


def _gqa_call(qa, ka, vat, tq, tk, fast):
    seq = qa.shape[0]
    nq = A_GROUP * tq
    out_specs, out_shape = _attn_outs(pl.BlockSpec((tq, A_GROUP * HEAD_DIM), lambda g, i: (i, g)),
                                      jax.ShapeDtypeStruct((seq, A_Q_W), BF16), A_KV_HEADS, seq // tq, nq, fast)
    return pl.pallas_call(
        functools.partial(_gqa_kernel, tq=tq, fast=fast),
        grid=(A_KV_HEADS, seq // tq),
        in_specs=[
            pl.BlockSpec((tq, A_GROUP * HEAD_DIM), lambda g, i: (i, g)),
            pl.BlockSpec((seq, A_KV_W), lambda g, i: (0, 0)),
            pl.BlockSpec((vat.shape[0], HEAD_DIM, vat.shape[2]), lambda g, i: (0, g, 0)),
        ],
        out_specs=out_specs,
        out_shape=out_shape,
        scratch_shapes=(_flash_scratch_fast if fast else _flash_scratch)(nq, tk, HEAD_DIM),
        compiler_params=_cparams(("parallel", "parallel")),
        name="gqa_fast" if fast else "gqa",
    )(qa, ka, vat)


def _diff_kernel(lam_ref, q_ref, k_ref, vt_ref, sg_ref, o_ref, *rest, tq, lambda_init, fast):
    lane = lax.broadcasted_iota(jnp.int32, (tq, LANES), 1)
    q = q_ref[...].astype(F32)
    qs = jnp.concatenate([jnp.where(lane < HEAD_DIM, q, 0.0),
                          jnp.where(lane >= HEAD_DIM, q, 0.0)], axis=0)
    acc_t, l = _attend(qs.T.astype(BF16), k_ref, vt_ref, rest, fast)
    o = (acc_t / l).T
    lv = lam_ref[...]
    lam = (jnp.exp(jnp.sum(lv[0:1] * lv[1:2], axis=-1, keepdims=True))
           - jnp.exp(jnp.sum(lv[2:3] * lv[3:4], axis=-1, keepdims=True)) + lambda_init)
    ob = o[:tq] - lam * o[tq:]
    ob = _rms(ob, sg_ref[...], SUBLN_EPS) * (1.0 - lambda_init)
    o_ref[...] = ob.astype(BF16)


def _diff_call(lam_vecs, qb, kb, vbt, subln_g, tq, tk, lambda_init, fast):
    seq = qb.shape[0]
    n_slabs, _, slab = vbt.shape
    out_specs, out_shape = _attn_outs(pl.BlockSpec((tq, LANES), lambda h, i: (i, h)),
                                      jax.ShapeDtypeStruct((seq, B_V_W), BF16), B_HEADS, seq // tq, 2 * tq, fast)
    return pl.pallas_call(
        functools.partial(_diff_kernel, tq=tq, lambda_init=lambda_init, fast=fast),
        grid=(B_HEADS, seq // tq),
        in_specs=[
            pl.BlockSpec((4, HEAD_DIM), lambda h, i: (0, 0)),
            pl.BlockSpec((tq, LANES), lambda h, i: (i, h)),
            pl.BlockSpec((seq, LANES), lambda h, i: (0, h)),
            pl.BlockSpec((n_slabs, B_V_DIM, slab), lambda h, i: (0, h, 0)),
            pl.BlockSpec((1, B_V_DIM), lambda h, i: (0, 0)),
        ],
        out_specs=out_specs,
        out_shape=out_shape,
        scratch_shapes=(_flash_scratch_fast if fast else _flash_scratch)(2 * tq, tk, B_V_DIM),
        compiler_params=_cparams(("parallel", "parallel")),
        name="diff_fast" if fast else "diff",
    )(lam_vecs, qb, kb, vbt, subln_g)


def _mid_kernel(oa_ref, ob_ref, x_ref, wo_ref, g_ref, wq_ref, sk_ref, h_ref, xn_ref, sc_ref):
    o = jnp.concatenate([oa_ref[...], ob_ref[...]], axis=1)
    h = x_ref[...] + jnp.dot(o, wo_ref[...], preferred_element_type=F32)
    h_ref[...] = h
    xn = _rms(h, g_ref[...], NORM_EPS).astype(BF16)
    xn_ref[...] = xn
    q = jnp.dot(xn, wq_ref[...], preferred_element_type=F32).astype(BF16)
    for hp in range(2 * PEER_HEADS):
        sc_ref[hp] = lax.dot_general(sk_ref[hp], q[:, PEER_HALF * hp:PEER_HALF * (hp + 1)],
                                     NT_DIMS, preferred_element_type=F32)


def _mid_call(oa, ob, x, w_out, g, w_query, sub_keys, tm):
    seq = x.shape[0]
    row = lambda i: (i, 0)
    fix = lambda i: (0, 0)
    nq = 2 * PEER_HEADS * PEER_HALF
    return pl.pallas_call(
        _mid_kernel,
        grid=(seq // tm,),
        in_specs=[
            pl.BlockSpec((tm, A_Q_W), row),
            pl.BlockSpec((tm, B_V_W), row),
            pl.BlockSpec((tm, D_MODEL), row),
            pl.BlockSpec((D_MODEL, D_MODEL), fix),
            pl.BlockSpec((1, D_MODEL), fix),
            pl.BlockSpec((D_MODEL, nq), fix),
            pl.BlockSpec((2 * PEER_HEADS, PEER_KEYS, PEER_HALF), lambda i: (0, 0, 0)),
        ],
        out_specs=[
            pl.BlockSpec((tm, D_MODEL), row),
            pl.BlockSpec((tm, D_MODEL), row),
            pl.BlockSpec((2 * PEER_HEADS, PEER_KEYS, tm), lambda i: (0, 0, i)),
        ],
        out_shape=[
            jax.ShapeDtypeStruct((seq, D_MODEL), F32),
            jax.ShapeDtypeStruct((seq, D_MODEL), BF16),
            jax.ShapeDtypeStruct((2 * PEER_HEADS, PEER_KEYS, seq), F32),
        ],
        compiler_params=_cparams(("parallel",)),
        name="mid",
    )(oa, ob, x, w_out, g, w_query, sub_keys)


N_TOP = PEER_TOPK + 1
_CAND = [(a, b) for a in range(N_TOP) for b in range(N_TOP) if (a + 1) * (b + 1) <= N_TOP]


def _top_vals(s, n):
    vals = []
    for r in range(n):
        m = jnp.max(s, axis=0, keepdims=True)
        vals.append(m)
        if r + 1 < n:
            s = jnp.where(s >= m, -jnp.inf, s)
    return vals


def _topk_kernel(sc_ref, e1_ref, e2_ref, tau_ref):
    t = sc_ref.shape[-1]
    taus = []
    for h in range(PEER_HEADS):
        s1 = sc_ref[2 * h]
        s2 = sc_ref[2 * h + 1]
        v1 = _top_vals(s1, N_TOP)
        v2 = _top_vals(s2, N_TOP)
        cands = [v1[a] + v2[b] for (a, b) in _CAND]
        pad = (-len(cands)) % 8
        cands += [jnp.full((1, t), -jnp.inf, F32)] * pad
        c = jnp.concatenate(cands, axis=0)
        top = _top_vals(c, N_TOP)
        t16, t17 = top[PEER_TOPK - 1], top[PEER_TOPK]
        best = v1[0] + v2[0]
        z = jnp.sum(jnp.where(c >= t16, jnp.exp(c - best), 0.0), axis=0, keepdims=True)
        rz = 1.0 / z
        e1_ref[h] = jnp.exp(s1 - v1[0]) * rz
        e2_ref[h] = jnp.exp(s2 - v2[0])
        taus.append(0.5 * (jnp.exp(t16 - best) + jnp.exp(t17 - best)) * rz)
    tau_ref[...] = jnp.concatenate(taus, axis=0)


def _topk_call(sc, tt):
    seq = sc.shape[-1]
    blk = lambda i: (0, 0, i)
    return pl.pallas_call(
        _topk_kernel,
        grid=(seq // tt,),
        in_specs=[pl.BlockSpec((2 * PEER_HEADS, PEER_KEYS, tt), blk)],
        out_specs=[
            pl.BlockSpec((PEER_HEADS, PEER_KEYS, tt), blk),
            pl.BlockSpec((PEER_HEADS, PEER_KEYS, tt), blk),
            pl.BlockSpec((PEER_HEADS, tt), lambda i: (0, i)),
        ],
        out_shape=[
            jax.ShapeDtypeStruct((PEER_HEADS, PEER_KEYS, seq), F32),
            jax.ShapeDtypeStruct((PEER_HEADS, PEER_KEYS, seq), F32),
            jax.ShapeDtypeStruct((PEER_HEADS, seq), F32),
        ],
        compiler_params=_cparams(("parallel",)),
        name="topk",
    )(sc)


def _peer_kernel(xn_ref, u_ref, vt_ref, e1_ref, e2_ref, tau_ref, h_ref, g_ref, o_ref,
                 acc_ref, a_ref, w_ref, *, eb):
    e = pl.program_id(1)
    tt = xn_ref.shape[0]
    inv_sqrt2 = 1.0 / math.sqrt(2.0)
    a_ref[...] = lax.dot_general(u_ref[...], xn_ref[...], NT_DIMS,
                                 preferred_element_type=F32)
    for ii in range(eb // PEER_KEYS):
        i = e * (eb // PEER_KEYS) + ii
        rows = slice(PEER_KEYS * ii, PEER_KEYS * (ii + 1))
        e1_rows = [e1_ref[h, pl.ds(i, 1), :] for h in range(PEER_HEADS)]
        for tc in range(tt // LANES):
            cols = slice(LANES * tc, LANES * (tc + 1))
            a = a_ref[rows, cols]
            act = 0.5 * a * (1.0 + lax.erf(a * inv_sqrt2))
            gate = None
            for h in range(PEER_HEADS):
                p = e2_ref[h, :, cols] * e1_rows[h][:, cols]
                term = jnp.where(p >= tau_ref[h:h + 1, cols], p, 0.0)
                gate = term if gate is None else gate + term
            w_ref[rows, cols] = (gate * act).astype(BF16)
    part = jnp.dot(vt_ref[...], w_ref[...], preferred_element_type=F32)

    @pl.when(e == 0)
    def _():
        acc_ref[...] = part

    @pl.when(e > 0)
    def _():
        acc_ref[...] += part

    @pl.when(e == pl.num_programs(1) - 1)
    def _():
        out = h_ref[...] + acc_ref[...].T
        o_ref[...] = _rms(out, g_ref[...], NORM_EPS)


def _peer_call(xn, u, vt, e1, e2, tau, h, g, tt, eb):
    seq = xn.shape[0]
    n_exp = u.shape[0]
    tok = lambda t, e: (t, 0)
    tok3 = lambda t, e: (0, 0, t)
    return pl.pallas_call(
        functools.partial(_peer_kernel, eb=eb),
        grid=(seq // tt, n_exp // eb),
        in_specs=[
            pl.BlockSpec((tt, D_MODEL), tok),
            pl.BlockSpec((eb, D_MODEL), lambda t, e: (e, 0)),
            pl.BlockSpec((D_MODEL, eb), lambda t, e: (0, e)),
            pl.BlockSpec((PEER_HEADS, PEER_KEYS, tt), tok3),
            pl.BlockSpec((PEER_HEADS, PEER_KEYS, tt), tok3),
            pl.BlockSpec((PEER_HEADS, tt), lambda t, e: (0, t)),
            pl.BlockSpec((tt, D_MODEL), tok),
            pl.BlockSpec((1, D_MODEL), lambda t, e: (0, 0)),
        ],
        out_specs=pl.BlockSpec((tt, D_MODEL), tok),
        out_shape=jax.ShapeDtypeStruct((seq, D_MODEL), F32),
        scratch_shapes=[pltpu.VMEM((D_MODEL, tt), F32), pltpu.VMEM((eb, tt), F32), pltpu.VMEM((eb, tt), BF16)],
        compiler_params=_cparams(("parallel", "arbitrary")),
        name="peer",
    )(xn, u, vt, e1, e2, tau, h, g)


def _tiles(seq):
    big = seq >= 4096
    return dict(
        tm=512 if big else 256,
        tq_a=128, tq_b=256,
        tk=1024 if big else 512,
        tt_topk=256, tt_peer=512 if big else 256, eb=2048,
    )


def kernel(x, norm_attn_g, w_in, q_norm_g, k_norm_g, lambda_q1, lambda_k1, lambda_q2, lambda_k2,
           subln_g, w_out, norm_ffn_g, w_query, sub_keys, expert_u, expert_v, norm_final_g):
    batch, seq, d = x.shape
    assert batch == 1 and d == D_MODEL and norm_attn_g.shape[0] == 1
    t = _tiles(seq)
    lambda_init = 0.8 - 0.6 * math.exp(-0.3 * 0)
    x2 = x.reshape(seq, d)

    tab = _rope_tables(seq)
    blk = np.arange(A_Q_W) // HEAD_DIM
    bd = jnp.asarray(np.where(blk[:, None] == blk[None, :], 1.0 / HEAD_DIM, 0.0), dtype=BF16)
    qg = (jnp.tile(q_norm_g[0], A_Q_HEADS) * Q_SCALE).reshape(1, A_Q_W)
    kg = jnp.tile(k_norm_g[0], A_KV_HEADS).reshape(1, A_KV_W)

    qa, ka, vat, qb, kb, vbt = _proj_call(x2, norm_attn_g, w_in[0].astype(BF16), qg, kg, tab, bd, t["tm"])
    lam_vecs = jnp.concatenate([lambda_q1, lambda_k1, lambda_q2, lambda_k2], axis=0)
    gqa = functools.partial(_gqa_call, qa, ka, vat, t["tq_a"], t["tk"])
    diff = functools.partial(_diff_call, lam_vecs, qb, kb, vbt, subln_g, t["tq_b"], t["tk"], lambda_init)
    oa, flag_a = gqa(True)
    ob, flag_b = diff(True)
    trusted = jnp.all(flag_a < P_LIMIT) & jnp.all(flag_b < P_LIMIT)
    oa, ob = lax.cond(trusted, lambda: (oa, ob), lambda: (gqa(False), diff(False)))

    sk = sub_keys[0].reshape(2 * PEER_HEADS, PEER_KEYS, PEER_HALF).astype(BF16)
    h, xn, sc = _mid_call(oa, ob, x2, w_out[0].astype(BF16), norm_ffn_g, w_query[0].astype(BF16), sk, t["tm"])
    e1, e2, tau = _topk_call(sc, t["tt_topk"])
    out = _peer_call(xn, expert_u[0].astype(BF16), expert_v[0].T.astype(BF16), e1, e2, tau, h,
                     norm_final_g.reshape(1, d), t["tt_peer"], t["eb"])
    return out.reshape(batch, seq, d)
```

```python
import functools
import math

import jax
import jax.numpy as jnp
import numpy as np
from jax import lax
from jax.experimental import pallas as pl
from jax.experimental.pallas import tpu as pltpu

F32 = jnp.float32
BF16 = jnp.bfloat16

D_MODEL = 1024
HEAD_DIM = 64
A_Q_HEADS = 8
A_KV_HEADS = 2
A_GROUP = A_Q_HEADS // A_KV_HEADS
B_HEADS = 4
B_V_DIM = 2 * HEAD_DIM
GRID_W = 64
AXIAL_THETA = 10000.0
AXIAL_HALF = HEAD_DIM // 2
ROPE_THETA = 500000.0
ROPE_DIMS = HEAD_DIM // 4
NORM_EPS = 1e-6
SUBLN_EPS = 1e-5
A_Q_W = A_Q_HEADS * HEAD_DIM
A_KV_W = A_KV_HEADS * HEAD_DIM
B_QK_W = 2 * B_HEADS * HEAD_DIM
B_V_W = B_HEADS * B_V_DIM
IN_COLS = A_Q_W + 2 * A_KV_W + 2 * B_QK_W + B_V_W
PEER_HEADS = 8
PEER_KEYS = 128
PEER_HALF = 128
PEER_TOPK = 16
LANES = 128

NT_DIMS = (((1,), (1,)), ((), ()))
Q_SCALE = (HEAD_DIM ** -0.5) * math.log2(math.e)

VMEM_LIMIT = 56 * 1024 * 1024


def _cparams(sem):
    return pltpu.CompilerParams(dimension_semantics=sem, vmem_limit_bytes=VMEM_LIMIT)


def _rms(x, g, eps):
    return x * lax.rsqrt(jnp.mean(x * x, axis=-1, keepdims=True) + eps) * g


def _group_mean_sq(v, bd):
    v2 = v * v
    hi = v2.astype(BF16)
    lo = (v2 - hi.astype(F32)).astype(BF16)
    return (jnp.dot(hi, bd, preferred_element_type=F32)
            + jnp.dot(lo, bd, preferred_element_type=F32))


def _rot_half(v, half, group):
    width = v.shape[-1]
    lane = lax.broadcasted_iota(jnp.int32, v.shape, 1)
    fwd = pltpu.roll(v, width - half, 1)
    bwd = pltpu.roll(v, half, 1)
    return jnp.where((lane % group) < half, fwd, bwd)


def _tile4(t):
    return jnp.concatenate([t, t, t, t], axis=1)


def _proj_kernel(x_ref, g_ref, w_ref, qg_ref, kg_ref, tab_ref, bd_ref,
                 qa_ref, ka_ref, va_ref, qb_ref, kb_ref, vb_ref):
    xn = _rms(x_ref[...], g_ref[...], NORM_EPS)
    proj = jnp.dot(xn.astype(BF16), w_ref[...], preferred_element_type=F32)
    tab = tab_ref[...]
    cos_a, sin_a = tab[:, 0:128], tab[:, 128:256]
    cos_b, sin_b = tab[:, 256:384], tab[:, 384:512]
    bd = bd_ref[...]

    c0 = 0
    qa = proj[:, c0:c0 + A_Q_W]
    qa = qa * lax.rsqrt(_group_mean_sq(qa, bd) + NORM_EPS) * qg_ref[...]
    qa = qa * _tile4(cos_a) + _rot_half(qa, AXIAL_HALF // 2, AXIAL_HALF) * _tile4(sin_a)
    qa_ref[...] = qa.astype(BF16)
    c0 += A_Q_W

    ka = proj[:, c0:c0 + A_KV_W]
    ka = ka * lax.rsqrt(_group_mean_sq(ka, bd[:A_KV_W, :A_KV_W]) + NORM_EPS) * kg_ref[...]
    ka = ka * cos_a + _rot_half(ka, AXIAL_HALF // 2, AXIAL_HALF) * sin_a
    ka_ref[...] = ka.astype(BF16)
    c0 += A_KV_W

    va_ref[0] = proj[:, c0:c0 + A_KV_W].T.astype(BF16)
    c0 += A_KV_W

    qb = proj[:, c0:c0 + B_QK_W]
    qb = qb * _tile4(cos_b) + _rot_half(qb, ROPE_DIMS // 2, HEAD_DIM) * _tile4(sin_b)
    qb_ref[...] = (qb * Q_SCALE).astype(BF16)
    c0 += B_QK_W

    kb = proj[:, c0:c0 + B_QK_W]
    kb = kb * _tile4(cos_b) + _rot_half(kb, ROPE_DIMS // 2, HEAD_DIM) * _tile4(sin_b)
    kb_ref[...] = kb.astype(BF16)
    c0 += B_QK_W

    vb_ref[0] = proj[:, c0:c0 + B_V_W].T.astype(BF16)


def _rope_tables(seq):
    f32 = np.float32
    rows = seq // GRID_W
    row = np.repeat(np.arange(rows, dtype=f32), GRID_W)
    col = np.tile(np.arange(GRID_W, dtype=f32), rows)
    pos = np.arange(seq, dtype=f32)
    inv_ax = (f32(AXIAL_THETA) ** (-np.arange(0, AXIAL_HALF, 2, dtype=f32) / f32(AXIAL_HALF))).astype(f32)
    inv_p = (f32(ROPE_THETA) ** (-np.arange(0, ROPE_DIMS, 2, dtype=f32) / f32(ROPE_DIMS))).astype(f32)
    row_ang = row[:, None] * inv_ax[None, :]
    col_ang = col[:, None] * inv_ax[None, :]
    pos_ang = pos[:, None] * inv_p[None, :]
    cr, sr = np.cos(row_ang), np.sin(row_ang)
    cc, sc = np.cos(col_ang), np.sin(col_ang)
    cp, sp = np.cos(pos_ang), np.sin(pos_ang)
    rest = HEAD_DIM - ROPE_DIMS
    cos_a = np.concatenate([cr, cr, cc, cc], axis=1)
    sin_a = np.concatenate([-sr, sr, -sc, sc], axis=1)
    cos_b = np.concatenate([cp, cp, np.ones((seq, rest), f32)], axis=1)
    sin_b = np.concatenate([-sp, sp, np.zeros((seq, rest), f32)], axis=1)
    two = lambda t: np.concatenate([t, t], axis=1)
    tab = np.concatenate([two(cos_a), two(sin_a), two(cos_b), two(sin_b)], axis=1).astype(f32)
    return jnp.asarray(tab)


def _proj_call(x, g, w_in, qg, kg, tab, bd, tm):
    seq = x.shape[0]
    row = lambda i: (i, 0)
    fix = lambda i: (0, 0)
    n = seq // tm
    rows_out = lambda w: (pl.BlockSpec((tm, w), row), jax.ShapeDtypeStruct((seq, w), BF16))
    cols_out = lambda w: (pl.BlockSpec((1, w, tm), lambda i: (i, 0, 0)), jax.ShapeDtypeStruct((n, w, tm), BF16))
    outs = [rows_out(A_Q_W), rows_out(A_KV_W), cols_out(A_KV_W), rows_out(B_QK_W), rows_out(B_QK_W), cols_out(B_V_W)]
    return pl.pallas_call(
        _proj_kernel,
        grid=(seq // tm,),
        in_specs=[
            pl.BlockSpec((tm, D_MODEL), row),
            pl.BlockSpec((1, D_MODEL), fix),
            pl.BlockSpec((D_MODEL, IN_COLS), fix),
            pl.BlockSpec((1, A_Q_W), fix),
            pl.BlockSpec((1, A_KV_W), fix),
            pl.BlockSpec((tm, 512), row),
            pl.BlockSpec((A_Q_W, A_Q_W), fix),
        ],
        out_specs=[o[0] for o in outs],
        out_shape=[o[1] for o in outs],
        compiler_params=_cparams(("parallel",)),
        name="proj",
    )(x, g, w_in, qg, kg, tab, bd)


SOFTMAX_ROWS = 16
N_SLOTS = 3
STEPS_PER_TRIP = 3
ST_M, ST_ALPHA, ST_NEXT_MAX, ST_L = 0, 1, 2, 8


def _flash_scratch(nq, tk, dv):
    return ([pltpu.VMEM((tk, nq), F32)] * N_SLOTS + [pltpu.VMEM((tk, nq), BF16)] * N_SLOTS
            + [pltpu.VMEM((dv, nq), F32), pltpu.VMEM((16, nq), F32)])


def _flash_cols(qst, k_ref, vt_ref, scratch):
    s_ref, p_ref = scratch[:N_SLOTS], scratch[N_SLOTS:2 * N_SLOTS]
    acc_ref, st_ref = scratch[2 * N_SLOTS:]
    nq = qst.shape[1]
    tk = s_ref[0].shape[0]
    per = tk // vt_ref.shape[2]
    n_chunks = vt_ref.shape[0] // per
    assert n_chunks >= N_SLOTS and tk % SOFTMAX_ROWS == 0
    n_sub = tk // SOFTMAX_ROWS
    row = lambda r, n=1: slice(r, r + n)
    sub = lambda slot, b: s_ref[slot][b * SOFTMAX_ROWS:(b + 1) * SOFTMAX_ROWS, :]

    def scores(j, slot):
        start = j * tk
        start = start if isinstance(start, int) else pl.multiple_of(start, tk)
        s_ref[slot][...] = jnp.dot(k_ref[pl.ds(start, tk), :], qst, preferred_element_type=F32)

    def column_max(slot):
        mx = sub(slot, 0)
        for b in range(1, n_sub):
            mx = jnp.maximum(mx, sub(slot, b))
        st_ref[row(ST_NEXT_MAX), :] = jnp.max(mx, axis=0, keepdims=True)

    def weighted(j, slot, alpha):
        vt = jnp.concatenate([vt_ref[per * j + c] for c in range(per)], axis=1)
        acc_ref[...] = alpha * acc_ref[...] + jnp.dot(vt, p_ref[slot][...], preferred_element_type=F32)

    def step(j, slot, with_scores=True, with_max=True):
        a_prev = st_ref[row(ST_ALPHA), :]
        m_old = st_ref[row(ST_M), :]
        m_new = jnp.maximum(m_old, st_ref[row(ST_NEXT_MAX), :])
        alpha = jnp.exp2(m_old - m_new)
        if with_scores:
            scores(j + 2, (slot + 2) % N_SLOTS)
        psum = jnp.zeros((8, nq), F32)
        for b in range(n_sub):
            p = jnp.exp2(sub(slot, b) - m_new)
            psum = psum + jnp.sum(p.reshape(SOFTMAX_ROWS // 8, 8, nq), axis=0)
            p_ref[slot][b * SOFTMAX_ROWS:(b + 1) * SOFTMAX_ROWS, :] = p.astype(BF16)
        weighted(jnp.maximum(j - 1, 0), (slot + 2) % N_SLOTS, a_prev)
        st_ref[row(ST_L, 8), :] = alpha * st_ref[row(ST_L, 8), :] + psum
        st_ref[row(ST_M), :] = m_new
        st_ref[row(ST_ALPHA), :] = alpha
        if with_max:
            column_max((slot + 1) % N_SLOTS)

    scores(0, 0)
    scores(1, 1)
    st_ref[...] = jnp.zeros_like(st_ref)
    st_ref[row(ST_M), :] = jnp.full((1, nq), -jnp.inf, F32)
    acc_ref[...] = jnp.zeros_like(acc_ref)
    p_ref[N_SLOTS - 1][...] = jnp.zeros((tk, nq), BF16)
    column_max(0)
    n_full = n_chunks - 2
    n_trips = n_full // STEPS_PER_TRIP
    lead = n_full - n_trips * STEPS_PER_TRIP
    for j in range(lead):
        step(j, j % N_SLOTS)

    def body(t, carry):
        for u in range(STEPS_PER_TRIP):
            step(lead + t * STEPS_PER_TRIP + u, (lead + u) % N_SLOTS)
        return carry

    lax.fori_loop(0, n_trips, body, 0)
    step(n_chunks - 2, (n_chunks - 2) % N_SLOTS, with_scores=False)
    step(n_chunks - 1, (n_chunks - 1) % N_SLOTS, with_scores=False, with_max=False)
    weighted(n_chunks - 1, (n_chunks - 1) % N_SLOTS, st_ref[row(ST_ALPHA), :])
    return acc_ref[...], jnp.sum(st_ref[row(ST_L, 8), :], axis=0, keepdims=True)


MXU_COLS = 256
FAST_STEPS_PER_TRIP = 6
P_LIMIT = 2.0 ** 60


def _flash_scratch_fast(nq, tk, dv):
    return [pltpu.VMEM((tk, nq), BF16)] * N_SLOTS + [pltpu.VMEM((dv, nq), F32), pltpu.VMEM((16, nq), F32)]


def _flash_cols_fast(qst, k_ref, vt_ref, scratch):
    p_ref = scratch[:N_SLOTS]
    acc_ref, st_ref = scratch[N_SLOTS:]
    nq = qst.shape[1]
    tk = p_ref[0].shape[0]
    per = tk // vt_ref.shape[2]
    n_chunks = vt_ref.shape[0] // per
    n_sub = tk // SOFTMAX_ROWS
    row = lambda r, n=1: slice(r, r + n)
    st_scale, st_flag = ST_ALPHA, ST_NEXT_MAX
    q_cols = [qst[:, c:c + MXU_COLS] for c in range(0, nq, MXU_COLS)]

    def keys(j):
        start = j * tk
        start = start if isinstance(start, int) else pl.multiple_of(start, tk)
        return k_ref[pl.ds(start, tk), :]

    def weighted(j, slot, scale):
        vt = jnp.concatenate([vt_ref[per * j + c] for c in range(per)], axis=1)
        acc_ref[...] = (acc_ref[...] + jnp.dot(vt, p_ref[slot][...], preferred_element_type=F32)) * scale

    def step(j, slot):
        scale_prev = st_ref[row(st_scale), :]
        m_ref = st_ref[row(ST_M), :]
        k = keys(j)
        pmaxs, psums = [], []
        for c, qc in enumerate(q_cols):
            cols = slice(c * MXU_COLS, (c + 1) * MXU_COLS)
            s = jnp.dot(k, qc, preferred_element_type=F32)
            mc = m_ref[:, cols]
            pm = None
            psum = jnp.zeros((8, MXU_COLS), F32)
            for b in range(n_sub):
                rows = slice(b * SOFTMAX_ROWS, (b + 1) * SOFTMAX_ROWS)
                p = jnp.exp2(s[rows] - mc)
                pm = p if pm is None else jnp.maximum(pm, p)
                psum = psum + jnp.sum(p.reshape(SOFTMAX_ROWS // 8, 8, MXU_COLS), axis=0)
                p_ref[slot][rows, cols] = p.astype(BF16)
            pmaxs.append(jnp.max(pm, axis=0, keepdims=True))
            psums.append(psum)
            if c == 0:
                weighted(jnp.maximum(j - 1, 0), (slot + 2) % N_SLOTS, scale_prev)
        pmax = jnp.concatenate(pmaxs, axis=1)
        big = jnp.maximum(pmax, 1.0)
        scale = 1.0 / big
        st_ref[row(ST_L, 8), :] = (st_ref[row(ST_L, 8), :] + jnp.concatenate(psums, axis=1)) * scale
        st_ref[row(ST_M), :] = m_ref + jnp.log2(big)
        st_ref[row(st_scale), :] = scale
        st_ref[row(st_flag), :] = jnp.maximum(st_ref[row(st_flag), :], pmax)

    k0 = keys(0)
    m0 = jnp.concatenate([jnp.max(jnp.dot(k0, qc, preferred_element_type=F32), axis=0, keepdims=True)
                          for qc in q_cols], axis=1)
    st_ref[...] = jnp.zeros_like(st_ref)
    st_ref[row(ST_M), :] = m0
    st_ref[row(st_scale), :] = jnp.ones((1, nq), F32)
    acc_ref[...] = jnp.zeros_like(acc_ref)
    p_ref[N_SLOTS - 1][...] = jnp.zeros((tk, nq), BF16)
    n_trips = n_chunks // FAST_STEPS_PER_TRIP
    lead = n_chunks - n_trips * FAST_STEPS_PER_TRIP
    for j in range(lead):
        step(j, j % N_SLOTS)

    def body(t, carry):
        for u in range(FAST_STEPS_PER_TRIP):
            step(lead + t * FAST_STEPS_PER_TRIP + u, (lead + u) % N_SLOTS)
        return carry

    lax.fori_loop(0, n_trips, body, 0)
    weighted(n_chunks - 1, (n_chunks - 1) % N_SLOTS, st_ref[row(st_scale), :])
    return acc_ref[...], jnp.sum(st_ref[row(ST_L, 8), :], axis=0, keepdims=True), st_ref[row(st_flag), :]


def _attend(qst, k_ref, vt_ref, rest, fast):
    if not fast:
        return _flash_cols(qst, k_ref, vt_ref, rest)
    flag_ref, scratch = rest[0], rest[1:]
    acc_t, l, flag = _flash_cols_fast(qst, k_ref, vt_ref, scratch)
    flag_ref[...] = jnp.broadcast_to(flag, flag_ref.shape)
    return acc_t, l


def _attn_outs(out_spec, out_shape, grid_axes, n_i, nq, fast):
    if not fast:
        return out_spec, out_shape
    flag_spec = pl.BlockSpec((8, nq), lambda a, i: (a * n_i + i, 0))
    return [out_spec, flag_spec], [out_shape, jax.ShapeDtypeStruct((grid_axes * n_i * 8, nq), F32)]


def _gqa_kernel(q_ref, k_ref, vt_ref, o_ref, *rest, tq, fast):
    g = pl.program_id(0)
    lane = lax.broadcasted_iota(jnp.int32, (tq, LANES), 1)
    in_g = (lane // HEAD_DIM) == g
    rows = []
    for hh in range(A_GROUP):
        qp = q_ref[:, LANES * (hh // 2):LANES * (hh // 2 + 1)].astype(F32)
        aligned = jnp.where((hh % 2) == g, qp, pltpu.roll(qp, HEAD_DIM, 1))
        rows.append(jnp.where(in_g, aligned, 0.0))
    qs = jnp.concatenate(rows, axis=0)
    acc_t, l = _attend(qs.T.astype(BF16), k_ref, vt_ref, rest, fast)
    o_t = acc_t / l
    o = jnp.concatenate([o_t, jnp.zeros_like(o_t)], axis=0).T
    left_half = lane < HEAD_DIM
    for p in range(A_GROUP // 2):
        a = o[(2 * p) * tq:(2 * p + 1) * tq]
        b = o[(2 * p + 1) * tq:(2 * p + 2) * tq]
        o_ref[:, LANES * p:LANES * (p + 1)] = jnp.where(left_half, a, pltpu.roll(b, HEAD_DIM, 1)).astype(BF16)


def _gqa_call(qa, ka, vat, tq, tk, fast):
    seq = qa.shape[0]
    nq = A_GROUP * tq
    out_specs, out_shape = _attn_outs(pl.BlockSpec((tq, A_GROUP * HEAD_DIM), lambda g, i: (i, g)),
                                      jax.ShapeDtypeStruct((seq, A_Q_W), BF16), A_KV_HEADS, seq // tq, nq, fast)
    return pl.pallas_call(
        functools.partial(_gqa_kernel, tq=tq, fast=fast),
        grid=(A_KV_HEADS, seq // tq),
        in_specs=[
            pl.BlockSpec((tq, A_GROUP * HEAD_DIM), lambda g, i: (i, g)),
            pl.BlockSpec((seq, A_KV_W), lambda g, i: (0, 0)),
            pl.BlockSpec((vat.shape[0], HEAD_DIM, vat.shape[2]), lambda g, i: (0, g, 0)),
        ],
        out_specs=out_specs,
        out_shape=out_shape,
        scratch_shapes=(_flash_scratch_fast if fast else _flash_scratch)(nq, tk, HEAD_DIM),
        compiler_params=_cparams(("parallel", "parallel")),
        name="gqa_fast" if fast else "gqa",
    )(qa, ka, vat)


def _diff_kernel(lam_ref, q_ref, k_ref, vt_ref, sg_ref, o_ref, *rest, tq, lambda_init, fast):
    lane = lax.broadcasted_iota(jnp.int32, (tq, LANES), 1)
    q = q_ref[...].astype(F32)
    qs = jnp.concatenate([jnp.where(lane < HEAD_DIM, q, 0.0),
                          jnp.where(lane >= HEAD_DIM, q, 0.0)], axis=0)
    acc_t, l = _attend(qs.T.astype(BF16), k_ref, vt_ref, rest, fast)
    o = (acc_t / l).T
    lv = lam_ref[...]
    lam = (jnp.exp(jnp.sum(lv[0:1] * lv[1:2], axis=-1, keepdims=True))
           - jnp.exp(jnp.sum(lv[2:3] * lv[3:4], axis=-1, keepdims=True)) + lambda_init)
    ob = o[:tq] - lam * o[tq:]
    ob = _rms(ob, sg_ref[...], SUBLN_EPS) * (1.0 - lambda_init)
    o_ref[...] = ob.astype(BF16)


def _diff_call(lam_vecs, qb, kb, vbt, subln_g, tq, tk, lambda_init, fast):
    seq = qb.shape[0]
    n_slabs, _, slab = vbt.shape
    out_specs, out_shape = _attn_outs(pl.BlockSpec((tq, LANES), lambda h, i: (i, h)),
                                      jax.ShapeDtypeStruct((seq, B_V_W), BF16), B_HEADS, seq // tq, 2 * tq, fast)
    return pl.pallas_call(
        functools.partial(_diff_kernel, tq=tq, lambda_init=lambda_init, fast=fast),
        grid=(B_HEADS, seq // tq),
        in_specs=[
            pl.BlockSpec((4, HEAD_DIM), lambda h, i: (0, 0)),
            pl.BlockSpec((tq, LANES), lambda h, i: (i, h)),
            pl.BlockSpec((seq, LANES), lambda h, i: (0, h)),
            pl.BlockSpec((n_slabs, B_V_DIM, slab), lambda h, i: (0, h, 0)),
            pl.BlockSpec((1, B_V_DIM), lambda h, i: (0, 0)),
        ],
        out_specs=out_specs,
        out_shape=out_shape,
        scratch_shapes=(_flash_scratch_fast if fast else _flash_scratch)(2 * tq, tk, B_V_DIM),
        compiler_params=_cparams(("parallel", "parallel")),
        name="diff_fast" if fast else "diff",
    )(lam_vecs, qb, kb, vbt, subln_g)


def _mid_kernel(oa_ref, ob_ref, x_ref, wo_ref, g_ref, wq_ref, sk_ref, h_ref, xn_ref, sc_ref):
    o = jnp.concatenate([oa_ref[...], ob_ref[...]], axis=1)
    h = x_ref[...] + jnp.dot(o, wo_ref[...], preferred_element_type=F32)
    h_ref[...] = h
    xn = _rms(h, g_ref[...], NORM_EPS).astype(BF16)
    xn_ref[...] = xn
    q = jnp.dot(xn, wq_ref[...], preferred_element_type=F32).astype(BF16)
    for hp in range(2 * PEER_HEADS):
        sc_ref[hp] = lax.dot_general(sk_ref[hp], q[:, PEER_HALF * hp:PEER_HALF * (hp + 1)],
                                     NT_DIMS, preferred_element_type=F32)


def _mid_call(oa, ob, x, w_out, g, w_query, sub_keys, tm):
    seq = x.shape[0]
    row = lambda i: (i, 0)
    fix = lambda i: (0, 0)
    nq = 2 * PEER_HEADS * PEER_HALF
    return pl.pallas_call(
        _mid_kernel,
        grid=(seq // tm,),
        in_specs=[
            pl.BlockSpec((tm, A_Q_W), row),
            pl.BlockSpec((tm, B_V_W), row),
            pl.BlockSpec((tm, D_MODEL), row),
            pl.BlockSpec((D_MODEL, D_MODEL), fix),
            pl.BlockSpec((1, D_MODEL), fix),
            pl.BlockSpec((D_MODEL, nq), fix),
            pl.BlockSpec((2 * PEER_HEADS, PEER_KEYS, PEER_HALF), lambda i: (0, 0, 0)),
        ],
        out_specs=[
            pl.BlockSpec((tm, D_MODEL), row),
            pl.BlockSpec((tm, D_MODEL), row),
            pl.BlockSpec((2 * PEER_HEADS, PEER_KEYS, tm), lambda i: (0, 0, i)),
        ],
        out_shape=[
            jax.ShapeDtypeStruct((seq, D_MODEL), F32),
            jax.ShapeDtypeStruct((seq, D_MODEL), BF16),
            jax.ShapeDtypeStruct((2 * PEER_HEADS, PEER_KEYS, seq), F32),
        ],
        compiler_params=_cparams(("parallel",)),
        name="mid",
    )(oa, ob, x, w_out, g, w_query, sub_keys)


N_TOP = PEER_TOPK + 1
_CAND = [(a, b) for a in range(N_TOP) for b in range(N_TOP) if (a + 1) * (b + 1) <= N_TOP]


def _top_vals(s, n):
    vals = []
    for r in range(n):
        m = jnp.max(s, axis=0, keepdims=True)
        vals.append(m)
        if r + 1 < n:
            s = jnp.where(s >= m, -jnp.inf, s)
    return vals


def _topk_kernel(sc_ref, e1_ref, e2_ref, tau_ref):
    t = sc_ref.shape[-1]
    taus = []
    for h in range(PEER_HEADS):
        s1 = sc_ref[2 * h]
        s2 = sc_ref[2 * h + 1]
        v1 = _top_vals(s1, N_TOP)
        v2 = _top_vals(s2, N_TOP)
        cands = [v1[a] + v2[b] for (a, b) in _CAND]
        pad = (-len(cands)) % 8
        cands += [jnp.full((1, t), -jnp.inf, F32)] * pad
        c = jnp.concatenate(cands, axis=0)
        top = _top_vals(c, N_TOP)
        t16, t17 = top[PEER_TOPK - 1], top[PEER_TOPK]
        best = v1[0] + v2[0]
        z = jnp.sum(jnp.where(c >= t16, jnp.exp(c - best), 0.0), axis=0, keepdims=True)
        rz = 1.0 / z
        e1_ref[h] = jnp.exp(s1 - v1[0]) * rz
        e2_ref[h] = jnp.exp(s2 - v2[0])
        taus.append(0.5 * (jnp.exp(t16 - best) + jnp.exp(t17 - best)) * rz)
    tau_ref[...] = jnp.concatenate(taus, axis=0)


def _topk_call(sc, tt):
    seq = sc.shape[-1]
    blk = lambda i: (0, 0, i)
    return pl.pallas_call(
        _topk_kernel,
        grid=(seq // tt,),
        in_specs=[pl.BlockSpec((2 * PEER_HEADS, PEER_KEYS, tt), blk)],
        out_specs=[
            pl.BlockSpec((PEER_HEADS, PEER_KEYS, tt), blk),
            pl.BlockSpec((PEER_HEADS, PEER_KEYS, tt), blk),
            pl.BlockSpec((PEER_HEADS, tt), lambda i: (0, i)),
        ],
        out_shape=[
            jax.ShapeDtypeStruct((PEER_HEADS, PEER_KEYS, seq), F32),
            jax.ShapeDtypeStruct((PEER_HEADS, PEER_KEYS, seq), F32),
            jax.ShapeDtypeStruct((PEER_HEADS, seq), F32),
        ],
        compiler_params=_cparams(("parallel",)),
        name="topk",
    )(sc)


def _peer_kernel(xn_ref, u_ref, vt_ref, e1_ref, e2_ref, tau_ref, h_ref, g_ref, o_ref,
                 acc_ref, a_ref, w_ref, *, eb):
    e = pl.program_id(1)
    tt = xn_ref.shape[0]
    inv_sqrt2 = 1.0 / math.sqrt(2.0)
    a_ref[...] = lax.dot_general(u_ref[...], xn_ref[...], NT_DIMS,
                                 preferred_element_type=F32)
    for ii in range(eb // PEER_KEYS):
        i = e * (eb // PEER_KEYS) + ii
        rows = slice(PEER_KEYS * ii, PEER_KEYS * (ii + 1))
        e1_rows = [e1_ref[h, pl.ds(i, 1), :] for h in range(PEER_HEADS)]
        for tc in range(tt // LANES):
            cols = slice(LANES * tc, LANES * (tc + 1))
            a = a_ref[rows, cols]
            act = 0.5 * a * (1.0 + lax.erf(a * inv_sqrt2))
            gate = None
            for h in range(PEER_HEADS):
                p = e2_ref[h, :, cols] * e1_rows[h][:, cols]
                term = jnp.where(p >= tau_ref[h:h + 1, cols], p, 0.0)
                gate = term if gate is None else gate + term
            w_ref[rows, cols] = (gate * act).astype(BF16)
    part = jnp.dot(vt_ref[...], w_ref[...], preferred_element_type=F32)

    @pl.when(e == 0)
    def _():
        acc_ref[...] = part

    @pl.when(e > 0)
    def _():
        acc_ref[...] += part

    @pl.when(e == pl.num_programs(1) - 1)
    def _():
        out = h_ref[...] + acc_ref[...].T
        o_ref[...] = _rms(out, g_ref[...], NORM_EPS)


def _peer_call(xn, u, vt, e1, e2, tau, h, g, tt, eb):
    seq = xn.shape[0]
    n_exp = u.shape[0]
    tok = lambda t, e: (t, 0)
    tok3 = lambda t, e: (0, 0, t)
    return pl.pallas_call(
        functools.partial(_peer_kernel, eb=eb),
        grid=(seq // tt, n_exp // eb),
        in_specs=[
            pl.BlockSpec((tt, D_MODEL), tok),
            pl.BlockSpec((eb, D_MODEL), lambda t, e: (e, 0)),
            pl.BlockSpec((D_MODEL, eb), lambda t, e: (0, e)),
            pl.BlockSpec((PEER_HEADS, PEER_KEYS, tt), tok3),
            pl.BlockSpec((PEER_HEADS, PEER_KEYS, tt), tok3),
            pl.BlockSpec((PEER_HEADS, tt), lambda t, e: (0, t)),
            pl.BlockSpec((tt, D_MODEL), tok),
            pl.BlockSpec((1, D_MODEL), lambda t, e: (0, 0)),
        ],
        out_specs=pl.BlockSpec((tt, D_MODEL), tok),
        out_shape=jax.ShapeDtypeStruct((seq, D_MODEL), F32),
        scratch_shapes=[pltpu.VMEM((D_MODEL, tt), F32), pltpu.VMEM((eb, tt), F32), pltpu.VMEM((eb, tt), BF16)],
        compiler_params=_cparams(("parallel", "arbitrary")),
        name="peer",
    )(xn, u, vt, e1, e2, tau, h, g)


def _tiles(seq):
    big = seq >= 4096
    return dict(
        tm=512 if big else 256,
        tq_a=128, tq_b=256,
        tk=1024 if big else 512,
        tt_topk=256, tt_peer=512 if big else 256, eb=2048,
    )


def kernel(x, norm_attn_g, w_in, q_norm_g, k_norm_g, lambda_q1, lambda_k1, lambda_q2, lambda_k2,
           subln_g, w_out, norm_ffn_g, w_query, sub_keys, expert_u, expert_v, norm_final_g):
    batch, seq, d = x.shape
    assert batch == 1 and d == D_MODEL and norm_attn_g.shape[0] == 1
    t = _tiles(seq)
    lambda_init = 0.8 - 0.6 * math.exp(-0.3 * 0)
    x2 = x.reshape(seq, d)

    tab = _rope_tables(seq)
    blk = np.arange(A_Q_W) // HEAD_DIM
    bd = jnp.asarray(np.where(blk[:, None] == blk[None, :], 1.0 / HEAD_DIM, 0.0), dtype=BF16)
    qg = (jnp.tile(q_norm_g[0], A_Q_HEADS) * Q_SCALE).reshape(1, A_Q_W)
    kg = jnp.tile(k_norm_g[0], A_KV_HEADS).reshape(1, A_KV_W)

    qa, ka, vat, qb, kb, vbt = _proj_call(x2, norm_attn_g, w_in[0].astype(BF16), qg, kg, tab, bd, t["tm"])
    lam_vecs = jnp.concatenate([lambda_q1, lambda_k1, lambda_q2, lambda_k2], axis=0)
    gqa = functools.partial(_gqa_call, qa, ka, vat, t["tq_a"], t["tk"])
    diff = functools.partial(_diff_call, lam_vecs, qb, kb, vbt, subln_g, t["tq_b"], t["tk"], lambda_init)
    oa, flag_a = gqa(True)
    ob, flag_b = diff(True)
    trusted = jnp.all(flag_a < P_LIMIT) & jnp.all(flag_b < P_LIMIT)
    oa, ob = lax.cond(trusted, lambda: (oa, ob), lambda: (gqa(False), diff(False)))

    sk = sub_keys[0].reshape(2 * PEER_HEADS, PEER_KEYS, PEER_HALF).astype(BF16)
    h, xn, sc = _mid_call(oa, ob, x2, w_out[0].astype(BF16), norm_ffn_g, w_query[0].astype(BF16), sk, t["tm"])
    e1, e2, tau = _topk_call(sc, t["tt_topk"])
    out = _peer_call(xn, expert_u[0].astype(BF16), expert_v[0].T.astype(BF16), e1, e2, tau, h,
                     norm_final_g.reshape(1, d), t["tt_peer"], t["eb"])
    return out.reshape(batch, seq, d)
```

```python
import functools
import math

import jax
import jax.numpy as jnp
import numpy as np
from jax import lax
from jax.experimental import pallas as pl
from jax.experimental.pallas import tpu as pltpu

F32 = jnp.float32
BF16 = jnp.bfloat16

D_MODEL = 1024
HEAD_DIM = 64
A_Q_HEADS = 8
A_KV_HEADS = 2
A_GROUP = A_Q_HEADS // A_KV_HEADS
B_HEADS = 4
B_V_DIM = 2 * HEAD_DIM
GRID_W = 64
AXIAL_THETA = 10000.0
AXIAL_HALF = HEAD_DIM // 2
ROPE_THETA = 500000.0
ROPE_DIMS = HEAD_DIM // 4
NORM_EPS = 1e-6
SUBLN_EPS = 1e-5
A_Q_W = A_Q_HEADS * HEAD_DIM
A_KV_W = A_KV_HEADS * HEAD_DIM
B_QK_W = 2 * B_HEADS * HEAD_DIM
B_V_W = B_HEADS * B_V_DIM
IN_COLS = A_Q_W + 2 * A_KV_W + 2 * B_QK_W + B_V_W
PEER_HEADS = 8
PEER_KEYS = 128
PEER_HALF = 128
PEER_TOPK = 16
LANES = 128

NT_DIMS = (((1,), (1,)), ((), ()))
Q_SCALE = (HEAD_DIM ** -0.5) * math.log2(math.e)

VMEM_LIMIT = 56 * 1024 * 1024


def _cparams(sem):
    return pltpu.CompilerParams(dimension_semantics=sem, vmem_limit_bytes=VMEM_LIMIT)


def _rms(x, g, eps):
    return x * lax.rsqrt(jnp.mean(x * x, axis=-1, keepdims=True) + eps) * g


def _group_mean_sq(v, bd):
    v2 = v * v
    hi = v2.astype(BF16)
    lo = (v2 - hi.astype(F32)).astype(BF16)
    return (jnp.dot(hi, bd, preferred_element_type=F32)
            + jnp.dot(lo, bd, preferred_element_type=F32))


def _rot_half(v, half, group):
    width = v.shape[-1]
    lane = lax.broadcasted_iota(jnp.int32, v.shape, 1)
    fwd = pltpu.roll(v, width - half, 1)
    bwd = pltpu.roll(v, half, 1)
    return jnp.where((lane % group) < half, fwd, bwd)


def _tile4(t):
    return jnp.concatenate([t, t, t, t], axis=1)


def _proj_kernel(x_ref, g_ref, w_ref, qg_ref, kg_ref, tab_ref, bd_ref,
                 qa_ref, ka_ref, va_ref, qb_ref, kb_ref, vb_ref):
    xn = _rms(x_ref[...], g_ref[...], NORM_EPS)
    proj = jnp.dot(xn.astype(BF16), w_ref[...], preferred_element_type=F32)
    tab = tab_ref[...]
    cos_a, sin_a = tab[:, 0:128], tab[:, 128:256]
    cos_b, sin_b = tab[:, 256:384], tab[:, 384:512]
    bd = bd_ref[...]

    c0 = 0
    qa = proj[:, c0:c0 + A_Q_W]
    qa = qa * lax.rsqrt(_group_mean_sq(qa, bd) + NORM_EPS) * qg_ref[...]
    qa = qa * _tile4(cos_a) + _rot_half(qa, AXIAL_HALF // 2, AXIAL_HALF) * _tile4(sin_a)
    qa_ref[...] = qa.astype(BF16)
    c0 += A_Q_W

    ka = proj[:, c0:c0 + A_KV_W]
    ka = ka * lax.rsqrt(_group_mean_sq(ka, bd[:A_KV_W, :A_KV_W]) + NORM_EPS) * kg_ref[...]
    ka = ka * cos_a + _rot_half(ka, AXIAL_HALF // 2, AXIAL_HALF) * sin_a
    ka_ref[...] = ka.astype(BF16)
    c0 += A_KV_W

    va_ref[0] = proj[:, c0:c0 + A_KV_W].T.astype(BF16)
    c0 += A_KV_W

    qb = proj[:, c0:c0 + B_QK_W]
    qb = qb * _tile4(cos_b) + _rot_half(qb, ROPE_DIMS // 2, HEAD_DIM) * _tile4(sin_b)
    qb_ref[...] = (qb * Q_SCALE).astype(BF16)
    c0 += B_QK_W

    kb = proj[:, c0:c0 + B_QK_W]
    kb = kb * _tile4(cos_b) + _rot_half(kb, ROPE_DIMS // 2, HEAD_DIM) * _tile4(sin_b)
    kb_ref[...] = kb.astype(BF16)
    c0 += B_QK_W

    vb_ref[0] = proj[:, c0:c0 + B_V_W].T.astype(BF16)


def _rope_tables(seq):
    f32 = np.float32
    rows = seq // GRID_W
    row = np.repeat(np.arange(rows, dtype=f32), GRID_W)
    col = np.tile(np.arange(GRID_W, dtype=f32), rows)
    pos = np.arange(seq, dtype=f32)
    inv_ax = (f32(AXIAL_THETA) ** (-np.arange(0, AXIAL_HALF, 2, dtype=f32) / f32(AXIAL_HALF))).astype(f32)
    inv_p = (f32(ROPE_THETA) ** (-np.arange(0, ROPE_DIMS, 2, dtype=f32) / f32(ROPE_DIMS))).astype(f32)
    row_ang = row[:, None] * inv_ax[None, :]
    col_ang = col[:, None] * inv_ax[None, :]
    pos_ang = pos[:, None] * inv_p[None, :]
    cr, sr = np.cos(row_ang), np.sin(row_ang)
    cc, sc = np.cos(col_ang), np.sin(col_ang)
    cp, sp = np.cos(pos_ang), np.sin(pos_ang)
    rest = HEAD_DIM - ROPE_DIMS
    cos_a = np.concatenate([cr, cr, cc, cc], axis=1)
    sin_a = np.concatenate([-sr, sr, -sc, sc], axis=1)
    cos_b = np.concatenate([cp, cp, np.ones((seq, rest), f32)], axis=1)
    sin_b = np.concatenate([-sp, sp, np.zeros((seq, rest), f32)], axis=1)
    two = lambda t: np.concatenate([t, t], axis=1)
    tab = np.concatenate([two(cos_a), two(sin_a), two(cos_b), two(sin_b)], axis=1).astype(f32)
    return jnp.asarray(tab)


def _proj_call(x, g, w_in, qg, kg, tab, bd, tm):
    seq = x.shape[0]
    row = lambda i: (i, 0)
    fix = lambda i: (0, 0)
    n = seq // tm
    rows_out = lambda w: (pl.BlockSpec((tm, w), row), jax.ShapeDtypeStruct((seq, w), BF16))
    cols_out = lambda w: (pl.BlockSpec((1, w, tm), lambda i: (i, 0, 0)), jax.ShapeDtypeStruct((n, w, tm), BF16))
    outs = [rows_out(A_Q_W), rows_out(A_KV_W), cols_out(A_KV_W), rows_out(B_QK_W), rows_out(B_QK_W), cols_out(B_V_W)]
    return pl.pallas_call(
        _proj_kernel,
        grid=(seq // tm,),
        in_specs=[
            pl.BlockSpec((tm, D_MODEL), row),
            pl.BlockSpec((1, D_MODEL), fix),
            pl.BlockSpec((D_MODEL, IN_COLS), fix),
            pl.BlockSpec((1, A_Q_W), fix),
            pl.BlockSpec((1, A_KV_W), fix),
            pl.BlockSpec((tm, 512), row),
            pl.BlockSpec((A_Q_W, A_Q_W), fix),
        ],
        out_specs=[o[0] for o in outs],
        out_shape=[o[1] for o in outs],
        compiler_params=_cparams(("parallel",)),
        name="proj",
    )(x, g, w_in, qg, kg, tab, bd)


SOFTMAX_ROWS = 16
N_SLOTS = 3
STEPS_PER_TRIP = 3
ST_M, ST_ALPHA, ST_NEXT_MAX, ST_L = 0, 1, 2, 8


def _flash_scratch(nq, tk, dv):
    return ([pltpu.VMEM((tk, nq), F32)] * N_SLOTS + [pltpu.VMEM((tk, nq), BF16)] * N_SLOTS
            + [pltpu.VMEM((dv, nq), F32), pltpu.VMEM((16, nq), F32)])


def _flash_cols(qst, k_ref, vt_ref, scratch):
    s_ref, p_ref = scratch[:N_SLOTS], scratch[N_SLOTS:2 * N_SLOTS]
    acc_ref, st_ref = scratch[2 * N_SLOTS:]
    nq = qst.shape[1]
    tk = s_ref[0].shape[0]
    per = tk // vt_ref.shape[2]
    n_chunks = vt_ref.shape[0] // per
    assert n_chunks >= N_SLOTS and tk % SOFTMAX_ROWS == 0
    n_sub = tk // SOFTMAX_ROWS
    row = lambda r, n=1: slice(r, r + n)
    sub = lambda slot, b: s_ref[slot][b * SOFTMAX_ROWS:(b + 1) * SOFTMAX_ROWS, :]

    def scores(j, slot):
        start = j * tk
        start = start if isinstance(start, int) else pl.multiple_of(start, tk)
        s_ref[slot][...] = jnp.dot(k_ref[pl.ds(start, tk), :], qst, preferred_element_type=F32)

    def column_max(slot):
        mx = sub(slot, 0)
        for b in range(1, n_sub):
            mx = jnp.maximum(mx, sub(slot, b))
        st_ref[row(ST_NEXT_MAX), :] = jnp.max(mx, axis=0, keepdims=True)

    def weighted(j, slot, alpha):
        vt = jnp.concatenate([vt_ref[per * j + c] for c in range(per)], axis=1)
        acc_ref[...] = alpha * acc_ref[...] + jnp.dot(vt, p_ref[slot][...], preferred_element_type=F32)

    def step(j, slot, with_scores=True, with_max=True):
        a_prev = st_ref[row(ST_ALPHA), :]
        m_old = st_ref[row(ST_M), :]
        m_new = jnp.maximum(m_old, st_ref[row(ST_NEXT_MAX), :])
        alpha = jnp.exp2(m_old - m_new)
        if with_scores:
            scores(j + 2, (slot + 2) % N_SLOTS)
        psum = jnp.zeros((8, nq), F32)
        for b in range(n_sub):
            p = jnp.exp2(sub(slot, b) - m_new)
            psum = psum + jnp.sum(p.reshape(SOFTMAX_ROWS // 8, 8, nq), axis=0)
            p_ref[slot][b * SOFTMAX_ROWS:(b + 1) * SOFTMAX_ROWS, :] = p.astype(BF16)
        weighted(jnp.maximum(j - 1, 0), (slot + 2) % N_SLOTS, a_prev)
        st_ref[row(ST_L, 8), :] = alpha * st_ref[row(ST_L, 8), :] + psum
        st_ref[row(ST_M), :] = m_new
        st_ref[row(ST_ALPHA), :] = alpha
        if with_max:
            column_max((slot + 1) % N_SLOTS)

    scores(0, 0)
    scores(1, 1)
    st_ref[...] = jnp.zeros_like(st_ref)
    st_ref[row(ST_M), :] = jnp.full((1, nq), -jnp.inf, F32)
    acc_ref[...] = jnp.zeros_like(acc_ref)
    p_ref[N_SLOTS - 1][...] = jnp.zeros((tk, nq), BF16)
    column_max(0)
    n_full = n_chunks - 2
    n_trips = n_full // STEPS_PER_TRIP
    lead = n_full - n_trips * STEPS_PER_TRIP
    for j in range(lead):
        step(j, j % N_SLOTS)

    def body(t, carry):
        for u in range(STEPS_PER_TRIP):
            step(lead + t * STEPS_PER_TRIP + u, (lead + u) % N_SLOTS)
        return carry

    lax.fori_loop(0, n_trips, body, 0)
    step(n_chunks - 2, (n_chunks - 2) % N_SLOTS, with_scores=False)
    step(n_chunks - 1, (n_chunks - 1) % N_SLOTS, with_scores=False, with_max=False)
    weighted(n_chunks - 1, (n_chunks - 1) % N_SLOTS, st_ref[row(ST_ALPHA), :])
    return acc_ref[...], jnp.sum(st_ref[row(ST_L, 8), :], axis=0, keepdims=True)


MXU_COLS = 256
FAST_STEPS_PER_TRIP = 6
P_LIMIT = 2.0 ** 60


def _flash_scratch_fast(nq, tk, dv):
    return [pltpu.VMEM((tk, nq), BF16)] * N_SLOTS + [pltpu.VMEM((dv, nq), F32), pltpu.VMEM((16, nq), F32)]


def _flash_cols_fast(qst, k_ref, vt_ref, scratch):
    p_ref = scratch[:N_SLOTS]
    acc_ref, st_ref = scratch[N_SLOTS:]
    nq = qst.shape[1]
    tk = p_ref[0].shape[0]
    per = tk // vt_ref.shape[2]
    n_chunks = vt_ref.shape[0] // per
    n_sub = tk // SOFTMAX_ROWS
    row = lambda r, n=1: slice(r, r + n)
    st_scale, st_flag = ST_ALPHA, ST_NEXT_MAX
    q_cols = [qst[:, c:c + MXU_COLS] for c in range(0, nq, MXU_COLS)]

    def keys(j):
        start = j * tk
        start = start if isinstance(start, int) else pl.multiple_of(start, tk)
        return k_ref[pl.ds(start, tk), :]

    def weighted(j, slot, scale):
        vt = jnp.concatenate([vt_ref[per * j + c] for c in range(per)], axis=1)
        acc_ref[...] = (acc_ref[...] + jnp.dot(vt, p_ref[slot][...], preferred_element_type=F32)) * scale

    def step(j, slot):
        scale_prev = st_ref[row(st_scale), :]
        m_ref = st_ref[row(ST_M), :]
        k = keys(j)
        pmaxs, psums = [], []
        for c, qc in enumerate(q_cols):
            cols = slice(c * MXU_COLS, (c + 1) * MXU_COLS)
            s = jnp.dot(k, qc, preferred_element_type=F32)
            mc = m_ref[:, cols]
            pm = None
            psum = jnp.zeros((8, MXU_COLS), F32)
            for b in range(n_sub):
                rows = slice(b * SOFTMAX_ROWS, (b + 1) * SOFTMAX_ROWS)
                p = jnp.exp2(s[rows] - mc)
                pm = p if pm is None else jnp.maximum(pm, p)
                psum = psum + jnp.sum(p.reshape(SOFTMAX_ROWS // 8, 8, MXU_COLS), axis=0)
                p_ref[slot][rows, cols] = p.astype(BF16)
            pmaxs.append(jnp.max(pm, axis=0, keepdims=True))
            psums.append(psum)
        pmax = jnp.concatenate(pmaxs, axis=1)
        weighted(jnp.maximum(j - 1, 0), (slot + 2) % N_SLOTS, scale_prev)
        big = jnp.maximum(pmax, 1.0)
        scale = 1.0 / big
        st_ref[row(ST_L, 8), :] = (st_ref[row(ST_L, 8), :] + jnp.concatenate(psums, axis=1)) * scale
        st_ref[row(ST_M), :] = m_ref + jnp.log2(big)
        st_ref[row(st_scale), :] = scale
        st_ref[row(st_flag), :] = jnp.maximum(st_ref[row(st_flag), :], pmax)

    k0 = keys(0)
    m0 = jnp.concatenate([jnp.max(jnp.dot(k0, qc, preferred_element_type=F32), axis=0, keepdims=True)
                          for qc in q_cols], axis=1)
    st_ref[...] = jnp.zeros_like(st_ref)
    st_ref[row(ST_M), :] = m0
    st_ref[row(st_scale), :] = jnp.ones((1, nq), F32)
    acc_ref[...] = jnp.zeros_like(acc_ref)
    p_ref[N_SLOTS - 1][...] = jnp.zeros((tk, nq), BF16)
    n_trips = n_chunks // FAST_STEPS_PER_TRIP
    lead = n_chunks - n_trips * FAST_STEPS_PER_TRIP
    for j in range(lead):
        step(j, j % N_SLOTS)

    def body(t, carry):
        for u in range(FAST_STEPS_PER_TRIP):
            step(lead + t * FAST_STEPS_PER_TRIP + u, (lead + u) % N_SLOTS)
        return carry

    lax.fori_loop(0, n_trips, body, 0)
    weighted(n_chunks - 1, (n_chunks - 1) % N_SLOTS, st_ref[row(st_scale), :])
    return acc_ref[...], jnp.sum(st_ref[row(ST_L, 8), :], axis=0, keepdims=True), st_ref[row(st_flag), :]


def _attend(qst, k_ref, vt_ref, rest, fast):
    if not fast:
        return _flash_cols(qst, k_ref, vt_ref, rest)
    flag_ref, scratch = rest[0], rest[1:]
    acc_t, l, flag = _flash_cols_fast(qst, k_ref, vt_ref, scratch)
    flag_ref[...] = jnp.broadcast_to(flag, flag_ref.shape)
    return acc_t, l


def _attn_outs(out_spec, out_shape, grid_axes, n_i, nq, fast):
    if not fast:
        return out_spec, out_shape
    flag_spec = pl.BlockSpec((8, nq), lambda a, i: (a * n_i + i, 0))
    return [out_spec, flag_spec], [out_shape, jax.ShapeDtypeStruct((grid_axes * n_i * 8, nq), F32)]


def _gqa_kernel(q_ref, k_ref, vt_ref, o_ref, *rest, tq, fast):
    g = pl.program_id(0)
    lane = lax.broadcasted_iota(jnp.int32, (tq, LANES), 1)
    in_g = (lane // HEAD_DIM) == g
    rows = []
    for hh in range(A_GROUP):
        qp = q_ref[:, LANES * (hh // 2):LANES * (hh // 2 + 1)].astype(F32)
        aligned = jnp.where((hh % 2) == g, qp, pltpu.roll(qp, HEAD_DIM, 1))
        rows.append(jnp.where(in_g, aligned, 0.0))
    qs = jnp.concatenate(rows, axis=0)
    acc_t, l = _attend(qs.T.astype(BF16), k_ref, vt_ref, rest, fast)
    o_t = acc_t / l
    o = jnp.concatenate([o_t, jnp.zeros_like(o_t)], axis=0).T
    left_half = lane < HEAD_DIM
    for p in range(A_GROUP // 2):
        a = o[(2 * p) * tq:(2 * p + 1) * tq]
        b = o[(2 * p + 1) * tq:(2 * p + 2) * tq]
        o_ref[:, LANES * p:LANES * (p + 1)] = jnp.where(left_half, a, pltpu.roll(b, HEAD_DIM, 1)).astype(BF16)


def _gqa_call(qa, ka, vat, tq, tk, fast):
    seq = qa.shape[0]
    nq = A_GROUP * tq
    out_specs, out_shape = _attn_outs(pl.BlockSpec((tq, A_GROUP * HEAD_DIM), lambda g, i: (i, g)),
                                      jax.ShapeDtypeStruct((seq, A_Q_W), BF16), A_KV_HEADS, seq // tq, nq, fast)
    return pl.pallas_call(
        functools.partial(_gqa_kernel, tq=tq, fast=fast),
        grid=(A_KV_HEADS, seq // tq),
        in_specs=[
            pl.BlockSpec((tq, A_GROUP * HEAD_DIM), lambda g, i: (i, g)),
            pl.BlockSpec((seq, A_KV_W), lambda g, i: (0, 0)),
            pl.BlockSpec((vat.shape[0], HEAD_DIM, vat.shape[2]), lambda g, i: (0, g, 0)),
        ],
        out_specs=out_specs,
        out_shape=out_shape,
        scratch_shapes=(_flash_scratch_fast if fast else _flash_scratch)(nq, tk, HEAD_DIM),
        compiler_params=_cparams(("parallel", "parallel")),
        name="gqa_fast" if fast else "gqa",
    )(qa, ka, vat)


def _diff_kernel(lam_ref, q_ref, k_ref, vt_ref, sg_ref, o_ref, *rest, tq, lambda_init, fast):
    lane = lax.broadcasted_iota(jnp.int32, (tq, LANES), 1)
    q = q_ref[...].astype(F32)
    qs = jnp.concatenate([jnp.where(lane < HEAD_DIM, q, 0.0),
                          jnp.where(lane >= HEAD_DIM, q, 0.0)], axis=0)
    acc_t, l = _attend(qs.T.astype(BF16), k_ref, vt_ref, rest, fast)
    o = (acc_t / l).T
    lv = lam_ref[...]
    lam = (jnp.exp(jnp.sum(lv[0:1] * lv[1:2], axis=-1, keepdims=True))
           - jnp.exp(jnp.sum(lv[2:3] * lv[3:4], axis=-1, keepdims=True)) + lambda_init)
    ob = o[:tq] - lam * o[tq:]
    ob = _rms(ob, sg_ref[...], SUBLN_EPS) * (1.0 - lambda_init)
    o_ref[...] = ob.astype(BF16)


def _diff_call(lam_vecs, qb, kb, vbt, subln_g, tq, tk, lambda_init, fast):
    seq = qb.shape[0]
    n_slabs, _, slab = vbt.shape
    out_specs, out_shape = _attn_outs(pl.BlockSpec((tq, LANES), lambda h, i: (i, h)),
                                      jax.ShapeDtypeStruct((seq, B_V_W), BF16), B_HEADS, seq // tq, 2 * tq, fast)
    return pl.pallas_call(
        functools.partial(_diff_kernel, tq=tq, lambda_init=lambda_init, fast=fast),
        grid=(B_HEADS, seq // tq),
        in_specs=[
            pl.BlockSpec((4, HEAD_DIM), lambda h, i: (0, 0)),
            pl.BlockSpec((tq, LANES), lambda h, i: (i, h)),
            pl.BlockSpec((seq, LANES), lambda h, i: (0, h)),
            pl.BlockSpec((n_slabs, B_V_DIM, slab), lambda h, i: (0, h, 0)),
            pl.BlockSpec((1, B_V_DIM), lambda h, i: (0, 0)),
        ],
        out_specs=out_specs,
        out_shape=out_shape,
        scratch_shapes=(_flash_scratch_fast if fast else _flash_scratch)(2 * tq, tk, B_V_DIM),
        compiler_params=_cparams(("parallel", "parallel")),
        name="diff_fast" if fast else "diff",
    )(lam_vecs, qb, kb, vbt, subln_g)


def _mid_kernel(oa_ref, ob_ref, x_ref, wo_ref, g_ref, wq_ref, sk_ref, h_ref, xn_ref, sc_ref):
    o = jnp.concatenate([oa_ref[...], ob_ref[...]], axis=1)
    h = x_ref[...] + jnp.dot(o, wo_ref[...], preferred_element_type=F32)
    h_ref[...] = h
    xn = _rms(h, g_ref[...], NORM_EPS).astype(BF16)
    xn_ref[...] = xn
    q = jnp.dot(xn, wq_ref[...], preferred_element_type=F32).astype(BF16)
    for hp in range(2 * PEER_HEADS):
        sc_ref[hp] = lax.dot_general(sk_ref[hp], q[:, PEER_HALF * hp:PEER_HALF * (hp + 1)],
                                     NT_DIMS, preferred_element_type=F32)


def _mid_call(oa, ob, x, w_out, g, w_query, sub_keys, tm):
    seq = x.shape[0]
    row = lambda i: (i, 0)
    fix = lambda i: (0, 0)
    nq = 2 * PEER_HEADS * PEER_HALF
    return pl.pallas_call(
        _mid_kernel,
        grid=(seq // tm,),
        in_specs=[
            pl.BlockSpec((tm, A_Q_W), row),
            pl.BlockSpec((tm, B_V_W), row),
            pl.BlockSpec((tm, D_MODEL), row),
            pl.BlockSpec((D_MODEL, D_MODEL), fix),
            pl.BlockSpec((1, D_MODEL), fix),
            pl.BlockSpec((D_MODEL, nq), fix),
            pl.BlockSpec((2 * PEER_HEADS, PEER_KEYS, PEER_HALF), lambda i: (0, 0, 0)),
        ],
        out_specs=[
            pl.BlockSpec((tm, D_MODEL), row),
            pl.BlockSpec((tm, D_MODEL), row),
            pl.BlockSpec((2 * PEER_HEADS, PEER_KEYS, tm), lambda i: (0, 0, i)),
        ],
        out_shape=[
            jax.ShapeDtypeStruct((seq, D_MODEL), F32),
            jax.ShapeDtypeStruct((seq, D_MODEL), BF16),
            jax.ShapeDtypeStruct((2 * PEER_HEADS, PEER_KEYS, seq), F32),
        ],
        compiler_params=_cparams(("parallel",)),
        name="mid",
    )(oa, ob, x, w_out, g, w_query, sub_keys)


N_TOP = PEER_TOPK + 1
_CAND = [(a, b) for a in range(N_TOP) for b in range(N_TOP) if (a + 1) * (b + 1) <= N_TOP]


def _top_vals(s, n):
    vals = []
    for r in range(n):
        m = jnp.max(s, axis=0, keepdims=True)
        vals.append(m)
        if r + 1 < n:
            s = jnp.where(s >= m, -jnp.inf, s)
    return vals


def _topk_kernel(sc_ref, e1_ref, e2_ref, tau_ref):
    t = sc_ref.shape[-1]
    taus = []
    for h in range(PEER_HEADS):
        s1 = sc_ref[2 * h]
        s2 = sc_ref[2 * h + 1]
        v1 = _top_vals(s1, N_TOP)
        v2 = _top_vals(s2, N_TOP)
        cands = [v1[a] + v2[b] for (a, b) in _CAND]
        pad = (-len(cands)) % 8
        cands += [jnp.full((1, t), -jnp.inf, F32)] * pad
        c = jnp.concatenate(cands, axis=0)
        top = _top_vals(c, N_TOP)
        t16, t17 = top[PEER_TOPK - 1], top[PEER_TOPK]
        best = v1[0] + v2[0]
        z = jnp.sum(jnp.where(c >= t16, jnp.exp(c - best), 0.0), axis=0, keepdims=True)
        rz = 1.0 / z
        e1_ref[h] = jnp.exp(s1 - v1[0]) * rz
        e2_ref[h] = jnp.exp(s2 - v2[0])
        taus.append(0.5 * (jnp.exp(t16 - best) + jnp.exp(t17 - best)) * rz)
    tau_ref[...] = jnp.concatenate(taus, axis=0)


def _topk_call(sc, tt):
    seq = sc.shape[-1]
    blk = lambda i: (0, 0, i)
    return pl.pallas_call(
        _topk_kernel,
        grid=(seq // tt,),
        in_specs=[pl.BlockSpec((2 * PEER_HEADS, PEER_KEYS, tt), blk)],
        out_specs=[
            pl.BlockSpec((PEER_HEADS, PEER_KEYS, tt), blk),
            pl.BlockSpec((PEER_HEADS, PEER_KEYS, tt), blk),
            pl.BlockSpec((PEER_HEADS, tt), lambda i: (0, i)),
        ],
        out_shape=[
            jax.ShapeDtypeStruct((PEER_HEADS, PEER_KEYS, seq), F32),
            jax.ShapeDtypeStruct((PEER_HEADS, PEER_KEYS, seq), F32),
            jax.ShapeDtypeStruct((PEER_HEADS, seq), F32),
        ],
        compiler_params=_cparams(("parallel",)),
        name="topk",
    )(sc)


def _peer_kernel(xn_ref, u_ref, vt_ref, e1_ref, e2_ref, tau_ref, h_ref, g_ref, o_ref,
                 acc_ref, a_ref, w_ref, *, eb):
    e = pl.program_id(1)
    tt = xn_ref.shape[0]
    inv_sqrt2 = 1.0 / math.sqrt(2.0)
    a_ref[...] = lax.dot_general(u_ref[...], xn_ref[...], NT_DIMS,
                                 preferred_element_type=F32)
    for ii in range(eb // PEER_KEYS):
        i = e * (eb // PEER_KEYS) + ii
        rows = slice(PEER_KEYS * ii, PEER_KEYS * (ii + 1))
        e1_rows = [e1_ref[h, pl.ds(i, 1), :] for h in range(PEER_HEADS)]
        for tc in range(tt // LANES):
            cols = slice(LANES * tc, LANES * (tc + 1))
            a = a_ref[rows, cols]
            act = 0.5 * a * (1.0 + lax.erf(a * inv_sqrt2))
            gate = None
            for h in range(PEER_HEADS):
                p = e2_ref[h, :, cols] * e1_rows[h][:, cols]
                term = jnp.where(p >= tau_ref[h:h + 1, cols], p, 0.0)
                gate = term if gate is None else gate + term
            w_ref[rows, cols] = (gate * act).astype(BF16)
    part = lax.dot_general(vt_ref[...], w_ref[...], (((0,), (0,)), ((), ())), preferred_element_type=F32)

    @pl.when(e == 0)
    def _():
        acc_ref[...] = part

    @pl.when(e > 0)
    def _():
        acc_ref[...] += part

    @pl.when(e == pl.num_programs(1) - 1)
    def _():
        out = h_ref[...] + acc_ref[...].T
        o_ref[...] = _rms(out, g_ref[...], NORM_EPS)


def _peer_call(xn, u, vt, e1, e2, tau, h, g, tt, eb):
    seq = xn.shape[0]
    n_exp = u.shape[0]
    tok = lambda t, e: (t, 0)
    tok3 = lambda t, e: (0, 0, t)
    return pl.pallas_call(
        functools.partial(_peer_kernel, eb=eb),
        grid=(seq // tt, n_exp // eb),
        in_specs=[
            pl.BlockSpec((tt, D_MODEL), tok),
            pl.BlockSpec((eb, D_MODEL), lambda t, e: (e, 0)),
            pl.BlockSpec((eb, D_MODEL), lambda t, e: (e, 0)),
            pl.BlockSpec((PEER_HEADS, PEER_KEYS, tt), tok3),
            pl.BlockSpec((PEER_HEADS, PEER_KEYS, tt), tok3),
            pl.BlockSpec((PEER_HEADS, tt), lambda t, e: (0, t)),
            pl.BlockSpec((tt, D_MODEL), tok),
            pl.BlockSpec((1, D_MODEL), lambda t, e: (0, 0)),
        ],
        out_specs=pl.BlockSpec((tt, D_MODEL), tok),
        out_shape=jax.ShapeDtypeStruct((seq, D_MODEL), F32),
        scratch_shapes=[pltpu.VMEM((D_MODEL, tt), F32), pltpu.VMEM((eb, tt), F32), pltpu.VMEM((eb, tt), BF16)],
        compiler_params=_cparams(("parallel", "arbitrary")),
        name="peer",
    )(xn, u, vt, e1, e2, tau, h, g)


def _tiles(seq):
    big = seq >= 4096
    return dict(
        tm=512 if big else 256,
        tq_a=128, tq_b=256,
        tk=1024 if big else 512,
        tt_topk=256, tt_peer=512 if big else 256, eb=2048,
    )


def kernel(x, norm_attn_g, w_in, q_norm_g, k_norm_g, lambda_q1, lambda_k1, lambda_q2, lambda_k2,
           subln_g, w_out, norm_ffn_g, w_query, sub_keys, expert_u, expert_v, norm_final_g):
    batch, seq, d = x.shape
    assert batch == 1 and d == D_MODEL and norm_attn_g.shape[0] == 1
    t = _tiles(seq)
    lambda_init = 0.8 - 0.6 * math.exp(-0.3 * 0)
    x2 = x.reshape(seq, d)

    tab = _rope_tables(seq)
    blk = np.arange(A_Q_W) // HEAD_DIM
    bd = jnp.asarray(np.where(blk[:, None] == blk[None, :], 1.0 / HEAD_DIM, 0.0), dtype=BF16)
    qg = (jnp.tile(q_norm_g[0], A_Q_HEADS) * Q_SCALE).reshape(1, A_Q_W)
    kg = jnp.tile(k_norm_g[0], A_KV_HEADS).reshape(1, A_KV_W)

    qa, ka, vat, qb, kb, vbt = _proj_call(x2, norm_attn_g, w_in[0].astype(BF16), qg, kg, tab, bd, t["tm"])
    lam_vecs = jnp.concatenate([lambda_q1, lambda_k1, lambda_q2, lambda_k2], axis=0)
    gqa = functools.partial(_gqa_call, qa, ka, vat, t["tq_a"], t["tk"])
    diff = functools.partial(_diff_call, lam_vecs, qb, kb, vbt, subln_g, t["tq_b"], t["tk"], lambda_init)
    oa, flag_a = gqa(True)
    ob, flag_b = diff(True)
    trusted = jnp.all(flag_a < P_LIMIT) & jnp.all(flag_b < P_LIMIT)
    oa, ob = lax.cond(trusted, lambda: (oa, ob), lambda: (gqa(False), diff(False)))

    sk = sub_keys[0].reshape(2 * PEER_HEADS, PEER_KEYS, PEER_HALF).astype(BF16)
    h, xn, sc = _mid_call(oa, ob, x2, w_out[0].astype(BF16), norm_ffn_g, w_query[0].astype(BF16), sk, t["tm"])
    e1, e2, tau = _topk_call(sc, t["tt_topk"])
    out = _peer_call(xn, expert_u[0].astype(BF16), expert_v[0].astype(BF16), e1, e2, tau, h,
                     norm_final_g.reshape(1, d), t["tt_peer"], t["eb"])
    return out.reshape(batch, seq, d)
```

```python
import functools
import math

import jax
import jax.numpy as jnp
import numpy as np
from jax import lax
from jax.experimental import pallas as pl
from jax.experimental.pallas import tpu as pltpu

F32 = jnp.float32
BF16 = jnp.bfloat16

D_MODEL = 1024
HEAD_DIM = 64
A_Q_HEADS = 8
A_KV_HEADS = 2
A_GROUP = A_Q_HEADS // A_KV_HEADS
B_HEADS = 4
B_V_DIM = 2 * HEAD_DIM
GRID_W = 64
AXIAL_THETA = 10000.0
AXIAL_HALF = HEAD_DIM // 2
ROPE_THETA = 500000.0
ROPE_DIMS = HEAD_DIM // 4
NORM_EPS = 1e-6
SUBLN_EPS = 1e-5
A_Q_W = A_Q_HEADS * HEAD_DIM
A_KV_W = A_KV_HEADS * HEAD_DIM
B_QK_W = 2 * B_HEADS * HEAD_DIM
B_V_W = B_HEADS * B_V_DIM
IN_COLS = A_Q_W + 2 * A_KV_W + 2 * B_QK_W + B_V_W
PEER_HEADS = 8
PEER_KEYS = 128
PEER_HALF = 128
PEER_TOPK = 16
LANES = 128

NT_DIMS = (((1,), (1,)), ((), ()))
Q_SCALE = (HEAD_DIM ** -0.5) * math.log2(math.e)

VMEM_LIMIT = 56 * 1024 * 1024


def _cparams(sem):
    return pltpu.CompilerParams(dimension_semantics=sem, vmem_limit_bytes=VMEM_LIMIT)


def _rms(x, g, eps):
    return x * lax.rsqrt(jnp.mean(x * x, axis=-1, keepdims=True) + eps) * g


def _group_mean_sq(v, bd):
    v2 = v * v
    hi = v2.astype(BF16)
    lo = (v2 - hi.astype(F32)).astype(BF16)
    return (jnp.dot(hi, bd, preferred_element_type=F32)
            + jnp.dot(lo, bd, preferred_element_type=F32))


def _rot_half(v, half, group):
    width = v.shape[-1]
    lane = lax.broadcasted_iota(jnp.int32, v.shape, 1)
    fwd = pltpu.roll(v, width - half, 1)
    bwd = pltpu.roll(v, half, 1)
    return jnp.where((lane % group) < half, fwd, bwd)


def _tile4(t):
    return jnp.concatenate([t, t, t, t], axis=1)


def _proj_kernel(x_ref, g_ref, w_ref, qg_ref, kg_ref, tab_ref, bd_ref,
                 qa_ref, ka_ref, va_ref, qb_ref, kb_ref, vb_ref):
    xn = _rms(x_ref[...], g_ref[...], NORM_EPS)
    proj = jnp.dot(xn.astype(BF16), w_ref[...], preferred_element_type=F32)
    tab = tab_ref[...]
    cos_a, sin_a = tab[:, 0:128], tab[:, 128:256]
    cos_b, sin_b = tab[:, 256:384], tab[:, 384:512]
    bd = bd_ref[...]

    c0 = 0
    qa = proj[:, c0:c0 + A_Q_W]
    qa = qa * lax.rsqrt(_group_mean_sq(qa, bd) + NORM_EPS) * qg_ref[...]
    qa = qa * _tile4(cos_a) + _rot_half(qa, AXIAL_HALF // 2, AXIAL_HALF) * _tile4(sin_a)
    qa_ref[...] = qa.astype(BF16)
    c0 += A_Q_W

    ka = proj[:, c0:c0 + A_KV_W]
    ka = ka * lax.rsqrt(_group_mean_sq(ka, bd[:A_KV_W, :A_KV_W]) + NORM_EPS) * kg_ref[...]
    ka = ka * cos_a + _rot_half(ka, AXIAL_HALF // 2, AXIAL_HALF) * sin_a
    ka_ref[...] = ka.astype(BF16)
    c0 += A_KV_W

    va_ref[0] = proj[:, c0:c0 + A_KV_W].T.astype(BF16)
    c0 += A_KV_W

    qb = proj[:, c0:c0 + B_QK_W]
    qb = qb * _tile4(cos_b) + _rot_half(qb, ROPE_DIMS // 2, HEAD_DIM) * _tile4(sin_b)
    qb_ref[...] = (qb * Q_SCALE).astype(BF16)
    c0 += B_QK_W

    kb = proj[:, c0:c0 + B_QK_W]
    kb = kb * _tile4(cos_b) + _rot_half(kb, ROPE_DIMS // 2, HEAD_DIM) * _tile4(sin_b)
    kb_ref[...] = kb.astype(BF16)
    c0 += B_QK_W

    vb_ref[0] = proj[:, c0:c0 + B_V_W].T.astype(BF16)


def _rope_tables(seq):
    f32 = np.float32
    rows = seq // GRID_W
    row = np.repeat(np.arange(rows, dtype=f32), GRID_W)
    col = np.tile(np.arange(GRID_W, dtype=f32), rows)
    pos = np.arange(seq, dtype=f32)
    inv_ax = (f32(AXIAL_THETA) ** (-np.arange(0, AXIAL_HALF, 2, dtype=f32) / f32(AXIAL_HALF))).astype(f32)
    inv_p = (f32(ROPE_THETA) ** (-np.arange(0, ROPE_DIMS, 2, dtype=f32) / f32(ROPE_DIMS))).astype(f32)
    row_ang = row[:, None] * inv_ax[None, :]
    col_ang = col[:, None] * inv_ax[None, :]
    pos_ang = pos[:, None] * inv_p[None, :]
    cr, sr = np.cos(row_ang), np.sin(row_ang)
    cc, sc = np.cos(col_ang), np.sin(col_ang)
    cp, sp = np.cos(pos_ang), np.sin(pos_ang)
    rest = HEAD_DIM - ROPE_DIMS
    cos_a = np.concatenate([cr, cr, cc, cc], axis=1)
    sin_a = np.concatenate([-sr, sr, -sc, sc], axis=1)
    cos_b = np.concatenate([cp, cp, np.ones((seq, rest), f32)], axis=1)
    sin_b = np.concatenate([-sp, sp, np.zeros((seq, rest), f32)], axis=1)
    two = lambda t: np.concatenate([t, t], axis=1)
    tab = np.concatenate([two(cos_a), two(sin_a), two(cos_b), two(sin_b)], axis=1).astype(f32)
    return jnp.asarray(tab)


def _proj_call(x, g, w_in, qg, kg, tab, bd, tm):
    seq = x.shape[0]
    row = lambda i: (i, 0)
    fix = lambda i: (0, 0)
    n = seq // tm
    rows_out = lambda w: (pl.BlockSpec((tm, w), row), jax.ShapeDtypeStruct((seq, w), BF16))
    cols_out = lambda w: (pl.BlockSpec((1, w, tm), lambda i: (i, 0, 0)), jax.ShapeDtypeStruct((n, w, tm), BF16))
    outs = [rows_out(A_Q_W), rows_out(A_KV_W), cols_out(A_KV_W), rows_out(B_QK_W), rows_out(B_QK_W), cols_out(B_V_W)]
    return pl.pallas_call(
        _proj_kernel,
        grid=(seq // tm,),
        in_specs=[
            pl.BlockSpec((tm, D_MODEL), row),
            pl.BlockSpec((1, D_MODEL), fix),
            pl.BlockSpec((D_MODEL, IN_COLS), fix),
            pl.BlockSpec((1, A_Q_W), fix),
            pl.BlockSpec((1, A_KV_W), fix),
            pl.BlockSpec((tm, 512), row),
            pl.BlockSpec((A_Q_W, A_Q_W), fix),
        ],
        out_specs=[o[0] for o in outs],
        out_shape=[o[1] for o in outs],
        compiler_params=_cparams(("parallel",)),
        name="proj",
    )(x, g, w_in, qg, kg, tab, bd)


SOFTMAX_ROWS = 16
N_SLOTS = 3
STEPS_PER_TRIP = 3
ST_M, ST_ALPHA, ST_NEXT_MAX, ST_L = 0, 1, 2, 8


def _flash_scratch(nq, tk, dv):
    return ([pltpu.VMEM((tk, nq), F32)] * N_SLOTS + [pltpu.VMEM((tk, nq), BF16)] * N_SLOTS
            + [pltpu.VMEM((dv, nq), F32), pltpu.VMEM((16, nq), F32)])


def _flash_cols(qst, k_ref, vt_ref, scratch):
    s_ref, p_ref = scratch[:N_SLOTS], scratch[N_SLOTS:2 * N_SLOTS]
    acc_ref, st_ref = scratch[2 * N_SLOTS:]
    nq = qst.shape[1]
    tk = s_ref[0].shape[0]
    per = tk // vt_ref.shape[2]
    n_chunks = vt_ref.shape[0] // per
    assert n_chunks >= N_SLOTS and tk % SOFTMAX_ROWS == 0
    n_sub = tk // SOFTMAX_ROWS
    row = lambda r, n=1: slice(r, r + n)
    sub = lambda slot, b: s_ref[slot][b * SOFTMAX_ROWS:(b + 1) * SOFTMAX_ROWS, :]

    def scores(j, slot):
        start = j * tk
        start = start if isinstance(start, int) else pl.multiple_of(start, tk)
        s_ref[slot][...] = jnp.dot(k_ref[pl.ds(start, tk), :], qst, preferred_element_type=F32)

    def column_max(slot):
        mx = sub(slot, 0)
        for b in range(1, n_sub):
            mx = jnp.maximum(mx, sub(slot, b))
        st_ref[row(ST_NEXT_MAX), :] = jnp.max(mx, axis=0, keepdims=True)

    def weighted(j, slot, alpha):
        vt = jnp.concatenate([vt_ref[per * j + c] for c in range(per)], axis=1)
        acc_ref[...] = alpha * acc_ref[...] + jnp.dot(vt, p_ref[slot][...], preferred_element_type=F32)

    def step(j, slot, with_scores=True, with_max=True):
        a_prev = st_ref[row(ST_ALPHA), :]
        m_old = st_ref[row(ST_M), :]
        m_new = jnp.maximum(m_old, st_ref[row(ST_NEXT_MAX), :])
        alpha = jnp.exp2(m_old - m_new)
        if with_scores:
            scores(j + 2, (slot + 2) % N_SLOTS)
        psum = jnp.zeros((8, nq), F32)
        for b in range(n_sub):
            p = jnp.exp2(sub(slot, b) - m_new)
            psum = psum + jnp.sum(p.reshape(SOFTMAX_ROWS // 8, 8, nq), axis=0)
            p_ref[slot][b * SOFTMAX_ROWS:(b + 1) * SOFTMAX_ROWS, :] = p.astype(BF16)
        weighted(jnp.maximum(j - 1, 0), (slot + 2) % N_SLOTS, a_prev)
        st_ref[row(ST_L, 8), :] = alpha * st_ref[row(ST_L, 8), :] + psum
        st_ref[row(ST_M), :] = m_new
        st_ref[row(ST_ALPHA), :] = alpha
        if with_max:
            column_max((slot + 1) % N_SLOTS)

    scores(0, 0)
    scores(1, 1)
    st_ref[...] = jnp.zeros_like(st_ref)
    st_ref[row(ST_M), :] = jnp.full((1, nq), -jnp.inf, F32)
    acc_ref[...] = jnp.zeros_like(acc_ref)
    p_ref[N_SLOTS - 1][...] = jnp.zeros((tk, nq), BF16)
    column_max(0)
    n_full = n_chunks - 2
    n_trips = n_full // STEPS_PER_TRIP
    lead = n_full - n_trips * STEPS_PER_TRIP
    for j in range(lead):
        step(j, j % N_SLOTS)

    def body(t, carry):
        for u in range(STEPS_PER_TRIP):
            step(lead + t * STEPS_PER_TRIP + u, (lead + u) % N_SLOTS)
        return carry

    lax.fori_loop(0, n_trips, body, 0)
    step(n_chunks - 2, (n_chunks - 2) % N_SLOTS, with_scores=False)
    step(n_chunks - 1, (n_chunks - 1) % N_SLOTS, with_scores=False, with_max=False)
    weighted(n_chunks - 1, (n_chunks - 1) % N_SLOTS, st_ref[row(ST_ALPHA), :])
    return acc_ref[...], jnp.sum(st_ref[row(ST_L, 8), :], axis=0, keepdims=True)


MXU_COLS = 256
FAST_STEPS_PER_TRIP = 6
P_LIMIT = 2.0 ** 60


def _flash_scratch_fast(nq, tk, dv):
    return [pltpu.VMEM((tk, nq), BF16)] * N_SLOTS + [pltpu.VMEM((dv, nq), F32), pltpu.VMEM((16, nq), F32)]


def _flash_cols_fast(qst, k_ref, vt_ref, scratch):
    p_ref = scratch[:N_SLOTS]
    acc_ref, st_ref = scratch[N_SLOTS:]
    nq = qst.shape[1]
    tk = p_ref[0].shape[0]
    per = tk // vt_ref.shape[2]
    n_chunks = vt_ref.shape[0] // per
    n_sub = tk // SOFTMAX_ROWS
    row = lambda r, n=1: slice(r, r + n)
    st_scale, st_flag = ST_ALPHA, ST_NEXT_MAX
    q_cols = [qst[:, c:c + MXU_COLS] for c in range(0, nq, MXU_COLS)]

    def keys(j):
        start = j * tk
        start = start if isinstance(start, int) else pl.multiple_of(start, tk)
        return k_ref[pl.ds(start, tk), :]

    def weighted(j, slot, scale):
        vt = jnp.concatenate([vt_ref[per * j + c] for c in range(per)], axis=1)
        acc_ref[...] = (acc_ref[...] + jnp.dot(vt, p_ref[slot][...], preferred_element_type=F32)) * scale

    def step(j, slot):
        scale_prev = st_ref[row(st_scale), :]
        m_ref = st_ref[row(ST_M), :]
        k = keys(j)
        pmaxs, psums = [], []
        for c, qc in enumerate(q_cols):
            cols = slice(c * MXU_COLS, (c + 1) * MXU_COLS)
            s = jnp.dot(k, qc, preferred_element_type=F32)
            mc = m_ref[:, cols]
            pm = None
            psum = jnp.zeros((8, MXU_COLS), F32)
            for b in range(n_sub):
                rows = slice(b * SOFTMAX_ROWS, (b + 1) * SOFTMAX_ROWS)
                p = jnp.exp2(s[rows] - mc)
                pm = p if pm is None else jnp.maximum(pm, p)
                psum = psum + jnp.sum(p.reshape(SOFTMAX_ROWS // 8, 8, MXU_COLS), axis=0)
                p_ref[slot][rows, cols] = p.astype(BF16)
            pmaxs.append(jnp.max(pm, axis=0, keepdims=True))
            psums.append(psum)
        pmax = jnp.concatenate(pmaxs, axis=1)
        weighted(jnp.maximum(j - 1, 0), (slot + 2) % N_SLOTS, scale_prev)
        big = jnp.maximum(pmax, 1.0)
        scale = 1.0 / big
        st_ref[row(ST_L, 8), :] = (st_ref[row(ST_L, 8), :] + jnp.concatenate(psums, axis=1)) * scale
        st_ref[row(ST_M), :] = m_ref + jnp.log2(big)
        st_ref[row(st_scale), :] = scale
        st_ref[row(st_flag), :] = jnp.maximum(st_ref[row(st_flag), :], pmax)

    k0 = keys(0)
    m0 = jnp.concatenate([jnp.max(jnp.dot(k0, qc, preferred_element_type=F32), axis=0, keepdims=True)
                          for qc in q_cols], axis=1)
    st_ref[...] = jnp.zeros_like(st_ref)
    st_ref[row(ST_M), :] = m0
    st_ref[row(st_scale), :] = jnp.ones((1, nq), F32)
    acc_ref[...] = jnp.zeros_like(acc_ref)
    p_ref[N_SLOTS - 1][...] = jnp.zeros((tk, nq), BF16)
    n_trips = n_chunks // FAST_STEPS_PER_TRIP
    lead = n_chunks - n_trips * FAST_STEPS_PER_TRIP
    for j in range(lead):
        step(j, j % N_SLOTS)

    def body(t, carry):
        for u in range(FAST_STEPS_PER_TRIP):
            step(lead + t * FAST_STEPS_PER_TRIP + u, (lead + u) % N_SLOTS)
        return carry

    lax.fori_loop(0, n_trips, body, 0)
    weighted(n_chunks - 1, (n_chunks - 1) % N_SLOTS, st_ref[row(st_scale), :])
    return acc_ref[...], jnp.sum(st_ref[row(ST_L, 8), :], axis=0, keepdims=True), st_ref[row(st_flag), :]


def _attend(qst, k_ref, vt_ref, rest, fast):
    if not fast:
        return _flash_cols(qst, k_ref, vt_ref, rest)
    flag_ref, scratch = rest[0], rest[1:]
    acc_t, l, flag = _flash_cols_fast(qst, k_ref, vt_ref, scratch)
    flag_ref[...] = jnp.broadcast_to(flag, flag_ref.shape)
    return acc_t, l


def _attn_outs(out_spec, out_shape, grid_axes, n_i, nq, fast):
    if not fast:
        return out_spec, out_shape
    flag_spec = pl.BlockSpec((8, nq), lambda a, i: (a * n_i + i, 0))
    return [out_spec, flag_spec], [out_shape, jax.ShapeDtypeStruct((grid_axes * n_i * 8, nq), F32)]


def _gqa_kernel(q_ref, k_ref, vt_ref, o_ref, *rest, tq, fast):
    g = pl.program_id(0)
    lane = lax.broadcasted_iota(jnp.int32, (tq, LANES), 1)
    in_g = (lane // HEAD_DIM) == g
    rows = []
    for hh in range(A_GROUP):
        qp = q_ref[:, LANES * (hh // 2):LANES * (hh // 2 + 1)].astype(F32)
        aligned = jnp.where((hh % 2) == g, qp, pltpu.roll(qp, HEAD_DIM, 1))
        rows.append(jnp.where(in_g, aligned, 0.0))
    qs = jnp.concatenate(rows, axis=0)
    acc_t, l = _attend(qs.T.astype(BF16), k_ref, vt_ref, rest, fast)
    o_t = acc_t / l
    o = jnp.concatenate([o_t, jnp.zeros_like(o_t)], axis=0).T
    left_half = lane < HEAD_DIM
    for p in range(A_GROUP // 2):
        a = o[(2 * p) * tq:(2 * p + 1) * tq]
        b = o[(2 * p + 1) * tq:(2 * p + 2) * tq]
        o_ref[:, LANES * p:LANES * (p + 1)] = jnp.where(left_half, a, pltpu.roll(b, HEAD_DIM, 1)).astype(BF16)


def _gqa_call(qa, ka, vat, tq, tk, fast):
    seq = qa.shape[0]
    nq = A_GROUP * tq
    out_specs, out_shape = _attn_outs(pl.BlockSpec((tq, A_GROUP * HEAD_DIM), lambda g, i: (i, g)),
                                      jax.ShapeDtypeStruct((seq, A_Q_W), BF16), A_KV_HEADS, seq // tq, nq, fast)
    return pl.pallas_call(
        functools.partial(_gqa_kernel, tq=tq, fast=fast),
        grid=(A_KV_HEADS, seq // tq),
        in_specs=[
            pl.BlockSpec((tq, A_GROUP * HEAD_DIM), lambda g, i: (i, g)),
            pl.BlockSpec((seq, A_KV_W), lambda g, i: (0, 0)),
            pl.BlockSpec((vat.shape[0], HEAD_DIM, vat.shape[2]), lambda g, i: (0, g, 0)),
        ],
        out_specs=out_specs,
        out_shape=out_shape,
        scratch_shapes=(_flash_scratch_fast if fast else _flash_scratch)(nq, tk, HEAD_DIM),
        compiler_params=_cparams(("parallel", "parallel")),
        name="gqa_fast" if fast else "gqa",
    )(qa, ka, vat)


def _diff_kernel(lam_ref, q_ref, k_ref, vt_ref, sg_ref, o_ref, *rest, tq, lambda_init, fast):
    lane = lax.broadcasted_iota(jnp.int32, (tq, LANES), 1)
    q = q_ref[...].astype(F32)
    qs = jnp.concatenate([jnp.where(lane < HEAD_DIM, q, 0.0),
                          jnp.where(lane >= HEAD_DIM, q, 0.0)], axis=0)
    acc_t, l = _attend(qs.T.astype(BF16), k_ref, vt_ref, rest, fast)
    o = (acc_t / l).T
    lv = lam_ref[...]
    lam = (jnp.exp(jnp.sum(lv[0:1] * lv[1:2], axis=-1, keepdims=True))
           - jnp.exp(jnp.sum(lv[2:3] * lv[3:4], axis=-1, keepdims=True)) + lambda_init)
    ob = o[:tq] - lam * o[tq:]
    ob = _rms(ob, sg_ref[...], SUBLN_EPS) * (1.0 - lambda_init)
    o_ref[...] = ob.astype(BF16)


def _diff_call(lam_vecs, qb, kb, vbt, subln_g, tq, tk, lambda_init, fast):
    seq = qb.shape[0]
    n_slabs, _, slab = vbt.shape
    out_specs, out_shape = _attn_outs(pl.BlockSpec((tq, LANES), lambda h, i: (i, h)),
                                      jax.ShapeDtypeStruct((seq, B_V_W), BF16), B_HEADS, seq // tq, 2 * tq, fast)
    return pl.pallas_call(
        functools.partial(_diff_kernel, tq=tq, lambda_init=lambda_init, fast=fast),
        grid=(B_HEADS, seq // tq),
        in_specs=[
            pl.BlockSpec((4, HEAD_DIM), lambda h, i: (0, 0)),
            pl.BlockSpec((tq, LANES), lambda h, i: (i, h)),
            pl.BlockSpec((seq, LANES), lambda h, i: (0, h)),
            pl.BlockSpec((n_slabs, B_V_DIM, slab), lambda h, i: (0, h, 0)),
            pl.BlockSpec((1, B_V_DIM), lambda h, i: (0, 0)),
        ],
        out_specs=out_specs,
        out_shape=out_shape,
        scratch_shapes=(_flash_scratch_fast if fast else _flash_scratch)(2 * tq, tk, B_V_DIM),
        compiler_params=_cparams(("parallel", "parallel")),
        name="diff_fast" if fast else "diff",
    )(lam_vecs, qb, kb, vbt, subln_g)


def _mid_kernel(oa_ref, ob_ref, x_ref, wo_ref, g_ref, wq_ref, sk_ref, h_ref, xn_ref, sc_ref):
    o = jnp.concatenate([oa_ref[...], ob_ref[...]], axis=1)
    h = x_ref[...] + jnp.dot(o, wo_ref[...], preferred_element_type=F32)
    h_ref[...] = h
    xn = _rms(h, g_ref[...], NORM_EPS).astype(BF16)
    xn_ref[...] = xn
    q = jnp.dot(xn, wq_ref[...], preferred_element_type=F32).astype(BF16)
    for hp in range(2 * PEER_HEADS):
        sc_ref[hp] = lax.dot_general(sk_ref[hp], q[:, PEER_HALF * hp:PEER_HALF * (hp + 1)],
                                     NT_DIMS, preferred_element_type=F32)


def _mid_call(oa, ob, x, w_out, g, w_query, sub_keys, tm):
    seq = x.shape[0]
    row = lambda i: (i, 0)
    fix = lambda i: (0, 0)
    nq = 2 * PEER_HEADS * PEER_HALF
    return pl.pallas_call(
        _mid_kernel,
        grid=(seq // tm,),
        in_specs=[
            pl.BlockSpec((tm, A_Q_W), row),
            pl.BlockSpec((tm, B_V_W), row),
            pl.BlockSpec((tm, D_MODEL), row),
            pl.BlockSpec((D_MODEL, D_MODEL), fix),
            pl.BlockSpec((1, D_MODEL), fix),
            pl.BlockSpec((D_MODEL, nq), fix),
            pl.BlockSpec((2 * PEER_HEADS, PEER_KEYS, PEER_HALF), lambda i: (0, 0, 0)),
        ],
        out_specs=[
            pl.BlockSpec((tm, D_MODEL), row),
            pl.BlockSpec((tm, D_MODEL), row),
            pl.BlockSpec((2 * PEER_HEADS, PEER_KEYS, tm), lambda i: (0, 0, i)),
        ],
        out_shape=[
            jax.ShapeDtypeStruct((seq, D_MODEL), F32),
            jax.ShapeDtypeStruct((seq, D_MODEL), BF16),
            jax.ShapeDtypeStruct((2 * PEER_HEADS, PEER_KEYS, seq), F32),
        ],
        compiler_params=_cparams(("parallel",)),
        name="mid",
    )(oa, ob, x, w_out, g, w_query, sub_keys)


N_TOP = PEER_TOPK + 1
_CAND = [(a, b) for a in range(N_TOP) for b in range(N_TOP) if (a + 1) * (b + 1) <= N_TOP]


def _top_vals(s, n):
    vals = []
    for r in range(n):
        m = jnp.max(s, axis=0, keepdims=True)
        vals.append(m)
        if r + 1 < n:
            s = jnp.where(s >= m, -jnp.inf, s)
    return vals


def _topk_kernel(sc_ref, e1_ref, e2_ref, tau_ref):
    t = sc_ref.shape[-1]
    taus = []
    for h in range(PEER_HEADS):
        s1 = sc_ref[2 * h]
        s2 = sc_ref[2 * h + 1]
        v1 = _top_vals(s1, N_TOP)
        v2 = _top_vals(s2, N_TOP)
        cands = [v1[a] + v2[b] for (a, b) in _CAND]
        pad = (-len(cands)) % 8
        cands += [jnp.full((1, t), -jnp.inf, F32)] * pad
        c = jnp.concatenate(cands, axis=0)
        top = _top_vals(c, N_TOP)
        t16, t17 = top[PEER_TOPK - 1], top[PEER_TOPK]
        best = v1[0] + v2[0]
        z = jnp.sum(jnp.where(c >= t16, jnp.exp(c - best), 0.0), axis=0, keepdims=True)
        rz = 1.0 / z
        e1_ref[h] = jnp.exp(s1 - v1[0]) * rz
        e2_ref[h] = jnp.exp(s2 - v2[0])
        taus.append(0.5 * (jnp.exp(t16 - best) + jnp.exp(t17 - best)) * rz)
    tau_ref[...] = jnp.concatenate(taus, axis=0)


def _topk_call(sc, tt):
    seq = sc.shape[-1]
    blk = lambda i: (0, 0, i)
    return pl.pallas_call(
        _topk_kernel,
        grid=(seq // tt,),
        in_specs=[pl.BlockSpec((2 * PEER_HEADS, PEER_KEYS, tt), blk)],
        out_specs=[
            pl.BlockSpec((PEER_HEADS, PEER_KEYS, tt), blk),
            pl.BlockSpec((PEER_HEADS, PEER_KEYS, tt), blk),
            pl.BlockSpec((PEER_HEADS, tt), lambda i: (0, i)),
        ],
        out_shape=[
            jax.ShapeDtypeStruct((PEER_HEADS, PEER_KEYS, seq), F32),
            jax.ShapeDtypeStruct((PEER_HEADS, PEER_KEYS, seq), F32),
            jax.ShapeDtypeStruct((PEER_HEADS, seq), F32),
        ],
        compiler_params=_cparams(("parallel",)),
        name="topk",
    )(sc)


def _peer_kernel(xn_ref, u_ref, vt_ref, e1_ref, e2_ref, tau_ref, h_ref, g_ref, o_ref,
                 acc_ref, a_ref, w_ref, *, eb):
    e = pl.program_id(1)
    tt = xn_ref.shape[0]
    inv_sqrt2 = 1.0 / math.sqrt(2.0)
    a_ref[...] = lax.dot_general(u_ref[...], xn_ref[...], NT_DIMS,
                                 preferred_element_type=F32)
    for ii in range(eb // PEER_KEYS):
        i = e * (eb // PEER_KEYS) + ii
        rows = slice(PEER_KEYS * ii, PEER_KEYS * (ii + 1))
        e1_rows = [e1_ref[h, pl.ds(i, 1), :] for h in range(PEER_HEADS)]
        for tc in range(tt // LANES):
            cols = slice(LANES * tc, LANES * (tc + 1))
            a = a_ref[rows, cols]
            act = 0.5 * a * (1.0 + lax.erf(a * inv_sqrt2))
            gate = None
            for h in range(PEER_HEADS):
                p = e2_ref[h, :, cols] * e1_rows[h][:, cols]
                term = jnp.where(p >= tau_ref[h:h + 1, cols], p, 0.0)
                gate = term if gate is None else gate + term
            w_ref[rows, cols] = (gate * act).astype(BF16)
    part = lax.dot_general(vt_ref[...], w_ref[...], (((0,), (0,)), ((), ())), preferred_element_type=F32)

    @pl.when(e == 0)
    def _():
        acc_ref[...] = part

    @pl.when(e > 0)
    def _():
        acc_ref[...] += part

    @pl.when(e == pl.num_programs(1) - 1)
    def _():
        out = h_ref[...] + acc_ref[...].T
        o_ref[...] = _rms(out, g_ref[...], NORM_EPS)


def _peer_call(xn, u, vt, e1, e2, tau, h, g, tt, eb):
    seq = xn.shape[0]
    n_exp = u.shape[0]
    tok = lambda t, e: (t, 0)
    tok3 = lambda t, e: (0, 0, t)
    return pl.pallas_call(
        functools.partial(_peer_kernel, eb=eb),
        grid=(seq // tt, n_exp // eb),
        in_specs=[
            pl.BlockSpec((tt, D_MODEL), tok),
            pl.BlockSpec((eb, D_MODEL), lambda t, e: (e, 0)),
            pl.BlockSpec((eb, D_MODEL), lambda t, e: (e, 0)),
            pl.BlockSpec((PEER_HEADS, PEER_KEYS, tt), tok3),
            pl.BlockSpec((PEER_HEADS, PEER_KEYS, tt), tok3),
            pl.BlockSpec((PEER_HEADS, tt), lambda t, e: (0, t)),
            pl.BlockSpec((tt, D_MODEL), tok),
            pl.BlockSpec((1, D_MODEL), lambda t, e: (0, 0)),
        ],
        out_specs=pl.BlockSpec((tt, D_MODEL), tok),
        out_shape=jax.ShapeDtypeStruct((seq, D_MODEL), F32),
        scratch_shapes=[pltpu.VMEM((D_MODEL, tt), F32), pltpu.VMEM((eb, tt), F32), pltpu.VMEM((eb, tt), BF16)],
        compiler_params=pltpu.CompilerParams(dimension_semantics=("parallel", "arbitrary"),
                                             vmem_limit_bytes=VMEM_LIMIT,
                                             allow_input_fusion=[False, True, True] + [False] * 5),
        name="peer",
    )(xn, u, vt, e1, e2, tau, h, g)


def _tiles(seq):
    big = seq >= 4096
    return dict(
        tm=512 if big else 256,
        tq_a=128, tq_b=256,
        tk=1024 if big else 512,
        tt_topk=256, tt_peer=512 if big else 256, eb=2048,
    )


def kernel(x, norm_attn_g, w_in, q_norm_g, k_norm_g, lambda_q1, lambda_k1, lambda_q2, lambda_k2,
           subln_g, w_out, norm_ffn_g, w_query, sub_keys, expert_u, expert_v, norm_final_g):
    batch, seq, d = x.shape
    assert batch == 1 and d == D_MODEL and norm_attn_g.shape[0] == 1
    t = _tiles(seq)
    lambda_init = 0.8 - 0.6 * math.exp(-0.3 * 0)
    x2 = x.reshape(seq, d)

    tab = _rope_tables(seq)
    blk = np.arange(A_Q_W) // HEAD_DIM
    bd = jnp.asarray(np.where(blk[:, None] == blk[None, :], 1.0 / HEAD_DIM, 0.0), dtype=BF16)
    qg = (jnp.tile(q_norm_g[0], A_Q_HEADS) * Q_SCALE).reshape(1, A_Q_W)
    kg = jnp.tile(k_norm_g[0], A_KV_HEADS).reshape(1, A_KV_W)

    qa, ka, vat, qb, kb, vbt = _proj_call(x2, norm_attn_g, w_in[0].astype(BF16), qg, kg, tab, bd, t["tm"])
    lam_vecs = jnp.concatenate([lambda_q1, lambda_k1, lambda_q2, lambda_k2], axis=0)
    gqa = functools.partial(_gqa_call, qa, ka, vat, t["tq_a"], t["tk"])
    diff = functools.partial(_diff_call, lam_vecs, qb, kb, vbt, subln_g, t["tq_b"], t["tk"], lambda_init)
    oa, flag_a = gqa(True)
    ob, flag_b = diff(True)
    trusted = jnp.all(flag_a < P_LIMIT) & jnp.all(flag_b < P_LIMIT)
    oa, ob = lax.cond(trusted, lambda: (oa, ob), lambda: (gqa(False), diff(False)))

    sk = sub_keys[0].reshape(2 * PEER_HEADS, PEER_KEYS, PEER_HALF).astype(BF16)
    h, xn, sc = _mid_call(oa, ob, x2, w_out[0].astype(BF16), norm_ffn_g, w_query[0].astype(BF16), sk, t["tm"])
    e1, e2, tau = _topk_call(sc, t["tt_topk"])
    out = _peer_call(xn, expert_u[0].astype(BF16), expert_v[0].astype(BF16), e1, e2, tau, h,
                     norm_final_g.reshape(1, d), t["tt_peer"], t["eb"])
    return out.reshape(batch, seq, d)
```
